```python
import jax
import jax.numpy as jnp
from jax import lax
import numpy as np

D_MODEL = 2048
BATCH = 4
SEQ = 2048
DEPTH = 4
DEC_BATCH = 8
DEC_SEQ = 1
PAST_LEN = 16384
PAGE_SIZE = 128

HEAD_DIM = 64
NORM_EPS = 1e-6

RWKV_HEADS = 12
RWKV_WIDTH = RWKV_HEADS * HEAD_DIM
DECAY_LORA = 64
ICLR_LORA = 64
RWKV_COLS = 4 * RWKV_WIDTH + DECAY_LORA + ICLR_LORA
GN_EPS = 64e-5

ATT_GROUPS = ((128, 1), (512, 4), (2048, 16))
HEADS_PER_GROUP = 4
ATT_HEADS = len(ATT_GROUPS) * HEADS_PER_GROUP
ATT_WIDTH = ATT_HEADS * HEAD_DIM
ATT_COLS = 4 * ATT_WIDTH
ROPE_THETA = 500000.0
ROPE_DIMS = HEAD_DIM // 4
BLOCK = 128

POOL_WINDOWS = (2, 4, 8, 16)
POOL_GROUP = 128
POOL_WIDTH = len(POOL_WINDOWS) * POOL_GROUP
POOL_COLS = 2 * POOL_WIDTH
POOL_BUF = max(POOL_WINDOWS) - 1

D_MIX = RWKV_WIDTH + ATT_WIDTH + POOL_WIDTH
D_IN = RWKV_COLS + ATT_COLS + POOL_COLS

kernel_name = 'hybrid_rwkv7_dilated_swa_pool_step'


def rms_norm(x, w, eps=NORM_EPS):
    xf = x.astype(jnp.float32)
    y = xf * lax.rsqrt(jnp.mean(xf * xf, axis=-1, keepdims=True) + eps)
    return (y * w.astype(jnp.float32)).astype(x.dtype)


def rope(x, pos):
    half = ROPE_DIMS // 2
    inv = jnp.power(jnp.float32(ROPE_THETA), -jnp.arange(half, dtype=jnp.float32) * 2.0 / ROPE_DIMS)
    ang = pos[:, None] * inv[None, :]
    cos = jnp.cos(ang)[None, :, None, :]
    sin = jnp.sin(ang)[None, :, None, :]
    x1 = x[..., :half]
    x2 = x[..., half:ROPE_DIMS]
    return jnp.concatenate([x1 * cos - x2 * sin, x2 * cos + x1 * sin, x[..., ROPE_DIMS:]], axis=-1)


def rwkv_time_mix(z, z_prev0, s0, mu, w0, w_up, a0, a_up, k_k, k_a, r_k, ln_w, ln_b):
    B, T, _ = z.shape
    W = RWKV_WIDTH
    z_prev = jnp.concatenate([z_prev0[:, None], z[:, :-1]], axis=1)
    zs = z + (z_prev - z) * mu
    r, k, v, g = (zs[..., i * W:(i + 1) * W] for i in range(4))
    w_dn = zs[..., 4 * W:4 * W + DECAY_LORA]
    a_dn = zs[..., 4 * W + DECAY_LORA:]
    w_log = -jax.nn.softplus(-(w0 + jnp.tanh(w_dn) @ w_up)) - 0.5
    decay = jnp.exp(-jnp.exp(w_log))
    a = jax.nn.sigmoid(a0 + a_dn @ a_up)
    heads = lambda t: t.reshape(B, T, RWKV_HEADS, HEAD_DIM)
    kk = heads(k * k_k)
    kk = kk / jnp.maximum(jnp.sqrt(jnp.sum(kk * kk, axis=-1, keepdims=True)), 1e-12)
    k = k * (1.0 + (a - 1.0) * k_a)
    r, k, v, decay, a = heads(r), heads(k), heads(v), heads(decay), heads(a)

    def step(S, inp):
        r_t, w_t, k_t, v_t, kk_t, a_t = inp
        sa = jnp.einsum('bhvk,bhk->bhv', S, -kk_t)
        S = (S * w_t[:, :, None, :] + sa[..., None] * (kk_t * a_t)[:, :, None, :]
             + v_t[..., None] * k_t[:, :, None, :])
        return S, jnp.einsum('bhvk,bhk->bhv', S, r_t)

    xs = tuple(jnp.moveaxis(t, 1, 0) for t in (r, decay, k, v, kk, a))
    s_last, o = lax.scan(step, s0, xs)
    o = jnp.moveaxis(o, 0, 1)
    mean = jnp.mean(o, axis=-1, keepdims=True)
    var = jnp.mean(jnp.square(o - mean), axis=-1, keepdims=True)
    o = ((o - mean) * lax.rsqrt(var + GN_EPS)).reshape(B, T, W) * ln_w + ln_b
    o = o + (jnp.sum(r * k * r_k, axis=-1, keepdims=True) * v).reshape(B, T, W)
    return o * jax.nn.silu(g), s_last


def dilated_band_attention(q, k, v, window, dil):
    B, S, H, E = q.shape
    span = window // dil
    n_sub = -(-S // dil)
    n_blk = -(-n_sub // BLOCK)
    pad = n_blk * BLOCK * dil - S

    def to_blocks(t):
        t = jnp.pad(t, ((0, 0), (0, pad), (0, 0), (0, 0)))
        return t.reshape(B, n_blk, BLOCK, dil, H, E)

    def with_prev(t):
        prev = jnp.pad(t[:, :-1], ((0, 0), (1, 0), (0, 0), (0, 0), (0, 0), (0, 0)))
        return jnp.concatenate([prev, t], axis=2)

    qb = to_blocks(q)
    kc = with_prev(to_blocks(k))
    vc = with_prev(to_blocks(v))
    s = jnp.einsum('bnidhe,bnjdhe->bndhij', qb, kc) * (E ** -0.5)
    qi = jnp.arange(BLOCK)[:, None]
    kj = jnp.arange(2 * BLOCK)[None, :]
    dist = BLOCK + qi - kj
    band = (dist >= 0) & (dist <= span)
    has_prev = (jnp.arange(n_blk) > 0)[:, None, None] | (kj >= BLOCK)[None]
    mask = band[None] & has_prev
    s = jnp.where(mask[None, :, None, None], s, -jnp.inf)
    m = jnp.max(s, axis=-1, keepdims=True)
    p = jnp.exp(s - m)
    l = jnp.sum(p, axis=-1, keepdims=True)
    o = jnp.einsum('bndhij,bnjdhe->bndhie', p / l, vc)
    lse = (m + jnp.log(l))[..., 0]
    o = jnp.transpose(o, (0, 1, 4, 2, 3, 5)).reshape(B, n_blk * BLOCK * dil, H, E)[:, :S]
    lse = jnp.transpose(lse, (0, 1, 4, 2, 3)).reshape(B, n_blk * BLOCK * dil, H)[:, :S]
    return o, lse


def dilated_gather_attention(q, k_all, v_all, n_buf, window, dil):
    T, E = q.shape[1], q.shape[-1]
    span = window // dil
    idx = n_buf + jnp.arange(T)[:, None] - dil * jnp.arange(span + 1)[None, :]
    valid = idx >= 0
    idx = jnp.maximum(idx, 0)
    kg = k_all[:, idx]
    vg = v_all[:, idx]
    s = jnp.einsum('bthe,btjhe->bthj', q, kg) * (E ** -0.5)
    s = jnp.where(valid[None, :, None, :], s, -jnp.inf)
    m = jnp.max(s, axis=-1, keepdims=True)
    p = jnp.exp(s - m)
    l = jnp.sum(p, axis=-1, keepdims=True)
    o = jnp.einsum('bthj,btjhe->bthe', p / l, vg)
    return o, (m + jnp.log(l))[..., 0]


def attention_branch(za, pos, kv_bufs, qn_w, kn_w):
    B, T, _ = za.shape
    q, k, v, g = (za[..., i * ATT_WIDTH:(i + 1) * ATT_WIDTH] for i in range(4))
    heads = lambda t: t.reshape(B, T, ATT_HEADS, HEAD_DIM)
    q = rope(rms_norm(heads(q), qn_w), pos)
    k = rope(rms_norm(heads(k), kn_w), pos)
    v = heads(v)
    outs, lses, new_kv = [], [], []
    for gi, (window, dil) in enumerate(ATT_GROUPS):
        hs = slice(gi * HEADS_PER_GROUP, (gi + 1) * HEADS_PER_GROUP)
        qg, kg, vg = q[:, :, hs], k[:, :, hs], v[:, :, hs]
        if kv_bufs is None:
            o, lse = dilated_band_attention(qg, kg, vg, window, dil)
            keep = min(window, T)
            new_kv.append((kg[:, T - keep:], vg[:, T - keep:]))
        else:
            k_buf, v_buf = kv_bufs[gi]
            k_all = jnp.concatenate([k_buf.astype(jnp.float32), kg], axis=1)
            v_all = jnp.concatenate([v_buf.astype(jnp.float32), vg], axis=1)
            o, lse = dilated_gather_attention(qg, k_all, v_all, k_buf.shape[1], window, dil)
            new_kv.append((kg, vg))
        outs.append(o)
        lses.append(lse)
    alpha = jax.nn.softmax(jnp.stack(lses, axis=2), axis=2)
    o = (jnp.stack(outs, axis=2) * alpha[..., None]).reshape(B, T, ATT_WIDTH)
    return o * jax.nn.silu(g), new_kv


def pool_branch(zp, pool_prev, pos, pool_w, pool_scale):
    B, T, _ = zp.shape
    u, g = zp[..., :POOL_WIDTH], zp[..., POOL_WIDTH:]
    u_ext = u if pool_prev is None else jnp.concatenate([pool_prev.astype(jnp.float32), u], axis=1)
    n_prev = u_ext.shape[1] - T
    cs = jnp.pad(jnp.cumsum(u_ext, axis=1), ((0, 0), (1, 0), (0, 0)))
    row = n_prev + jnp.arange(T) + 1
    hi = cs[:, row]
    means = []
    for gi, w in enumerate(POOL_WINDOWS):
        ch = slice(gi * POOL_GROUP, (gi + 1) * POOL_GROUP)
        lo = cs[:, jnp.maximum(row - w, 0), ch]
        cnt = jnp.minimum(jnp.float32(w), pos + 1.0)
        means.append((hi[..., ch] - lo) / cnt[None, :, None])
    d = jnp.concatenate(means, axis=-1) - u
    d = jnp.einsum('btgc,gcd->btgd', d.reshape(B, T, len(POOL_WINDOWS), POOL_GROUP), pool_w)
    d = d.reshape(B, T, POOL_WIDTH) * pool_scale
    return d * jax.nn.silu(g), u_ext[:, -POOL_BUF:]


def trunk_layer(x, pos, shift0, wkv0, pool_prev, kv_bufs, norm_w, w_in, w_out, mu, w0, w_up, a0, a_up,
                k_k, k_a, r_k, ln_w, ln_b, qn_w, kn_w, pool_w, pool_scale):
    h = rms_norm(x, norm_w)
    z = jnp.einsum('btd,dc->btc', h, w_in).astype(jnp.float32)
    z_a = z[..., :RWKV_COLS]
    z_b = z[..., RWKV_COLS:RWKV_COLS + ATT_COLS]
    z_c = z[..., RWKV_COLS + ATT_COLS:]
    y_a, wkv_new = rwkv_time_mix(z_a, shift0, wkv0, mu, w0, w_up, a0, a_up, k_k, k_a, r_k, ln_w, ln_b)
    y_b, kv_new = attention_branch(z_b, pos, kv_bufs, qn_w, kn_w)
    y_c, pool_new = pool_branch(z_c, pool_prev, pos, pool_w, pool_scale)
    mix = jnp.concatenate([y_a, y_b, y_c], axis=-1).astype(x.dtype)
    y = x + jnp.einsum('btc,cd->btd', mix, w_out)
    return y, z_a[:, -1], wkv_new, pool_new, kv_new


def stack_states(per_layer, dt):
    st = lambda f: jnp.stack([f(s) for s in per_layer]).astype(dt)
    wkv = st(lambda s: s[1])
    shift = st(lambda s: s[0])
    pool = st(lambda s: s[2])
    kv = [st(lambda s, g=g, j=j: s[3][g][j]) for g in range(len(ATT_GROUPS)) for j in range(2)]
    return (wkv, shift, pool, *kv)


def setup_inputs(seed: int = 0) -> dict:
    key = jax.random.key(seed)
    ks = jax.random.split(key, 28)
    f32 = jnp.float32
    nrm = lambda k, shape, s: jax.random.normal(k, shape, f32) * s
    near_one = lambda k, shape: 1.0 + nrm(k, shape, 0.02)
    lens = [min(w, PAST_LEN) for w, _ in ATT_GROUPS]
    kv_shape = lambda n: (DEPTH, DEC_BATCH, n, HEADS_PER_GROUP, HEAD_DIM)
    return {
        'x_prompt': nrm(ks[0], (BATCH, SEQ, D_MODEL), 1.0),
        'x_sample': nrm(ks[1], (DEC_BATCH, DEC_SEQ, D_MODEL), 1.0),
        'state_wkv': nrm(ks[2], (DEPTH, DEC_BATCH, RWKV_HEADS, HEAD_DIM, HEAD_DIM), 1.0),
        'state_shift': nrm(ks[3], (DEPTH, DEC_BATCH, RWKV_COLS), 1.0),
        'state_pool': nrm(ks[4], (DEPTH, DEC_BATCH, POOL_BUF, POOL_WIDTH), 1.0),
        'cache_k_w128': nrm(ks[5], kv_shape(lens[0]), 1.0),
        'cache_v_w128': nrm(ks[6], kv_shape(lens[0]), 1.0),
        'cache_k_w512': nrm(ks[7], kv_shape(lens[1]), 1.0),
        'cache_v_w512': nrm(ks[8], kv_shape(lens[1]), 1.0),
        'cache_k_w2048': nrm(ks[9], kv_shape(lens[2]), 1.0),
        'cache_v_w2048': nrm(ks[10], kv_shape(lens[2]), 1.0),
        'norm_w': near_one(ks[11], (DEPTH, D_MODEL)),
        'w_in': nrm(ks[12], (DEPTH, D_MODEL, D_IN), D_MODEL ** -0.5),
        'w_out': nrm(ks[13], (DEPTH, D_MIX, D_MODEL), D_MIX ** -0.5),
        'rwkv_mu': jax.random.uniform(ks[14], (DEPTH, RWKV_COLS), f32),
        'rwkv_w0': jax.random.uniform(ks[15], (DEPTH, RWKV_WIDTH), f32, -6.0, -1.0),
        'rwkv_w_up': nrm(ks[16], (DEPTH, DECAY_LORA, RWKV_WIDTH), 0.5 * DECAY_LORA ** -0.5),
        'rwkv_a0': nrm(ks[17], (DEPTH, RWKV_WIDTH), 0.1),
        'rwkv_a_up': nrm(ks[18], (DEPTH, ICLR_LORA, RWKV_WIDTH), 0.5 * ICLR_LORA ** -0.5),
        'rwkv_k_k': 0.85 + nrm(ks[19], (DEPTH, RWKV_WIDTH), 0.02),
        'rwkv_k_a': near_one(ks[20], (DEPTH, RWKV_WIDTH)),
        'rwkv_r_k': nrm(ks[21], (DEPTH, RWKV_HEADS, HEAD_DIM), 0.1),
        'rwkv_ln_w': near_one(ks[22], (DEPTH, RWKV_WIDTH)),
        'rwkv_ln_b': nrm(ks[23], (DEPTH, RWKV_WIDTH), 0.02),
        'q_norm_w': near_one(ks[24], (DEPTH, HEAD_DIM)),
        'k_norm_w': near_one(ks[25], (DEPTH, HEAD_DIM)),
        'pool_w': nrm(ks[26], (DEPTH, len(POOL_WINDOWS), POOL_GROUP, POOL_GROUP), POOL_GROUP ** -0.5),
        'pool_scale': near_one(ks[27], (DEPTH, POOL_WIDTH)),
    }


def reference(x_prompt, x_sample, state_wkv, state_shift, state_pool,
              cache_k_w128, cache_v_w128, cache_k_w512, cache_v_w512, cache_k_w2048, cache_v_w2048,
              norm_w, w_in, w_out, rwkv_mu, rwkv_w0, rwkv_w_up, rwkv_a0, rwkv_a_up,
              rwkv_k_k, rwkv_k_a, rwkv_r_k, rwkv_ln_w, rwkv_ln_b, q_norm_w, k_norm_w, pool_w, pool_scale):
    dt = x_prompt.dtype
    B, S, _ = x_prompt.shape
    T = x_sample.shape[1]
    pos_p = jnp.arange(S, dtype=jnp.float32)
    pos_s = PAST_LEN + jnp.arange(T, dtype=jnp.float32)
    cache_k = (cache_k_w128, cache_k_w512, cache_k_w2048)
    cache_v = (cache_v_w128, cache_v_w512, cache_v_w2048)
    zero_shift = jnp.zeros((B, RWKV_COLS), jnp.float32)
    zero_wkv = jnp.zeros((B, RWKV_HEADS, HEAD_DIM, HEAD_DIM), jnp.float32)
    hp, hs = x_prompt, x_sample
    p_layers, s_layers = [], []
    for l in range(DEPTH):
        lw = (norm_w[l], w_in[l], w_out[l], rwkv_mu[l], rwkv_w0[l], rwkv_w_up[l], rwkv_a0[l], rwkv_a_up[l],
              rwkv_k_k[l], rwkv_k_a[l], rwkv_r_k[l], rwkv_ln_w[l], rwkv_ln_b[l], q_norm_w[l], k_norm_w[l],
              pool_w[l], pool_scale[l])
        hp, p_shift_l, p_wkv_l, p_pool_l, p_kv_l = trunk_layer(hp, pos_p, zero_shift, zero_wkv, None, None, *lw)
        bufs = [(cache_k[g][l], cache_v[g][l]) for g in range(len(ATT_GROUPS))]
        hs, s_shift_l, s_wkv_l, s_pool_l, s_kv_l = trunk_layer(
            hs, pos_s, state_shift[l].astype(jnp.float32), state_wkv[l].astype(jnp.float32),
            state_pool[l], bufs, *lw)
        p_layers.append((p_shift_l, p_wkv_l, p_pool_l, p_kv_l))
        s_layers.append((s_shift_l, s_wkv_l, s_pool_l, s_kv_l))
    p_wkv, p_shift, p_pool, p_k128, p_v128, p_k512, p_v512, p_k2048, p_v2048 = stack_states(p_layers, dt)
    s_wkv, s_shift, s_pool, s_k128, s_v128, s_k512, s_v512, s_k2048, s_v2048 = stack_states(s_layers, dt)
    return (hp, hs,
            p_wkv, p_shift, p_pool, p_k128, p_v128, p_k512, p_v512, p_k2048, p_v2048,
            s_wkv, s_shift, s_pool, s_k128, s_v128, s_k512, s_v512, s_k2048, s_v2048)
```

```python
import functools

import jax
import jax.numpy as jnp
from jax import lax
from jax.experimental import pallas as pl
from jax.experimental.pallas import tpu as pltpu

f32 = jnp.float32
bf16 = jnp.bfloat16

HEAD = 64
LANES = 128
NORM_EPS = 1e-6
GN_EPS = 64e-5
RWKV_HEADS = 12
RWKV_W = RWKV_HEADS * HEAD
LORA = 64
RWKV_COLS = 4 * RWKV_W + 2 * LORA
ATT_GROUPS = ((128, 1), (512, 4), (2048, 16))
ATT_W = 12 * HEAD
ATT_COLS = 4 * ATT_W
ROPE_THETA = 500000.0
ROPE_DIMS = 16
QBLK = 128
POOL_WINDOWS = (2, 4, 8, 16)
POOL_W = 512
POOL_BUF = 15
CHUNK = 64
VMEM_LIMIT = 56 * 1024 * 1024

ATT_BLK0 = RWKV_COLS // LANES
POOL_BLK0 = (RWKV_COLS + ATT_COLS) // LANES


def _dot(a, b):
    return jnp.dot(a, b, preferred_element_type=f32)


def _dot_nt(a, b):
    return lax.dot_general(a, b, (((1,), (1,)), ((), ())), preferred_element_type=f32)


def _dot_tn(a, b):
    return lax.dot_general(a, b, (((0,), (0,)), ((), ())), preferred_element_type=f32)


def _split2(x):
    hi = x.astype(bf16)
    lo = (x - hi.astype(f32)).astype(bf16)
    return hi, lo


def _split3(x):
    hi = x.astype(bf16)
    r1 = x - hi.astype(f32)
    mid = r1.astype(bf16)
    lo = (r1 - mid.astype(f32)).astype(bf16)
    return hi, mid, lo


def _silu(x):
    return x * jax.nn.sigmoid(x)


def _rms(x, w):
    ms = jnp.mean(x * x, axis=-1, keepdims=True)
    return x * lax.rsqrt(ms + NORM_EPS) * w


def _pair_consts(n):
    lane = lax.broadcasted_iota(jnp.int32, (n, LANES), 1)
    return lane < HEAD


def _ones_blockdiag():
    r = lax.broadcasted_iota(jnp.int32, (LANES, LANES), 0)
    c = lax.broadcasted_iota(jnp.int32, (LANES, LANES), 1)
    return jnp.where((r < HEAD) == (c < HEAD), 1.0, 0.0).astype(bf16)


def _segsum(x, ones_bd):
    hi, lo = _split2(x)
    return _dot(hi, ones_bd) + _dot(lo, ones_bd)


def _inproj_body(xp_ref, xs_ref, nw_ref, w_ref, zp_ref, zs_ref, h_scr, *, tn):
    i = pl.program_id(0)
    j = pl.program_id(1)

    @pl.when(j == 0)
    def _():
        h_scr[...] = _rms(xp_ref[...], nw_ref[...]).astype(bf16)

    w = w_ref[...].astype(bf16)
    zp_ref[...] = _dot(h_scr[...], w)

    @pl.when(i == 0)
    def _():
        hs = _rms(xs_ref[...], nw_ref[...]).astype(bf16)
        col = pl.multiple_of(j * tn, LANES)
        zs_ref[:, pl.ds(col, tn)] = _dot(hs, w)


def _inproj(xp, xs, nw, w, *, tm=1024, tn=384):
    m, d = xp.shape
    n = w.shape[1]
    nb = xs.shape[0]
    return pl.pallas_call(
        functools.partial(_inproj_body, tn=tn),
        grid=(m // tm, n // tn),
        in_specs=[
            pl.BlockSpec((tm, d), lambda i, j: (i, 0)),
            pl.BlockSpec((nb, d), lambda i, j: (0, 0)),
            pl.BlockSpec((1, d), lambda i, j: (0, 0)),
            pl.BlockSpec((d, tn), lambda i, j: (0, j)),
        ],
        out_specs=[
            pl.BlockSpec((tm, tn), lambda i, j: (i, j)),
            pl.BlockSpec((nb, n), lambda i, j: (0, 0)),
        ],
        out_shape=[jax.ShapeDtypeStruct((m, n), f32), jax.ShapeDtypeStruct((nb, n), f32)],
        scratch_shapes=[pltpu.VMEM((tm, d), bf16)],
        compiler_params=pltpu.CompilerParams(
            dimension_semantics=("arbitrary", "arbitrary"), vmem_limit_bytes=VMEM_LIMIT),
        name="inproj",
    )(xp, xs, nw, w)


def _outproj_body(*refs, tn, widths):
    n = len(widths)
    xp_ref = refs[0]
    y_refs = refs[1:1 + n]
    xs_ref, ms_ref, w_ref, op_ref, os_ref = refs[1 + n:]
    i = pl.program_id(0)
    j = pl.program_id(1)
    w = w_ref[...].astype(bf16)
    acc = None
    row = 0
    for y_ref, width in zip(y_refs, widths):
        part = _dot(y_ref[...], w[row:row + width])
        acc = part if acc is None else acc + part
        row += width
    op_ref[...] = xp_ref[...] + acc

    @pl.when(i == 0)
    def _():
        col = pl.multiple_of(j * tn, LANES)
        os_ref[:, pl.ds(col, tn)] = xs_ref[:, pl.ds(col, tn)] + _dot(ms_ref[...].astype(bf16), w)


def _outproj(xp, ys, xs, ms, w, *, tm=1024, tn=512):
    m, d = xp.shape
    nb = xs.shape[0]
    dm = w.shape[0]
    widths = tuple(y.shape[1] for y in ys)
    assert sum(widths) == dm
    return pl.pallas_call(
        functools.partial(_outproj_body, tn=tn, widths=widths),
        grid=(m // tm, d // tn),
        in_specs=[pl.BlockSpec((tm, tn), lambda i, j: (i, j))]
        + [pl.BlockSpec((tm, width), lambda i, j: (i, 0)) for width in widths]
        + [
            pl.BlockSpec((nb, d), lambda i, j: (0, 0)),
            pl.BlockSpec((nb, dm), lambda i, j: (0, 0)),
            pl.BlockSpec((dm, tn), lambda i, j: (0, j)),
        ],
        out_specs=[
            pl.BlockSpec((tm, tn), lambda i, j: (i, j)),
            pl.BlockSpec((nb, d), lambda i, j: (0, 0)),
        ],
        out_shape=[jax.ShapeDtypeStruct((m, d), f32), jax.ShapeDtypeStruct((nb, d), f32)],
        compiler_params=pltpu.CompilerParams(
            dimension_semantics=("arbitrary", "arbitrary"), vmem_limit_bytes=VMEM_LIMIT),
        name="outproj",
    )(xp, *ys, xs, ms, w)


_P_MU, _P_W0, _P_A0, _P_KK, _P_KA, _P_RK, _P_LNW, _P_LNB = 0, 5, 6, 7, 8, 9, 10, 11


def _rwkv_chunk(at, bt, kt, rt, vc, wc, st, consts):
    m0, gmask, bdmask, eye = consts
    m0w = jnp.concatenate([m0, m0], axis=1)
    z64 = jnp.zeros((CHUNK, LANES), f32)

    def sel0(x, m):
        return jnp.where(m, x, 0.0)

    def sel1(x, m):
        return jnp.where(m, 0.0, x)

    lhs = jnp.concatenate([at, rt], axis=0).astype(bf16)
    rhs = jnp.concatenate([sel0(bt, m0), sel1(bt, m0), sel0(kt, m0), sel1(kt, m0)], axis=0).astype(bf16)
    g = jnp.where(gmask, _dot_nt(lhs, rhs), 0.0)
    p = g[0:CHUNK, 0:LANES]
    ak = g[0:CHUNK, LANES:]
    rb = g[CHUNK:, 0:LANES]
    rk = g[CHUNK:, LANES:]

    vs = jnp.concatenate([sel0(vc, m0), sel1(vc, m0)], axis=0).astype(bf16)
    av = _dot(ak.astype(bf16), vs)
    x = jnp.concatenate([av, at], axis=1)
    for lev in range(6):
        xx = jnp.concatenate([sel0(x, m0w), sel1(x, m0w)], axis=0)
        if lev < 5:
            bd = jnp.concatenate([sel0(p, m0), sel1(p, m0)], axis=0)
            out = _dot(p.astype(bf16), jnp.concatenate([bd, xx], axis=1).astype(bf16))
            p = out[:, 0:LANES]
            x = x + out[:, LANES:]
        else:
            x = x + _dot(p.astype(bf16), xx.astype(bf16))
    uv = x[:, 0:LANES]
    ap = x[:, LANES:]

    tnl = jnp.concatenate([bt * wc, kt * wc], axis=0).astype(bf16)
    tnr = jnp.concatenate(
        [jnp.concatenate([ap, uv], axis=1), jnp.concatenate([z64, vc], axis=1)], axis=0).astype(bf16)
    mn = jnp.where(bdmask, _dot_tn(tnl, tnr), 0.0)
    mc = mn[:, 0:LANES] + jnp.where(eye, jnp.broadcast_to(wc, (LANES, LANES)), 0.0)
    nc = mn[:, LANES:]

    l2 = jnp.concatenate([rb, rk], axis=1).astype(bf16)
    r2 = jnp.concatenate([
        jnp.concatenate([sel0(ap, m0), sel0(uv, m0)], axis=1),
        jnp.concatenate([sel1(ap, m0), sel1(uv, m0)], axis=1),
        jnp.concatenate([z64, sel0(vc, m0)], axis=1),
        jnp.concatenate([z64, sel1(vc, m0)], axis=1)], axis=0).astype(bf16)
    ro = _dot(l2, r2)
    rp = rt + ro[:, 0:LANES]
    ov = ro[:, LANES:]

    seq = _dot(jnp.concatenate([rp, mc], axis=0).astype(bf16), st.astype(bf16))
    return seq[0:CHUNK] + ov, seq[CHUNK:] + nc


def _rwkv_body(zr_ref, zk_ref, zv_ref, zg_ref, zl_ref, pp_ref, wl_ref, y_ref, s_ref,
               sh_scr, st_scr, *, tt):
    tb = pl.program_id(2)
    nt = pl.num_programs(2)
    zrefs = (zr_ref, zk_ref, zv_ref, zg_ref, zl_ref)

    @pl.when(tb == 0)
    def _():
        st_scr[...] = jnp.zeros_like(st_scr)
        for i in range(5):
            sh_scr[i, 7:8, :] = jnp.zeros((1, LANES), f32)

    @pl.when(tb > 0)
    def _():
        for i in range(5):
            sh_scr[i, 7:8, :] = sh_scr[i, 7 + tt:8 + tt, :]

    pp = pp_ref[...]
    zs = []
    for i in range(5):
        z = zrefs[i][0]
        sh_scr[i, 8:8 + tt, :] = z
        zprev = sh_scr[i, pl.ds(7, tt), :]
        zs.append(z + (zprev - z) * pp[_P_MU + i:_P_MU + i + 1])
    r, k, v, g, lo = zs

    m0t = _pair_consts(tt)
    ones_bd = _ones_blockdiag()
    xl = jnp.where(m0t, jnp.tanh(lo), lo)
    la = _dot(xl.astype(bf16), wl_ref[...].astype(bf16))
    w_log = -jax.nn.softplus(-(pp[_P_W0:_P_W0 + 1] + la[:, 0:LANES])) - 0.5
    ld = -jnp.exp(w_log)
    a = jax.nn.sigmoid(pp[_P_A0:_P_A0 + 1] + la[:, LANES:])
    kk = k * pp[_P_KK:_P_KK + 1]
    kk = kk / jnp.maximum(jnp.sqrt(_segsum(kk * kk, ones_bd)), 1e-12)
    k2 = k * (1.0 + (a - 1.0) * pp[_P_KA:_P_KA + 1])
    beta = kk * a
    bonus = _segsum(r * k2 * pp[_P_RK:_P_RK + 1], ones_bd) * v

    m0 = _pair_consts(CHUNK)
    gr = lax.broadcasted_iota(jnp.int32, (2 * CHUNK, 2 * LANES), 0)
    gc = lax.broadcasted_iota(jnp.int32, (2 * CHUNK, 2 * LANES), 1)
    gt = gr % CHUNK
    gs = gc % CHUNK
    gmask = (gt > gs) | ((gr >= CHUNK) & (gt == gs))
    br = lax.broadcasted_iota(jnp.int32, (LANES, 2 * LANES), 0)
    bc = lax.broadcasted_iota(jnp.int32, (LANES, 2 * LANES), 1)
    bdmask = (br < HEAD) == ((bc % LANES) < HEAD)
    er = lax.broadcasted_iota(jnp.int32, (LANES, LANES), 0)
    ec = lax.broadcasted_iota(jnp.int32, (LANES, LANES), 1)
    eye = er == ec
    consts = (m0, gmask, bdmask, eye)
    tr = lax.broadcasted_iota(jnp.int32, (CHUNK, CHUNK), 0)
    tc = lax.broadcasted_iota(jnp.int32, (CHUNK, CHUNK), 1)
    tril = jnp.where(tc <= tr, 1.0, 0.0).astype(bf16)

    st = st_scr[...]
    outs = []
    for c in range(tt // CHUNK):
        sl = slice(c * CHUNK, (c + 1) * CHUNK)
        ld_c = ld[sl]
        h3 = _split3(ld_c)
        cl = _dot(tril, h3[0]) + _dot(tril, h3[1]) + _dot(tril, h3[2])
        e_in = jnp.exp(cl)
        e_ex = jnp.exp(cl - ld_c)
        e_neg = jnp.exp(-cl)
        at = -kk[sl] * e_ex
        bt = beta[sl] * e_neg
        kt = k2[sl] * e_neg
        rt = r[sl] * e_in
        o_c, st = _rwkv_chunk(at, bt, kt, rt, v[sl], e_in[CHUNK - 1:CHUNK], st, consts)
        outs.append(o_c)
    st_scr[...] = st
    o = jnp.concatenate(outs, axis=0)

    mean = _segsum(o, ones_bd) * (1.0 / HEAD)
    dl = o - mean
    var = _segsum(dl * dl, ones_bd) * (1.0 / HEAD)
    on = dl * lax.rsqrt(var + GN_EPS) * pp[_P_LNW:_P_LNW + 1] + pp[_P_LNB:_P_LNB + 1] + bonus
    y_ref[0] = (on * _silu(g)).astype(bf16)

    @pl.when(tb == nt - 1)
    def _():
        stt = st.T
        s_ref[0, 0] = stt[0:HEAD, 0:HEAD]
        s_ref[0, 1] = stt[HEAD:, HEAD:]


def _rwkv_prompt(z3, pp, wl, *, tt=256):
    b, t, _ = z3.shape
    npair = RWKV_HEADS // 2

    def zspec(off):
        return pl.BlockSpec((1, tt, LANES), lambda bi, p, tb, off=off: (bi, tb, off + p))

    lora_blk = 4 * RWKV_W // LANES
    return pl.pallas_call(
        functools.partial(_rwkv_body, tt=tt),
        grid=(b, npair, t // tt),
        in_specs=[
            zspec(0), zspec(npair), zspec(2 * npair), zspec(3 * npair),
            pl.BlockSpec((1, tt, LANES), lambda bi, p, tb: (bi, tb, lora_blk)),
            pl.BlockSpec((None, 12, LANES), lambda bi, p, tb: (p, 0, 0)),
            pl.BlockSpec((None, LANES, 2 * LANES), lambda bi, p, tb: (p, 0, 0)),
        ],
        out_specs=[
            pl.BlockSpec((1, tt, LANES), lambda bi, p, tb: (bi, tb, p)),
            pl.BlockSpec((1, 2, HEAD, HEAD), lambda bi, p, tb: (bi, p, 0, 0)),
        ],
        out_shape=[jax.ShapeDtypeStruct((b, t, RWKV_W), bf16),
                   jax.ShapeDtypeStruct((b, RWKV_HEADS, HEAD, HEAD), f32)],
        scratch_shapes=[pltpu.VMEM((5, tt + 8, LANES), f32), pltpu.VMEM((LANES, LANES), f32)],
        compiler_params=pltpu.CompilerParams(
            dimension_semantics=("arbitrary", "arbitrary", "arbitrary"), vmem_limit_bytes=VMEM_LIMIT),
        name="rwkv_prompt",
    )(z3, z3, z3, z3, z3, pp, wl)


def _rope_pair(x, cos, sa, sb):
    return x * cos + pltpu.roll(x, LANES - ROPE_DIMS // 2, 1) * sa + pltpu.roll(x, ROPE_DIMS // 2, 1) * sb


def _attn_body(*refs, t):
    qkvg = [refs[4 * gi:4 * gi + 4] for gi in range(3)]
    cos_ref, sa_ref, sb_ref, qnw_ref, knw_ref = refs[12:17]
    y_refs = refs[17:20]
    kv_refs = [refs[20 + 2 * gi:22 + 2 * gi] for gi in range(3)]
    qn_scr, kn_scr, o_scr, lse_scr = refs[26:30]

    rb = 256
    ones_bd = _ones_blockdiag()
    m0q = _pair_consts(QBLK)
    scale = HEAD ** -0.5

    for gi, (window, dil) in enumerate(ATT_GROUPS):
        q_ref, k_ref, v_ref, g_ref = qkvg[gi]
        pk_ref, pv_ref = kv_refs[gi]
        keep = min(window, t)

        def norm_rows(i, carry, q_ref=q_ref, k_ref=k_ref):
            rows = pl.ds(pl.multiple_of(i * rb, rb), rb)
            cos, sa, sb = cos_ref[rows, :], sa_ref[rows, :], sb_ref[rows, :]
            for src, nw, dst in ((q_ref, qnw_ref, qn_scr), (k_ref, knw_ref, kn_scr)):
                x = src[0, rows, :]
                ms = _segsum(x * x, ones_bd) * (1.0 / HEAD)
                xn = x * lax.rsqrt(ms + NORM_EPS) * nw[...]
                dst[rows, :] = _rope_pair(xn, cos, sa, sb)
            return carry

        lax.fori_loop(0, t // rb, norm_rows, 0)
        pk_ref[0] = kn_scr[t - keep:t, :]
        pv_ref[0] = v_ref[0, t - keep:t, :]

        n_sub = t // dil
        n_blk = n_sub // QBLK

        def tile(q0, k0, nk, first, v_ref=v_ref, dil=dil, gi=gi):
            qt = qn_scr[pl.ds(q0, QBLK, stride=dil), :]
            kt = kn_scr[pl.ds(k0, nk, stride=dil), :].astype(bf16)
            vt = v_ref[0, pl.ds(k0, nk, stride=dil), :].astype(bf16)
            qi = lax.broadcasted_iota(jnp.int32, (QBLK, nk), 0)
            kj = lax.broadcasted_iota(jnp.int32, (QBLK, nk), 1)
            mask = (kj <= qi) if first else ((kj >= qi) & (kj <= qi + QBLK))
            res = []
            for hh in range(2):
                qh = jnp.where(m0q, qt, 0.0) if hh == 0 else jnp.where(m0q, 0.0, qt)
                s = _dot_nt(qh.astype(bf16), kt) * scale
                s = jnp.where(mask, s, -jnp.inf)
                m = jnp.max(s, axis=-1, keepdims=True)
                p = jnp.exp(s - m)
                l = jnp.sum(p, axis=-1, keepdims=True)
                oh = _dot(p.astype(bf16), vt) / l
                res.append((oh, m + jnp.log(l)))
            o = jnp.where(m0q, res[0][0], res[1][0])
            lse = jnp.where(m0q, jnp.broadcast_to(res[0][1], (QBLK, LANES)),
                            jnp.broadcast_to(res[1][1], (QBLK, LANES)))
            o_scr[gi, pl.ds(q0, QBLK, stride=dil), :] = o
            lse_scr[gi, pl.ds(q0, QBLK, stride=dil), :] = lse

        def first_tiles(r, carry, tile=tile):
            tile(r, r, QBLK, True)
            return carry

        lax.fori_loop(0, dil, first_tiles, 0)

        if n_blk > 1:
            def later_tiles(i, carry, tile=tile, dil=dil, n_blk=n_blk):
                r = i % dil
                blk = i // dil + 1
                tile(r + blk * QBLK * dil, r + (blk - 1) * QBLK * dil, 2 * QBLK, False)
                return carry

            lax.fori_loop(0, dil * (n_blk - 1), later_tiles, 0)

    def combine(i, carry):
        rows = pl.ds(pl.multiple_of(i * rb, rb), rb)
        ls = [lse_scr[gi, rows, :] for gi in range(3)]
        mx = jnp.maximum(jnp.maximum(ls[0], ls[1]), ls[2])
        es = [jnp.exp(l - mx) for l in ls]
        inv = 1.0 / (es[0] + es[1] + es[2])
        for gi in range(3):
            gate = qkvg[gi][3][0, rows, :]
            y_refs[gi][0, rows, :] = (o_scr[gi, rows, :] * (es[gi] * inv) * _silu(gate)).astype(bf16)
        return carry

    lax.fori_loop(0, t // rb, combine, 0)


def _attn_prompt(z3, cos, sa, sb, qnw, knw):
    b, t, _ = z3.shape
    in_specs = []
    for gi in range(3):
        for part in range(4):
            off = ATT_BLK0 + part * 6 + 2 * gi
            in_specs.append(pl.BlockSpec((1, t, LANES), lambda bi, jp, off=off: (bi, 0, off + jp)))
    in_specs += [pl.BlockSpec((t, LANES), lambda bi, jp: (0, 0))] * 3
    in_specs += [pl.BlockSpec((1, LANES), lambda bi, jp: (0, 0))] * 2
    out_specs = [pl.BlockSpec((1, t, LANES), lambda bi, jp: (bi, 0, jp))] * 3
    out_shape = [jax.ShapeDtypeStruct((b, t, 4 * HEAD), bf16)] * 3
    for window, _ in ATT_GROUPS:
        keep = min(window, t)
        out_specs += [pl.BlockSpec((1, keep, LANES), lambda bi, jp: (bi, 0, jp))] * 2
        out_shape += [jax.ShapeDtypeStruct((b, keep, 4 * HEAD), f32)] * 2
    outs = pl.pallas_call(
        functools.partial(_attn_body, t=t),
        grid=(b, 2),
        in_specs=in_specs,
        out_specs=out_specs,
        out_shape=out_shape,
        scratch_shapes=[pltpu.VMEM((t, LANES), f32), pltpu.VMEM((t, LANES), f32),
                        pltpu.VMEM((3, t, LANES), f32), pltpu.VMEM((3, t, LANES), f32)],
        compiler_params=pltpu.CompilerParams(
            dimension_semantics=("arbitrary", "arbitrary"), vmem_limit_bytes=VMEM_LIMIT),
        name="attn_prompt",
    )(*([z3] * 12), cos, sa, sb, qnw, knw)
    return outs


def _pool_body(u_ref, g_ref, w_ref, sc_ref, y_ref, scr, *, tt):
    gi = pl.program_id(1)
    tb = pl.program_id(2)

    @pl.when(tb == 0)
    def _():
        scr[0:16, :] = jnp.zeros((16, LANES), f32)

    @pl.when(tb > 0)
    def _():
        scr[0:16, :] = scr[tt:tt + 16, :]

    u = u_ref[0]
    scr[16:16 + tt, :] = u
    acc = u
    sums = {}
    for k in range(1, max(POOL_WINDOWS)):
        acc = acc + scr[pl.ds(16 - k, tt), :]
        if k + 1 in POOL_WINDOWS:
            sums[k + 1] = acc
    sw = jnp.where(gi == 0, sums[2], jnp.where(gi == 1, sums[4], jnp.where(gi == 2, sums[8], sums[16])))
    wf = jnp.left_shift(2, gi).astype(f32)
    pos = (tb * tt + lax.broadcasted_iota(jnp.int32, (tt, 1), 0)).astype(f32)
    cnt = jnp.minimum(wf, pos + 1.0)
    d = sw / cnt - u
    dd = _dot(d.astype(bf16), w_ref[0].astype(bf16)) * sc_ref[...]
    y_ref[0] = (dd * _silu(g_ref[0])).astype(bf16)


def _pool_prompt(z3, pw, psc, *, tt=512):
    b, t, _ = z3.shape
    ng = len(POOL_WINDOWS)
    return pl.pallas_call(
        functools.partial(_pool_body, tt=tt),
        grid=(b, ng, t // tt),
        in_specs=[
            pl.BlockSpec((1, tt, LANES), lambda bi, gi, tb: (bi, tb, POOL_BLK0 + gi)),
            pl.BlockSpec((1, tt, LANES), lambda bi, gi, tb: (bi, tb, POOL_BLK0 + ng + gi)),
            pl.BlockSpec((1, LANES, LANES), lambda bi, gi, tb: (gi, 0, 0)),
            pl.BlockSpec((1, LANES), lambda bi, gi, tb: (0, gi)),
        ],
        out_specs=pl.BlockSpec((1, tt, LANES), lambda bi, gi, tb: (bi, tb, gi)),
        out_shape=jax.ShapeDtypeStruct((b, t, POOL_W), bf16),
        scratch_shapes=[pltpu.VMEM((tt + 16, LANES), f32)],
        compiler_params=pltpu.CompilerParams(
            dimension_semantics=("arbitrary", "arbitrary", "arbitrary"), vmem_limit_bytes=VMEM_LIMIT),
        name="pool_prompt",
    )(z3, z3, pw, psc)


def _bcast8(x):
    return jnp.broadcast_to(x, (8, x.shape[-1]))


def _sample_body(z_ref, sh_ref, wkv_ref, pool_ref, ck0, cv0, ck1, cv1, ck2, cv2,
                 mu_ref, rp_ref, wup_ref, aup_ref, qnw_ref, knw_ref, cos_ref, sa_ref, sb_ref,
                 pw_ref, psc_ref,
                 mix_ref, swkv_ref, spool_ref, sk0, sv0, sk1, sv1, sk2, sv2,
                 qk_scr, *, past_len):
    z = z_ref[0]
    za = z[:, 0:RWKV_COLS]
    zb = z[:, RWKV_COLS:RWKV_COLS + ATT_COLS]
    zc = z[:, RWKV_COLS + ATT_COLS:]
    er = lax.broadcasted_iota(jnp.int32, (HEAD, HEAD), 0)
    ec = lax.broadcasted_iota(jnp.int32, (HEAD, HEAD), 1)
    eye = er == ec

    zs = za + (sh_ref[0] - za) * mu_ref[...]
    w_ = RWKV_W
    r, k, v, g = (zs[:, i * w_:(i + 1) * w_] for i in range(4))
    w_dn = zs[:, 4 * w_:4 * w_ + LORA]
    a_dn = zs[:, 4 * w_ + LORA:]
    rp = rp_ref[...]
    w0, a0, k_k, k_a, r_k, ln_w, ln_b = (rp[i:i + 1] for i in range(7))
    lw = _dot(_bcast8(jnp.tanh(w_dn)).astype(bf16), wup_ref[...].astype(bf16))[0:1]
    la = _dot(_bcast8(a_dn).astype(bf16), aup_ref[...].astype(bf16))[0:1]
    decay = jnp.exp(-jnp.exp(-jax.nn.softplus(-(w0 + lw)) - 0.5))
    a = jax.nn.sigmoid(a0 + la)
    kk = k * k_k
    k2 = k * (1.0 + (a - 1.0) * k_a)
    for h in range(RWKV_HEADS):
        hs = slice(h * HEAD, (h + 1) * HEAD)
        kkh = kk[:, hs]
        kkh = kkh / jnp.maximum(jnp.sqrt(jnp.sum(kkh * kkh, axis=-1, keepdims=True)), 1e-12)
        kh, vh, rh, ah = k2[:, hs], v[:, hs], r[:, hs], a[:, hs]
        s = wkv_ref[0, h]
        sa_col = jnp.sum(s * (-kkh), axis=-1, keepdims=True)
        v_col = jnp.sum(jnp.where(eye, jnp.broadcast_to(vh, (HEAD, HEAD)), 0.0), axis=-1, keepdims=True)
        sn = s * decay[:, hs] + sa_col * (kkh * ah) + v_col * kh
        swkv_ref[0, h] = sn
        o_col = jnp.sum(sn * rh, axis=-1, keepdims=True)
        o = jnp.sum(jnp.where(eye, jnp.broadcast_to(o_col, (HEAD, HEAD)), 0.0), axis=0, keepdims=True)
        mean = jnp.mean(o, axis=-1, keepdims=True)
        dl = o - mean
        var = jnp.mean(dl * dl, axis=-1, keepdims=True)
        on = dl * lax.rsqrt(var + GN_EPS) * ln_w[:, hs] + ln_b[:, hs]
        on = on + jnp.sum(rh * kh * r_k[:, hs], axis=-1, keepdims=True) * vh
        mix_ref[0, :, hs] = on * _silu(g[:, hs])

    aw = ATT_W
    q, kx, vx, gx = (zb[:, i * aw:(i + 1) * aw] for i in range(4))
    m0 = _pair_consts(8)
    for hp in range(ATT_W // LANES):
        ls = slice(hp * LANES, (hp + 1) * LANES)
        for idx, (src, nw) in enumerate(((q, qnw_ref), (kx, knw_ref))):
            x = _bcast8(src[:, ls])
            sq = x * x
            s0 = jnp.sum(jnp.where(m0, sq, 0.0), axis=-1, keepdims=True)
            s1 = jnp.sum(jnp.where(m0, 0.0, sq), axis=-1, keepdims=True)
            ms = jnp.where(m0, s0, s1) * (1.0 / HEAD)
            xn = x * lax.rsqrt(ms + NORM_EPS) * nw[...]
            qk_scr[idx, :, ls] = _rope_pair(xn, cos_ref[...], sa_ref[...], sb_ref[...])
    qn = qk_scr[0, 0:1, :]
    kn = qk_scr[1, 0:1, :]
    scale = HEAD ** -0.5
    caches = ((ck0, cv0, sk0, sv0), (ck1, cv1, sk1, sv1), (ck2, cv2, sk2, sv2))
    o_all, lse_all = [], []
    for gi, (ck, cv, sk, sv) in enumerate(caches):
        gs = slice(gi * 4 * HEAD, (gi + 1) * 4 * HEAD)
        sk[0] = kn[:, gs]
        sv[0] = vx[:, gs]
        kc = ck[0]
        vc = cv[0]
        og, lg = [], []
        for j in range(4):
            hd = gi * 4 + j
            hs = slice(hd * HEAD, (hd + 1) * HEAD)
            js = slice(j * HEAD, (j + 1) * HEAD)
            qh = qn[:, hs]
            s = jnp.sum(kc[:, js] * qh, axis=-1, keepdims=True) * scale
            s_new = jnp.sum(kn[:, hs] * qh, axis=-1, keepdims=True) * scale
            m = jnp.maximum(jnp.max(s, axis=0, keepdims=True), s_new)
            p = jnp.exp(s - m)
            p_new = jnp.exp(s_new - m)
            l = jnp.sum(p, axis=0, keepdims=True) + p_new
            o = (jnp.sum(p * vc[:, js], axis=0, keepdims=True) + p_new * vx[:, hs]) / l
            og.append(o)
            lg.append(m + jnp.log(l))
        o_all.append(og)
        lse_all.append(lg)
    for j in range(4):
        mx = jnp.maximum(jnp.maximum(lse_all[0][j], lse_all[1][j]), lse_all[2][j])
        es = [jnp.exp(lse_all[gi][j] - mx) for gi in range(3)]
        inv = 1.0 / (es[0] + es[1] + es[2])
        for gi in range(3):
            hd = gi * 4 + j
            hs = slice(hd * HEAD, (hd + 1) * HEAD)
            mix_ref[0, :, RWKV_W + hd * HEAD:RWKV_W + (hd + 1) * HEAD] = (
                o_all[gi][j] * (es[gi] * inv) * _silu(gx[:, hs]))

    u = zc[:, 0:POOL_W]
    gate = zc[:, POOL_W:]
    prev = pool_ref[0]
    for gi, w in enumerate(POOL_WINDOWS):
        cs = slice(gi * LANES, (gi + 1) * LANES)
        ug = u[:, cs]
        sw = jnp.sum(prev[POOL_BUF - (w - 1):POOL_BUF, cs], axis=0, keepdims=True) + ug
        cnt = min(float(w), float(past_len) + 1.0)
        d = sw / cnt - ug
        dd = _dot(_bcast8(d).astype(bf16), pw_ref[gi].astype(bf16))[0:1] * psc_ref[:, cs]
        mix_ref[0, :, RWKV_W + ATT_W + gi * LANES:RWKV_W + ATT_W + (gi + 1) * LANES] = dd * _silu(gate[:, cs])
    spool_ref[0, 0:POOL_BUF - 1, :] = prev[1:POOL_BUF]
    spool_ref[0, POOL_BUF - 1:POOL_BUF, :] = u


def _sample_step(zs, sh, wkv, pool, caches, mu, rp, wup, aup, qnw, knw, cos, sa, sb, pw, psc, *, past_len):
    nb = zs.shape[0]
    d_in = zs.shape[-1]
    d_mix = RWKV_W + ATT_W + POOL_W
    span = QBLK
    kvw = 4 * HEAD

    def full(shape):
        nd = len(shape)
        return pl.BlockSpec(shape, lambda b, nd=nd: (0,) * nd)

    in_specs = [
        pl.BlockSpec((1, 1, d_in), lambda b: (b, 0, 0)),
        pl.BlockSpec((1, 1, RWKV_COLS), lambda b: (b, 0, 0)),
        pl.BlockSpec((1, RWKV_HEADS, HEAD, HEAD), lambda b: (b, 0, 0, 0)),
        pl.BlockSpec((1, POOL_BUF, POOL_W), lambda b: (b, 0, 0)),
    ]
    in_specs += [pl.BlockSpec((1, span, kvw), lambda b: (b, 0, 0))] * 6
    in_specs += [full(mu.shape), full(rp.shape), full(wup.shape), full(aup.shape), full(qnw.shape),
                 full(knw.shape), full(cos.shape), full(sa.shape), full(sb.shape), full(pw.shape),
                 full(psc.shape)]
    out_specs = [
        pl.BlockSpec((1, 1, d_mix), lambda b: (b, 0, 0)),
        pl.BlockSpec((1, RWKV_HEADS, HEAD, HEAD), lambda b: (b, 0, 0, 0)),
        pl.BlockSpec((1, POOL_BUF, POOL_W), lambda b: (b, 0, 0)),
    ] + [pl.BlockSpec((1, 1, kvw), lambda b: (b, 0, 0))] * 6
    out_shape = [
        jax.ShapeDtypeStruct((nb, 1, d_mix), f32),
        jax.ShapeDtypeStruct((nb, RWKV_HEADS, HEAD, HEAD), f32),
        jax.ShapeDtypeStruct((nb, POOL_BUF, POOL_W), f32),
    ] + [jax.ShapeDtypeStruct((nb, 1, kvw), f32)] * 6
    return pl.pallas_call(
        functools.partial(_sample_body, past_len=past_len),
        grid=(nb,),
        in_specs=in_specs,
        out_specs=out_specs,
        out_shape=out_shape,
        scratch_shapes=[pltpu.VMEM((2, 8, ATT_W), f32)],
        compiler_params=pltpu.CompilerParams(
            dimension_semantics=("arbitrary",), vmem_limit_bytes=VMEM_LIMIT),
        name="sample_step",
    )(zs, sh, wkv, pool, *caches, mu, rp, wup, aup, qnw, knw, cos, sa, sb, pw, psc)


def _rope_tables(pos):
    half = ROPE_DIMS // 2
    inv = jnp.power(jnp.float32(ROPE_THETA), -jnp.arange(half, dtype=f32) * 2.0 / ROPE_DIMS)
    ang = pos[:, None] * inv[None, :]
    cos, sin = jnp.cos(ang), jnp.sin(ang)
    n = pos.shape[0]
    pad = jnp.zeros((n, HEAD - ROPE_DIMS), f32)
    zero = jnp.zeros((n, half), f32)
    c_head = jnp.concatenate([cos, cos, pad + 1.0], axis=1)
    a_head = jnp.concatenate([-sin, zero, pad], axis=1)
    b_head = jnp.concatenate([zero, sin, pad], axis=1)
    return tuple(jnp.concatenate([x, x], axis=1) for x in (c_head, a_head, b_head))


def kernel(x_prompt, x_sample, state_wkv, state_shift, state_pool,
           cache_k_w128, cache_v_w128, cache_k_w512, cache_v_w512, cache_k_w2048, cache_v_w2048,
           norm_w, w_in, w_out, rwkv_mu, rwkv_w0, rwkv_w_up, rwkv_a0, rwkv_a_up,
           rwkv_k_k, rwkv_k_a, rwkv_r_k, rwkv_ln_w, rwkv_ln_b, q_norm_w, k_norm_w, pool_w, pool_scale):
    b, t, d = x_prompt.shape
    nb, ts, _ = x_sample.shape
    depth = w_in.shape[0]
    d_in = w_in.shape[2]
    assert ts == 1 and t % 512 == 0
    past_len = 16384
    caches_in = ((cache_k_w128, cache_v_w128), (cache_k_w512, cache_v_w512), (cache_k_w2048, cache_v_w2048))
    for (window, _), (ck, _) in zip(ATT_GROUPS, caches_in):
        assert ck.shape[2] == window, "window buffers are expected to be full"

    cos_p, sa_p, sb_p = _rope_tables(jnp.arange(t, dtype=f32))
    cos_s, sa_s, sb_s = _rope_tables(past_len + jnp.arange(1, dtype=f32))

    hp = x_prompt.reshape(b * t, d)
    hs = x_sample.reshape(nb, d)
    npair = RWKV_HEADS // 2
    p_out = [[] for _ in range(9)]
    s_out = [[] for _ in range(9)]
    for l in range(depth):
        zp, zs = _inproj(hp, hs, norm_w[l].reshape(1, d), w_in[l])
        z3 = zp.reshape(b, t, d_in)

        mu = rwkv_mu[l]
        mu4 = mu[:4 * RWKV_W].reshape(4, npair, LANES).transpose(1, 0, 2)
        mul = jnp.broadcast_to(mu[4 * RWKV_W:].reshape(1, 1, LANES), (npair, 1, LANES))
        vecs = jnp.stack([rwkv_w0[l], rwkv_a0[l], rwkv_k_k[l], rwkv_k_a[l], rwkv_r_k[l].reshape(-1),
                          rwkv_ln_w[l], rwkv_ln_b[l]])
        pp = jnp.concatenate([mu4, mul, vecs.reshape(7, npair, LANES).transpose(1, 0, 2)], axis=1)
        wup = rwkv_w_up[l].reshape(LORA, npair, LANES).transpose(1, 0, 2)
        aup = rwkv_a_up[l].reshape(LORA, npair, LANES).transpose(1, 0, 2)
        zl = jnp.zeros_like(wup)
        wl = jnp.concatenate([jnp.concatenate([wup, zl], axis=2), jnp.concatenate([zl, aup], axis=2)], axis=1)

        ya, p_wkv = _rwkv_prompt(z3, pp, wl)
        qnw = jnp.tile(q_norm_w[l], 2).reshape(1, LANES)
        knw = jnp.tile(k_norm_w[l], 2).reshape(1, LANES)
        att = _attn_prompt(z3, cos_p, sa_p, sb_p, qnw, knw)
        psc = pool_scale[l].reshape(1, POOL_W)
        yc = _pool_prompt(z3, pool_w[l], psc)

        sample = _sample_step(
            zs.reshape(nb, 1, d_in), state_shift[l].reshape(nb, 1, RWKV_COLS), state_wkv[l], state_pool[l],
            [c[l].reshape(nb, QBLK, -1) for pair in caches_in for c in pair],
            mu.reshape(1, RWKV_COLS), vecs, rwkv_w_up[l], rwkv_a_up[l], qnw, knw, cos_s, sa_s, sb_s,
            pool_w[l], psc, past_len=past_len)
        mix_s = sample[0].reshape(nb, -1)

        ys = [ya.reshape(b * t, RWKV_W)] + [y.reshape(b * t, 4 * HEAD) for y in att[:3]] + [yc.reshape(b * t, POOL_W)]
        hp, hs = _outproj(hp, ys, hs, mix_s, w_out[l])

        p_out[0].append(p_wkv)
        p_out[1].append(z3[:, -1, :RWKV_COLS])
        p_out[2].append(z3[:, t - POOL_BUF:, RWKV_COLS + ATT_COLS:RWKV_COLS + ATT_COLS + POOL_W])
        for gi in range(3):
            keep = att[3 + 2 * gi].shape[1]
            p_out[3 + 2 * gi].append(att[3 + 2 * gi].reshape(b, keep, 4, HEAD))
            p_out[4 + 2 * gi].append(att[4 + 2 * gi].reshape(b, keep, 4, HEAD))
        s_out[0].append(sample[1])
        s_out[1].append(zs[:, :RWKV_COLS])
        s_out[2].append(sample[2])
        for i in range(6):
            s_out[3 + i].append(sample[3 + i].reshape(nb, 1, 4, HEAD))

    return (hp.reshape(b, t, d), hs.reshape(nb, 1, d),
            *[jnp.stack(x) for x in p_out], *[jnp.stack(x) for x in s_out])
```

```python
import functools

import jax
import jax.numpy as jnp
from jax import lax
from jax.experimental import pallas as pl
from jax.experimental.pallas import tpu as pltpu

f32 = jnp.float32
bf16 = jnp.bfloat16

HEAD = 64
LANES = 128
NORM_EPS = 1e-6
GN_EPS = 64e-5
RWKV_HEADS = 12
RWKV_W = RWKV_HEADS * HEAD
LORA = 64
RWKV_COLS = 4 * RWKV_W + 2 * LORA
ATT_GROUPS = ((128, 1), (512, 4), (2048, 16))
ATT_W = 12 * HEAD
ATT_COLS = 4 * ATT_W
ROPE_THETA = 500000.0
ROPE_DIMS = 16
QBLK = 128
POOL_WINDOWS = (2, 4, 8, 16)
POOL_W = 512
POOL_BUF = 15
CHUNK = 64
VMEM_LIMIT = 56 * 1024 * 1024

ATT_BLK0 = RWKV_COLS // LANES
POOL_BLK0 = (RWKV_COLS + ATT_COLS) // LANES


def _dot(a, b):
    return jnp.dot(a, b, preferred_element_type=f32)


def _dot_nt(a, b):
    return lax.dot_general(a, b, (((1,), (1,)), ((), ())), preferred_element_type=f32)


def _dot_tn(a, b):
    return lax.dot_general(a, b, (((0,), (0,)), ((), ())), preferred_element_type=f32)


def _split2(x):
    hi = x.astype(bf16)
    lo = (x - hi.astype(f32)).astype(bf16)
    return hi, lo


def _split3(x):
    hi = x.astype(bf16)
    r1 = x - hi.astype(f32)
    mid = r1.astype(bf16)
    lo = (r1 - mid.astype(f32)).astype(bf16)
    return hi, mid, lo


def _silu(x):
    return x * jax.nn.sigmoid(x)


def _rms(x, w):
    ms = jnp.mean(x * x, axis=-1, keepdims=True)
    return x * lax.rsqrt(ms + NORM_EPS) * w


def _pair_consts(n):
    lane = lax.broadcasted_iota(jnp.int32, (n, LANES), 1)
    return lane < HEAD


def _ones_blockdiag():
    r = lax.broadcasted_iota(jnp.int32, (LANES, LANES), 0)
    c = lax.broadcasted_iota(jnp.int32, (LANES, LANES), 1)
    return jnp.where((r < HEAD) == (c < HEAD), 1.0, 0.0).astype(bf16)


def _segsum(x, ones_bd):
    hi, lo = _split2(x)
    return _dot(hi, ones_bd) + _dot(lo, ones_bd)


def _inproj_body(xp_ref, xs_ref, nw_ref, w_ref, zp_ref, zs_ref, h_scr, *, tn, n):
    i = pl.program_id(0)
    j = pl.program_id(1)

    @pl.when(j == 0)
    def _():
        h_scr[...] = _rms(xp_ref[...], nw_ref[...]).astype(bf16)

    w = w_ref[...].astype(bf16)
    zp_ref[...] = _dot(h_scr[...], w)

    @pl.when(i == 0)
    def _():
        hs = _rms(xs_ref[...], nw_ref[...]).astype(bf16)
        col = pl.multiple_of(j * tn, LANES)
        zs = _dot(hs, w)
        cols = col + lax.broadcasted_iota(jnp.int32, zs.shape, 1)
        zs_ref[:, pl.ds(col, tn)] = jnp.where(cols < n, zs, 0.0)


def _inproj(xp, xs, nw, w_all, layer, *, tm=1024, tn=512):
    m, d = xp.shape
    n = w_all.shape[2]
    nb = xs.shape[0]
    nj = pl.cdiv(n, tn)
    return pl.pallas_call(
        functools.partial(_inproj_body, tn=tn, n=n),
        grid=(m // tm, nj),
        in_specs=[
            pl.BlockSpec((tm, d), lambda i, j: (i, 0)),
            pl.BlockSpec((nb, d), lambda i, j: (0, 0)),
            pl.BlockSpec((1, d), lambda i, j: (0, 0)),
            pl.BlockSpec((None, d, tn), lambda i, j: (layer, 0, j)),
        ],
        out_specs=[
            pl.BlockSpec((tm, tn), lambda i, j: (i, j)),
            pl.BlockSpec((nb, nj * tn), lambda i, j: (0, 0)),
        ],
        out_shape=[jax.ShapeDtypeStruct((m, n), f32), jax.ShapeDtypeStruct((nb, nj * tn), f32)],
        scratch_shapes=[pltpu.VMEM((tm, d), bf16)],
        compiler_params=pltpu.CompilerParams(
            dimension_semantics=("arbitrary", "arbitrary"), vmem_limit_bytes=VMEM_LIMIT),
        name="inproj",
    )(xp, xs, nw, w_all)


def _outproj_body(*refs, tn, widths):
    n = len(widths)
    xp_ref = refs[0]
    y_refs = refs[1:1 + n]
    xs_ref, ms_ref, w_ref, op_ref, os_ref = refs[1 + n:]
    i = pl.program_id(0)
    j = pl.program_id(1)
    w = w_ref[...].astype(bf16)
    acc = None
    row = 0
    for y_ref, width in zip(y_refs, widths):
        part = _dot(y_ref[...], w[row:row + width])
        acc = part if acc is None else acc + part
        row += width
    op_ref[...] = xp_ref[...] + acc

    @pl.when(i == 0)
    def _():
        col = pl.multiple_of(j * tn, LANES)
        os_ref[:, pl.ds(col, tn)] = xs_ref[:, pl.ds(col, tn)] + _dot(ms_ref[...].astype(bf16), w)


def _outproj(xp, ys, xs, ms, w_all, layer, *, tm=1024, tn=512):
    m, d = xp.shape
    nb = xs.shape[0]
    dm = w_all.shape[1]
    widths = tuple(y.shape[1] for y in ys)
    assert sum(widths) == dm
    return pl.pallas_call(
        functools.partial(_outproj_body, tn=tn, widths=widths),
        grid=(m // tm, d // tn),
        in_specs=[pl.BlockSpec((tm, tn), lambda i, j: (i, j))]
        + [pl.BlockSpec((tm, width), lambda i, j: (i, 0)) for width in widths]
        + [
            pl.BlockSpec((nb, d), lambda i, j: (0, 0)),
            pl.BlockSpec((nb, dm), lambda i, j: (0, 0)),
            pl.BlockSpec((None, dm, tn), lambda i, j: (layer, 0, j)),
        ],
        out_specs=[
            pl.BlockSpec((tm, tn), lambda i, j: (i, j)),
            pl.BlockSpec((nb, d), lambda i, j: (0, 0)),
        ],
        out_shape=[jax.ShapeDtypeStruct((m, d), f32), jax.ShapeDtypeStruct((nb, d), f32)],
        compiler_params=pltpu.CompilerParams(
            dimension_semantics=("arbitrary", "arbitrary"), vmem_limit_bytes=VMEM_LIMIT),
        name="outproj",
    )(xp, *ys, xs, ms, w_all)


_P_MU, _P_W0, _P_A0, _P_KK, _P_KA, _P_RK, _P_LNW, _P_LNB = 0, 5, 6, 7, 8, 9, 10, 11


def _rwkv_chunks(at, bt, kt, rt, v, wcs, st, consts, group=8):
    m0, gmask, bdmask, eye = consts
    m0w = jnp.concatenate([m0, m0], axis=1)
    zb64 = jnp.zeros((CHUNK, LANES), bf16)
    n = len(wcs)

    def rows(x, c):
        return x[c * CHUNK:(c + 1) * CHUNK]

    def split_heads(xb, m):
        zero = jnp.zeros_like(xb)
        return jnp.concatenate([jnp.where(m, xb, zero), jnp.where(m, zero, xb)], axis=0)

    g, p, x, mc, nc, rp, ov = {}, {}, {}, {}, {}, {}, {}

    def stage_scores(chs):
        for c in chs:
            lhs = jnp.concatenate([rows(at, c), rows(rt, c)], axis=0).astype(bf16)
            rhs = jnp.concatenate([split_heads(rows(bt, c).astype(bf16), m0),
                                   split_heads(rows(kt, c).astype(bf16), m0)], axis=0)
            g[c] = jnp.where(gmask, _dot_nt(lhs, rhs), 0.0)
            p[c] = g[c][0:CHUNK, 0:LANES]

    def stage_av(chs):
        for c in chs:
            vs = split_heads(rows(v, c).astype(bf16), m0)
            av = _dot(g[c][0:CHUNK, LANES:].astype(bf16), vs)
            x[c] = jnp.concatenate([av, rows(at, c)], axis=1)

    def stage_level(chs, lev):
        for c in chs:
            xx = split_heads(x[c].astype(bf16), m0w)
            pb = p[c].astype(bf16)
            if lev < 5:
                out = _dot(pb, jnp.concatenate([split_heads(pb, m0), xx], axis=1))
                p[c] = out[:, 0:LANES]
                x[c] = x[c] + out[:, LANES:]
            else:
                x[c] = x[c] + _dot(pb, xx)

    def stage_fold(chs):
        for c in chs:
            xb = x[c].astype(bf16)
            uv = xb[:, 0:LANES]
            ap = xb[:, LANES:]
            vb = rows(v, c).astype(bf16)
            tnl = jnp.concatenate([rows(bt, c) * wcs[c], rows(kt, c) * wcs[c]], axis=0).astype(bf16)
            tnr = jnp.concatenate(
                [jnp.concatenate([ap, uv], axis=1), jnp.concatenate([zb64, vb], axis=1)], axis=0)
            mn = jnp.where(bdmask, _dot_tn(tnl, tnr), 0.0)
            mc[c] = mn[:, 0:LANES] + jnp.where(eye, jnp.broadcast_to(wcs[c], (LANES, LANES)), 0.0)
            nc[c] = mn[:, LANES:]
            l2 = g[c][CHUNK:].astype(bf16)
            r2 = jnp.concatenate([
                split_heads(jnp.concatenate([ap, uv], axis=1), m0w),
                split_heads(jnp.concatenate([zb64, vb], axis=1), m0w)], axis=0)
            ro = _dot(l2, r2)
            rp[c] = rows(rt, c) + ro[:, 0:LANES]
            ov[c] = ro[:, LANES:]

    outs = {}
    state = [st]

    def carry(c):
        seq = _dot(jnp.concatenate([rp[c], mc[c]], axis=0).astype(bf16), state[0].astype(bf16))
        outs[c] = seq[0:CHUNK] + ov[c]
        state[0] = seq[CHUNK:] + nc[c]

    stages = ([stage_scores, stage_av] + [functools.partial(stage_level, lev=lev) for lev in range(6)]
              + [stage_fold])
    groups = [list(range(i, min(i + group, n))) for i in range(0, n, group)]
    pending = []
    for grp in groups:
        for stage in stages:
            stage(grp)
            if pending:
                carry(pending.pop(0))
        while pending:
            carry(pending.pop(0))
        pending = list(grp)
    while pending:
        carry(pending.pop(0))
    return jnp.concatenate([outs[c] for c in range(n)], axis=0), state[0]


def _rwkv_body(zr_ref, zk_ref, zv_ref, zg_ref, zl_ref, pp_ref, wl_ref, y_ref, s_ref,
               sh_scr, st_scr, *, tt):
    tb = pl.program_id(2)
    nt = pl.num_programs(2)
    zrefs = (zr_ref, zk_ref, zv_ref, zg_ref, zl_ref)

    @pl.when(tb == 0)
    def _():
        st_scr[...] = jnp.zeros_like(st_scr)
        for i in range(5):
            sh_scr[i, 7:8, :] = jnp.zeros((1, LANES), f32)

    @pl.when(tb > 0)
    def _():
        for i in range(5):
            sh_scr[i, 7:8, :] = sh_scr[i, 7 + tt:8 + tt, :]

    pp = pp_ref[...]
    zs = []
    for i in range(5):
        z = zrefs[i][0]
        sh_scr[i, 8:8 + tt, :] = z
        zprev = sh_scr[i, pl.ds(7, tt), :]
        zs.append(z + (zprev - z) * pp[_P_MU + i:_P_MU + i + 1])
    r, k, v, g, lo = zs

    m0t = _pair_consts(tt)
    ones_bd = _ones_blockdiag()
    xl = jnp.where(m0t, jnp.tanh(lo), lo)
    la = _dot(xl.astype(bf16), wl_ref[...].astype(bf16))
    w_log = -jax.nn.softplus(-(pp[_P_W0:_P_W0 + 1] + la[:, 0:LANES])) - 0.5
    ld = -jnp.exp(w_log)
    a = jax.nn.sigmoid(pp[_P_A0:_P_A0 + 1] + la[:, LANES:])
    kk = k * pp[_P_KK:_P_KK + 1]
    kk = kk / jnp.maximum(jnp.sqrt(_segsum(kk * kk, ones_bd)), 1e-12)
    k2 = k * (1.0 + (a - 1.0) * pp[_P_KA:_P_KA + 1])
    beta = kk * a
    bonus = _segsum(r * k2 * pp[_P_RK:_P_RK + 1], ones_bd) * v

    m0 = _pair_consts(CHUNK)
    gr = lax.broadcasted_iota(jnp.int32, (2 * CHUNK, 2 * LANES), 0)
    gc = lax.broadcasted_iota(jnp.int32, (2 * CHUNK, 2 * LANES), 1)
    gt = gr % CHUNK
    gs = gc % CHUNK
    gmask = (gt > gs) | ((gr >= CHUNK) & (gt == gs))
    br = lax.broadcasted_iota(jnp.int32, (LANES, 2 * LANES), 0)
    bc = lax.broadcasted_iota(jnp.int32, (LANES, 2 * LANES), 1)
    bdmask = (br < HEAD) == ((bc % LANES) < HEAD)
    er = lax.broadcasted_iota(jnp.int32, (LANES, LANES), 0)
    ec = lax.broadcasted_iota(jnp.int32, (LANES, LANES), 1)
    eye = er == ec
    consts = (m0, gmask, bdmask, eye)
    tr = lax.broadcasted_iota(jnp.int32, (CHUNK, CHUNK), 0)
    tc = lax.broadcasted_iota(jnp.int32, (CHUNK, CHUNK), 1)
    tril = jnp.where(tc <= tr, 1.0, 0.0).astype(bf16)

    h3 = _split3(ld)
    cl = jnp.concatenate([
        sum(_dot(tril, h[c * CHUNK:(c + 1) * CHUNK]) for h in h3) for c in range(tt // CHUNK)],
        axis=0)
    e_in = jnp.exp(cl)
    e_neg = jnp.exp(-cl)
    wcs = [e_in[(c + 1) * CHUNK - 1:(c + 1) * CHUNK] for c in range(tt // CHUNK)]
    o, st = _rwkv_chunks(-kk * jnp.exp(cl - ld), beta * e_neg, k2 * e_neg, r * e_in, v, wcs,
                         st_scr[...], consts)
    st_scr[...] = st

    mean = _segsum(o, ones_bd) * (1.0 / HEAD)
    dl = o - mean
    var = _segsum(dl * dl, ones_bd) * (1.0 / HEAD)
    on = dl * lax.rsqrt(var + GN_EPS) * pp[_P_LNW:_P_LNW + 1] + pp[_P_LNB:_P_LNB + 1] + bonus
    y_ref[0] = (on * _silu(g)).astype(bf16)

    @pl.when(tb == nt - 1)
    def _():
        stt = st.T
        s_ref[0, 0] = stt[0:HEAD, 0:HEAD]
        s_ref[0, 1] = stt[HEAD:, HEAD:]


def _rwkv_prompt(z3, pp, wl, *, tt=512):
    b, t, _ = z3.shape
    npair = RWKV_HEADS // 2

    def zspec(off):
        return pl.BlockSpec((1, tt, LANES), lambda bi, p, tb, off=off: (bi, tb, off + p))

    lora_blk = 4 * RWKV_W // LANES
    return pl.pallas_call(
        functools.partial(_rwkv_body, tt=tt),
        grid=(b, npair, t // tt),
        in_specs=[
            zspec(0), zspec(npair), zspec(2 * npair), zspec(3 * npair),
            pl.BlockSpec((1, tt, LANES), lambda bi, p, tb: (bi, tb, lora_blk)),
            pl.BlockSpec((None, 12, LANES), lambda bi, p, tb: (p, 0, 0)),
            pl.BlockSpec((None, LANES, 2 * LANES), lambda bi, p, tb: (p, 0, 0)),
        ],
        out_specs=[
            pl.BlockSpec((1, tt, LANES), lambda bi, p, tb: (bi, tb, p)),
            pl.BlockSpec((1, 2, HEAD, HEAD), lambda bi, p, tb: (bi, p, 0, 0)),
        ],
        out_shape=[jax.ShapeDtypeStruct((b, t, RWKV_W), bf16),
                   jax.ShapeDtypeStruct((b, RWKV_HEADS, HEAD, HEAD), f32)],
        scratch_shapes=[pltpu.VMEM((5, tt + 8, LANES), f32), pltpu.VMEM((LANES, LANES), f32)],
        compiler_params=pltpu.CompilerParams(
            dimension_semantics=("arbitrary", "arbitrary", "arbitrary"), vmem_limit_bytes=VMEM_LIMIT),
        name="rwkv_prompt",
    )(z3, z3, z3, z3, z3, pp, wl)


def _rope_pair(x, cos, sa, sb):
    return x * cos + pltpu.roll(x, LANES - ROPE_DIMS // 2, 1) * sa + pltpu.roll(x, ROPE_DIMS // 2, 1) * sb


def _attn_body(*refs, t):
    qkvg = [refs[4 * gi:4 * gi + 4] for gi in range(3)]
    cos_ref, sin_ref, qnw_ref, knw_ref = refs[12:16]
    y_refs = refs[16:19]
    kv_refs = [refs[19 + 2 * gi:21 + 2 * gi] for gi in range(3)]
    qn_scr, kn_scr, o_scr, lse_scr = refs[25:29]

    rb = 256
    ones_bd = _ones_blockdiag()
    m0q = _pair_consts(QBLK)
    scale = HEAD ** -0.5
    half = ROPE_DIMS // 2
    pj = lax.broadcasted_iota(jnp.int32, (LANES, LANES), 0)
    pi = lax.broadcasted_iota(jnp.int32, (LANES, LANES), 1)
    pin = pi % HEAD
    perm = jnp.where(((pin < half) & (pj == pi + half)) | ((pin >= half) & (pin < ROPE_DIMS) & (pj == pi - half)),
                     1.0, 0.0).astype(bf16)

    for gi, (window, dil) in enumerate(ATT_GROUPS):
        q_ref, k_ref, v_ref, g_ref = qkvg[gi]
        pk_ref, pv_ref = kv_refs[gi]
        keep = min(window, t)

        def norm_rows(i, carry, q_ref=q_ref, k_ref=k_ref):
            jobs = []
            for u in range(2):
                rows = pl.ds(pl.multiple_of((2 * i + u) * rb, rb), rb)
                for src, nw, dst in ((q_ref, qnw_ref, qn_scr), (k_ref, knw_ref, kn_scr)):
                    x = src[0, rows, :]
                    jobs.append((x, _segsum(x * x, ones_bd), nw, dst, rows))
            for x, ss, nw, dst, rows in jobs:
                xn = x * lax.rsqrt(ss * (1.0 / HEAD) + NORM_EPS) * nw[...]
                hi, lo = _split2(xn)
                swapped = _dot(hi, perm) + _dot(lo, perm)
                dst[rows, :] = xn * cos_ref[rows, :] + swapped * sin_ref[rows, :]
            return carry

        lax.fori_loop(0, t // (2 * rb), norm_rows, 0)
        pk_ref[0] = kn_scr[t - keep:t, :]
        pv_ref[0] = v_ref[0, t - keep:t, :]

        n_sub = t // dil
        n_blk = n_sub // QBLK

        def tiles(starts, nk, first, v_ref=v_ref, dil=dil, gi=gi):
            qi = lax.broadcasted_iota(jnp.int32, (QBLK, nk), 0)
            kj = lax.broadcasted_iota(jnp.int32, (QBLK, nk), 1)
            mask = (kj <= qi) if first else ((kj >= qi) & (kj <= qi + QBLK))
            scores, vts = [], []
            for q0, k0 in starts:
                qt = qn_scr[pl.ds(q0, QBLK, stride=dil), :] * scale
                kt = kn_scr[pl.ds(k0, nk, stride=dil), :].astype(bf16)
                vts.append(v_ref[0, pl.ds(k0, nk, stride=dil), :].astype(bf16))
                for hh in range(2):
                    qh = jnp.where(m0q, qt, 0.0) if hh == 0 else jnp.where(m0q, 0.0, qt)
                    scores.append(_dot_nt(qh.astype(bf16), kt))
            probs, sums, lses = [], [], []
            for s in scores:
                s = jnp.where(mask, s, -jnp.inf)
                m = jnp.max(s, axis=-1, keepdims=True)
                p = jnp.exp(s - m)
                l = jnp.sum(p, axis=-1, keepdims=True)
                probs.append(p.astype(bf16))
                sums.append(l)
                lses.append(m + jnp.log(l))
            for ti, (q0, _) in enumerate(starts):
                oh = [_dot(probs[2 * ti + hh], vts[ti]) / sums[2 * ti + hh] for hh in range(2)]
                o_scr[gi, pl.ds(q0, QBLK, stride=dil), :] = jnp.where(m0q, oh[0], oh[1])
                lse_scr[gi, pl.ds(q0, QBLK, stride=dil), :] = jnp.where(
                    m0q, jnp.broadcast_to(lses[2 * ti], (QBLK, LANES)),
                    jnp.broadcast_to(lses[2 * ti + 1], (QBLK, LANES)))

        per_first = min(4, dil)

        def first_tiles(i, carry, tiles=tiles, per=per_first):
            tiles([(i * per + u, i * per + u) for u in range(per)], QBLK, True)
            return carry

        lax.fori_loop(0, dil // per_first, first_tiles, 0)

        if n_blk > 1:
            per_later = 3
            assert (dil * (n_blk - 1)) % per_later == 0

            def later_tiles(i, carry, tiles=tiles, dil=dil, per=per_later):
                starts = []
                for u in range(per):
                    idx = i * per + u
                    r = idx % dil
                    blk = idx // dil + 1
                    starts.append((r + blk * QBLK * dil, r + (blk - 1) * QBLK * dil))
                tiles(starts, 2 * QBLK, False)
                return carry

            lax.fori_loop(0, dil * (n_blk - 1) // per_later, later_tiles, 0)

    def combine(i, carry):
        rows = pl.ds(pl.multiple_of(i * rb, rb), rb)
        ls = [lse_scr[gi, rows, :] for gi in range(3)]
        mx = jnp.maximum(jnp.maximum(ls[0], ls[1]), ls[2])
        es = [jnp.exp(l - mx) for l in ls]
        inv = 1.0 / (es[0] + es[1] + es[2])
        for gi in range(3):
            gate = qkvg[gi][3][0, rows, :]
            y_refs[gi][0, rows, :] = (o_scr[gi, rows, :] * (es[gi] * inv) * _silu(gate)).astype(bf16)
        return carry

    lax.fori_loop(0, t // rb, combine, 0)


def _attn_prompt(z3, cos, sin, qnw, knw):
    b, t, _ = z3.shape
    in_specs = []
    for gi in range(3):
        for part in range(4):
            off = ATT_BLK0 + part * 6 + 2 * gi
            in_specs.append(pl.BlockSpec((1, t, LANES), lambda bi, jp, off=off: (bi, 0, off + jp)))
    in_specs += [pl.BlockSpec((t, LANES), lambda bi, jp: (0, 0))] * 2
    in_specs += [pl.BlockSpec((1, LANES), lambda bi, jp: (0, 0))] * 2
    out_specs = [pl.BlockSpec((1, t, LANES), lambda bi, jp: (bi, 0, jp))] * 3
    out_shape = [jax.ShapeDtypeStruct((b, t, 4 * HEAD), bf16)] * 3
    for window, _ in ATT_GROUPS:
        keep = min(window, t)
        out_specs += [pl.BlockSpec((1, keep, LANES), lambda bi, jp: (bi, 0, jp))] * 2
        out_shape += [jax.ShapeDtypeStruct((b, keep, 4 * HEAD), f32)] * 2
    outs = pl.pallas_call(
        functools.partial(_attn_body, t=t),
        grid=(b, 2),
        in_specs=in_specs,
        out_specs=out_specs,
        out_shape=out_shape,
        scratch_shapes=[pltpu.VMEM((t, LANES), f32), pltpu.VMEM((t, LANES), f32),
                        pltpu.VMEM((3, t, LANES), f32), pltpu.VMEM((3, t, LANES), f32)],
        compiler_params=pltpu.CompilerParams(
            dimension_semantics=("arbitrary", "arbitrary"), vmem_limit_bytes=VMEM_LIMIT),
        name="attn_prompt",
    )(*([z3] * 12), cos, sin, qnw, knw)
    return outs


def _pool_body(u_ref, g_ref, w_ref, sc_ref, y_ref, scr, *, tt):
    gi = pl.program_id(1)
    tb = pl.program_id(2)

    @pl.when(tb == 0)
    def _():
        scr[0:16, :] = jnp.zeros((16, LANES), f32)

    @pl.when(tb > 0)
    def _():
        scr[0:16, :] = scr[tt:tt + 16, :]

    u = u_ref[0]
    scr[16:16 + tt, :] = u
    acc = u
    sums = {}
    for k in range(1, max(POOL_WINDOWS)):
        acc = acc + scr[pl.ds(16 - k, tt), :]
        if k + 1 in POOL_WINDOWS:
            sums[k + 1] = acc
    sw = jnp.where(gi == 0, sums[2], jnp.where(gi == 1, sums[4], jnp.where(gi == 2, sums[8], sums[16])))
    wf = jnp.left_shift(2, gi).astype(f32)
    pos = (tb * tt + lax.broadcasted_iota(jnp.int32, (tt, 1), 0)).astype(f32)
    cnt = jnp.minimum(wf, pos + 1.0)
    d = sw / cnt - u
    dd = _dot(d.astype(bf16), w_ref[0].astype(bf16)) * sc_ref[...]
    y_ref[0] = (dd * _silu(g_ref[0])).astype(bf16)


def _pool_prompt(z3, pw, psc, *, tt=512):
    b, t, _ = z3.shape
    ng = len(POOL_WINDOWS)
    return pl.pallas_call(
        functools.partial(_pool_body, tt=tt),
        grid=(b, ng, t // tt),
        in_specs=[
            pl.BlockSpec((1, tt, LANES), lambda bi, gi, tb: (bi, tb, POOL_BLK0 + gi)),
            pl.BlockSpec((1, tt, LANES), lambda bi, gi, tb: (bi, tb, POOL_BLK0 + ng + gi)),
            pl.BlockSpec((1, LANES, LANES), lambda bi, gi, tb: (gi, 0, 0)),
            pl.BlockSpec((1, LANES), lambda bi, gi, tb: (0, gi)),
        ],
        out_specs=pl.BlockSpec((1, tt, LANES), lambda bi, gi, tb: (bi, tb, gi)),
        out_shape=jax.ShapeDtypeStruct((b, t, POOL_W), bf16),
        scratch_shapes=[pltpu.VMEM((tt + 16, LANES), f32)],
        compiler_params=pltpu.CompilerParams(
            dimension_semantics=("arbitrary", "arbitrary", "arbitrary"), vmem_limit_bytes=VMEM_LIMIT),
        name="pool_prompt",
    )(z3, z3, pw, psc)


def _bcast8(x):
    return jnp.broadcast_to(x, (8, x.shape[-1]))


def _sample_body(z_ref, sh_ref, wkv_ref, pool_ref, ck0, cv0, ck1, cv1, ck2, cv2,
                 mu_ref, rp_ref, wup_ref, aup_ref, qnw_ref, knw_ref, cos_ref, sa_ref, sb_ref,
                 pw_ref, psc_ref,
                 mix_ref, swkv_ref, spool_ref, sk0, sv0, sk1, sv1, sk2, sv2,
                 qk_scr, *, past_len):
    z = z_ref[0]
    za = z[:, 0:RWKV_COLS]
    zb = z[:, RWKV_COLS:RWKV_COLS + ATT_COLS]
    zc = z[:, RWKV_COLS + ATT_COLS:RWKV_COLS + ATT_COLS + 2 * POOL_W]
    er = lax.broadcasted_iota(jnp.int32, (HEAD, HEAD), 0)
    ec = lax.broadcasted_iota(jnp.int32, (HEAD, HEAD), 1)
    eye = er == ec

    zs = za + (sh_ref[0] - za) * mu_ref[...]
    w_ = RWKV_W
    r, k, v, g = (zs[:, i * w_:(i + 1) * w_] for i in range(4))
    w_dn = zs[:, 4 * w_:4 * w_ + LORA]
    a_dn = zs[:, 4 * w_ + LORA:]
    rp = rp_ref[...]
    w0, a0, k_k, k_a, r_k, ln_w, ln_b = (rp[i:i + 1] for i in range(7))
    lw = _dot(_bcast8(jnp.tanh(w_dn)).astype(bf16), wup_ref[...].astype(bf16))[0:1]
    la = _dot(_bcast8(a_dn).astype(bf16), aup_ref[...].astype(bf16))[0:1]
    decay = jnp.exp(-jnp.exp(-jax.nn.softplus(-(w0 + lw)) - 0.5))
    a = jax.nn.sigmoid(a0 + la)
    kk = k * k_k
    k2 = k * (1.0 + (a - 1.0) * k_a)
    for h in range(RWKV_HEADS):
        hs = slice(h * HEAD, (h + 1) * HEAD)
        kkh = kk[:, hs]
        kkh = kkh / jnp.maximum(jnp.sqrt(jnp.sum(kkh * kkh, axis=-1, keepdims=True)), 1e-12)
        kh, vh, rh, ah = k2[:, hs], v[:, hs], r[:, hs], a[:, hs]
        s = wkv_ref[0, h]
        sa_col = jnp.sum(s * (-kkh), axis=-1, keepdims=True)
        v_col = jnp.sum(jnp.where(eye, jnp.broadcast_to(vh, (HEAD, HEAD)), 0.0), axis=-1, keepdims=True)
        sn = s * decay[:, hs] + sa_col * (kkh * ah) + v_col * kh
        swkv_ref[0, h] = sn
        o_col = jnp.sum(sn * rh, axis=-1, keepdims=True)
        o = jnp.sum(jnp.where(eye, jnp.broadcast_to(o_col, (HEAD, HEAD)), 0.0), axis=0, keepdims=True)
        mean = jnp.mean(o, axis=-1, keepdims=True)
        dl = o - mean
        var = jnp.mean(dl * dl, axis=-1, keepdims=True)
        on = dl * lax.rsqrt(var + GN_EPS) * ln_w[:, hs] + ln_b[:, hs]
        on = on + jnp.sum(rh * kh * r_k[:, hs], axis=-1, keepdims=True) * vh
        mix_ref[0, :, hs] = on * _silu(g[:, hs])

    aw = ATT_W
    q, kx, vx, gx = (zb[:, i * aw:(i + 1) * aw] for i in range(4))
    m0 = _pair_consts(8)
    for hp in range(ATT_W // LANES):
        ls = slice(hp * LANES, (hp + 1) * LANES)
        for idx, (src, nw) in enumerate(((q, qnw_ref), (kx, knw_ref))):
            x = _bcast8(src[:, ls])
            sq = x * x
            s0 = jnp.sum(jnp.where(m0, sq, 0.0), axis=-1, keepdims=True)
            s1 = jnp.sum(jnp.where(m0, 0.0, sq), axis=-1, keepdims=True)
            ms = jnp.where(m0, s0, s1) * (1.0 / HEAD)
            xn = x * lax.rsqrt(ms + NORM_EPS) * nw[...]
            qk_scr[idx, :, ls] = _rope_pair(xn, cos_ref[...], sa_ref[...], sb_ref[...])
    qn = qk_scr[0, 0:1, :]
    kn = qk_scr[1, 0:1, :]
    scale = HEAD ** -0.5

    def heads4(row, gi):
        return jnp.concatenate([row[:, (gi * 4 + j) * HEAD:(gi * 4 + j + 1) * HEAD] for j in range(4)], axis=0)

    caches = ((ck0, cv0, sk0, sv0), (ck1, cv1, sk1, sv1), (ck2, cv2, sk2, sv2))
    o_all, lse_all = [], []
    for gi, (ck, cv, sk, sv) in enumerate(caches):
        q4, k4, v4 = heads4(qn, gi), heads4(kn, gi), heads4(vx, gi)
        sk[0, 0] = k4
        sv[0, 0] = v4
        kc = ck[...]
        vc = cv[...]
        s = jnp.sum(kc * q4[None], axis=-1, keepdims=True) * scale
        s_new = jnp.sum(k4 * q4, axis=-1, keepdims=True) * scale
        m = jnp.maximum(jnp.max(s, axis=0), s_new)
        p = jnp.exp(s - m[None])
        p_new = jnp.exp(s_new - m)
        l = jnp.sum(p, axis=0) + p_new
        o_all.append((jnp.sum(p * vc, axis=0) + p_new * v4) / l)
        lse_all.append(m + jnp.log(l))
    mx = jnp.maximum(jnp.maximum(lse_all[0], lse_all[1]), lse_all[2])
    es = [jnp.exp(lse - mx) for lse in lse_all]
    inv = 1.0 / (es[0] + es[1] + es[2])
    for gi in range(3):
        yg = o_all[gi] * (es[gi] * inv) * _silu(heads4(gx, gi))
        for j in range(4):
            hd = gi * 4 + j
            mix_ref[0, :, RWKV_W + hd * HEAD:RWKV_W + (hd + 1) * HEAD] = yg[j:j + 1]

    u = zc[:, 0:POOL_W]
    gate = zc[:, POOL_W:]
    prev = pool_ref[0]
    for gi, w in enumerate(POOL_WINDOWS):
        cs = slice(gi * LANES, (gi + 1) * LANES)
        ug = u[:, cs]
        sw = jnp.sum(prev[POOL_BUF - (w - 1):POOL_BUF, cs], axis=0, keepdims=True) + ug
        cnt = min(float(w), float(past_len) + 1.0)
        d = sw / cnt - ug
        dd = _dot(_bcast8(d).astype(bf16), pw_ref[gi].astype(bf16))[0:1] * psc_ref[:, cs]
        mix_ref[0, :, RWKV_W + ATT_W + gi * LANES:RWKV_W + ATT_W + (gi + 1) * LANES] = dd * _silu(gate[:, cs])
    spool_ref[0, 0:POOL_BUF - 1, :] = prev[1:POOL_BUF]
    spool_ref[0, POOL_BUF - 1:POOL_BUF, :] = u


def _sample_step(zs, sh_all, wkv_all, pool_all, caches_all, layer, mu, rp, wup, aup, qnw, knw, cos, sa, sb,
                 pw, psc, *, past_len):
    nb = zs.shape[0]
    d_in = zs.shape[-1]
    d_mix = RWKV_W + ATT_W + POOL_W
    span = QBLK

    def full(shape):
        nd = len(shape)
        return pl.BlockSpec(shape, lambda b, nd=nd: (0,) * nd)

    in_specs = [
        pl.BlockSpec((1, 1, d_in), lambda b: (b, 0, 0)),
        pl.BlockSpec((None, 1, 1, RWKV_COLS), lambda b: (layer, b, 0, 0)),
        pl.BlockSpec((None, 1, RWKV_HEADS, HEAD, HEAD), lambda b: (layer, b, 0, 0, 0)),
        pl.BlockSpec((None, 1, POOL_BUF, POOL_W), lambda b: (layer, b, 0, 0)),
    ]
    in_specs += [pl.BlockSpec((None, None, span, None, 4, HEAD), lambda b: (layer, b, 0, 0, 0, 0))] * 6
    in_specs += [full(mu.shape), full(rp.shape), full(wup.shape), full(aup.shape), full(qnw.shape),
                 full(knw.shape), full(cos.shape), full(sa.shape), full(sb.shape), full(pw.shape),
                 full(psc.shape)]
    out_specs = [
        pl.BlockSpec((1, 1, d_mix), lambda b: (b, 0, 0)),
        pl.BlockSpec((1, RWKV_HEADS, HEAD, HEAD), lambda b: (b, 0, 0, 0)),
        pl.BlockSpec((1, POOL_BUF, POOL_W), lambda b: (b, 0, 0)),
    ] + [pl.BlockSpec((1, 1, 4, HEAD), lambda b: (b, 0, 0, 0))] * 6
    out_shape = [
        jax.ShapeDtypeStruct((nb, 1, d_mix), f32),
        jax.ShapeDtypeStruct((nb, RWKV_HEADS, HEAD, HEAD), f32),
        jax.ShapeDtypeStruct((nb, POOL_BUF, POOL_W), f32),
    ] + [jax.ShapeDtypeStruct((nb, 1, 4, HEAD), f32)] * 6
    return pl.pallas_call(
        functools.partial(_sample_body, past_len=past_len),
        grid=(nb,),
        in_specs=in_specs,
        out_specs=out_specs,
        out_shape=out_shape,
        scratch_shapes=[pltpu.VMEM((2, 8, ATT_W), f32)],
        compiler_params=pltpu.CompilerParams(
            dimension_semantics=("arbitrary",), vmem_limit_bytes=VMEM_LIMIT),
        name="sample_step",
    )(zs, sh_all, wkv_all, pool_all, *caches_all, mu, rp, wup, aup, qnw, knw, cos, sa, sb, pw, psc)


def _rope_tables(pos):
    half = ROPE_DIMS // 2
    inv = jnp.power(jnp.float32(ROPE_THETA), -jnp.arange(half, dtype=f32) * 2.0 / ROPE_DIMS)
    ang = pos[:, None] * inv[None, :]
    cos, sin = jnp.cos(ang), jnp.sin(ang)
    n = pos.shape[0]
    pad = jnp.zeros((n, HEAD - ROPE_DIMS), f32)
    zero = jnp.zeros((n, half), f32)
    c_head = jnp.concatenate([cos, cos, pad + 1.0], axis=1)
    a_head = jnp.concatenate([-sin, zero, pad], axis=1)
    b_head = jnp.concatenate([zero, sin, pad], axis=1)
    return tuple(jnp.concatenate([x, x], axis=1) for x in (c_head, a_head, b_head))


def kernel(x_prompt, x_sample, state_wkv, state_shift, state_pool,
           cache_k_w128, cache_v_w128, cache_k_w512, cache_v_w512, cache_k_w2048, cache_v_w2048,
           norm_w, w_in, w_out, rwkv_mu, rwkv_w0, rwkv_w_up, rwkv_a0, rwkv_a_up,
           rwkv_k_k, rwkv_k_a, rwkv_r_k, rwkv_ln_w, rwkv_ln_b, q_norm_w, k_norm_w, pool_w, pool_scale):
    b, t, d = x_prompt.shape
    nb, ts, _ = x_sample.shape
    depth = w_in.shape[0]
    d_in = w_in.shape[2]
    assert ts == 1 and t % 512 == 0
    past_len = 16384
    caches_in = ((cache_k_w128, cache_v_w128), (cache_k_w512, cache_v_w512), (cache_k_w2048, cache_v_w2048))
    for (window, _), (ck, _) in zip(ATT_GROUPS, caches_in):
        assert ck.shape[2] == window, "window buffers are expected to be full"

    cos_p, sa_p, sb_p = _rope_tables(jnp.arange(t, dtype=f32))
    cos_s, sa_s, sb_s = _rope_tables(past_len + jnp.arange(1, dtype=f32))

    hp = x_prompt.reshape(b * t, d)
    hs = x_sample.reshape(nb, d)
    npair = RWKV_HEADS // 2
    p_out = [[] for _ in range(9)]
    s_out = [[] for _ in range(9)]
    sh_all = state_shift.reshape(depth, nb, 1, RWKV_COLS)
    caches_all = [c.reshape(depth, nb, QBLK, dil, 4, HEAD)
                  for (_, dil), pair in zip(ATT_GROUPS, caches_in) for c in pair]
    for l in range(depth):
        zp, zs = _inproj(hp, hs, norm_w[l].reshape(1, d), w_in, l)
        z3 = zp.reshape(b, t, d_in)

        mu = rwkv_mu[l]
        mu4 = mu[:4 * RWKV_W].reshape(4, npair, LANES).transpose(1, 0, 2)
        mul = jnp.broadcast_to(mu[4 * RWKV_W:].reshape(1, 1, LANES), (npair, 1, LANES))
        vecs = jnp.stack([rwkv_w0[l], rwkv_a0[l], rwkv_k_k[l], rwkv_k_a[l], rwkv_r_k[l].reshape(-1),
                          rwkv_ln_w[l], rwkv_ln_b[l]])
        pp = jnp.concatenate([mu4, mul, vecs.reshape(7, npair, LANES).transpose(1, 0, 2)], axis=1)
        wup = rwkv_w_up[l].reshape(LORA, npair, LANES).transpose(1, 0, 2)
        aup = rwkv_a_up[l].reshape(LORA, npair, LANES).transpose(1, 0, 2)
        zl = jnp.zeros_like(wup)
        wl = jnp.concatenate([jnp.concatenate([wup, zl], axis=2), jnp.concatenate([zl, aup], axis=2)], axis=1)

        ya, p_wkv = _rwkv_prompt(z3, pp, wl)
        qnw = jnp.tile(q_norm_w[l], 2).reshape(1, LANES)
        knw = jnp.tile(k_norm_w[l], 2).reshape(1, LANES)
        att = _attn_prompt(z3, cos_p, sa_p + sb_p, qnw, knw)
        psc = pool_scale[l].reshape(1, POOL_W)
        yc = _pool_prompt(z3, pool_w[l], psc)

        sample = _sample_step(
            zs.reshape(nb, 1, -1), sh_all, state_wkv, state_pool, caches_all, l,
            mu.reshape(1, RWKV_COLS), vecs, rwkv_w_up[l], rwkv_a_up[l], qnw, knw, cos_s, sa_s, sb_s,
            pool_w[l], psc, past_len=past_len)
        mix_s = sample[0].reshape(nb, -1)

        ys = [ya.reshape(b * t, RWKV_W)] + [y.reshape(b * t, 4 * HEAD) for y in att[:3]] + [yc.reshape(b * t, POOL_W)]
        hp, hs = _outproj(hp, ys, hs, mix_s, w_out, l)

        p_out[0].append(p_wkv)
        p_out[1].append(z3[:, -1, :RWKV_COLS])
        p_out[2].append(z3[:, t - POOL_BUF:, RWKV_COLS + ATT_COLS:RWKV_COLS + ATT_COLS + POOL_W])
        for gi in range(3):
            keep = att[3 + 2 * gi].shape[1]
            p_out[3 + 2 * gi].append(att[3 + 2 * gi].reshape(b, keep, 4, HEAD))
            p_out[4 + 2 * gi].append(att[4 + 2 * gi].reshape(b, keep, 4, HEAD))
        s_out[0].append(sample[1])
        s_out[1].append(zs[:, :RWKV_COLS])
        s_out[2].append(sample[2])
        for i in range(6):
            s_out[3 + i].append(sample[3 + i])

    return (hp.reshape(b, t, d), hs.reshape(nb, 1, d),
            *[jnp.stack(x) for x in p_out], *[jnp.stack(x) for x in s_out])
```

```python
import functools

import jax
import jax.numpy as jnp
from jax import lax
from jax.experimental import pallas as pl
from jax.experimental.pallas import tpu as pltpu

f32 = jnp.float32
bf16 = jnp.bfloat16

HEAD = 64
LANES = 128
NORM_EPS = 1e-6
GN_EPS = 64e-5
RWKV_HEADS = 12
RWKV_W = RWKV_HEADS * HEAD
LORA = 64
RWKV_COLS = 4 * RWKV_W + 2 * LORA
ATT_GROUPS = ((128, 1), (512, 4), (2048, 16))
ATT_W = 12 * HEAD
ATT_COLS = 4 * ATT_W
ROPE_THETA = 500000.0
ROPE_DIMS = 16
QBLK = 128
POOL_WINDOWS = (2, 4, 8, 16)
POOL_W = 512
POOL_BUF = 15
CHUNK = 64
VMEM_LIMIT = 56 * 1024 * 1024

ATT_BLK0 = RWKV_COLS // LANES
POOL_BLK0 = (RWKV_COLS + ATT_COLS) // LANES


def _dot(a, b):
    return jnp.dot(a, b, preferred_element_type=f32)


def _dot_nt(a, b):
    return lax.dot_general(a, b, (((1,), (1,)), ((), ())), preferred_element_type=f32)


def _dot_tn(a, b):
    return lax.dot_general(a, b, (((0,), (0,)), ((), ())), preferred_element_type=f32)


def _split2(x):
    hi = x.astype(bf16)
    lo = (x - hi.astype(f32)).astype(bf16)
    return hi, lo


def _split3(x):
    hi = x.astype(bf16)
    r1 = x - hi.astype(f32)
    mid = r1.astype(bf16)
    lo = (r1 - mid.astype(f32)).astype(bf16)
    return hi, mid, lo


def _silu(x):
    return x * jax.nn.sigmoid(x)


def _rms(x, w):
    ms = jnp.mean(x * x, axis=-1, keepdims=True)
    return x * lax.rsqrt(ms + NORM_EPS) * w


def _pair_consts(n):
    lane = lax.broadcasted_iota(jnp.int32, (n, LANES), 1)
    return lane < HEAD


def _ones_blockdiag():
    r = lax.broadcasted_iota(jnp.int32, (LANES, LANES), 0)
    c = lax.broadcasted_iota(jnp.int32, (LANES, LANES), 1)
    return jnp.where((r < HEAD) == (c < HEAD), 1.0, 0.0).astype(bf16)


def _segsum(x, ones_bd):
    hi, lo = _split2(x)
    return _dot(hi, ones_bd) + _dot(lo, ones_bd)


def _inproj_body(xp_ref, xs_ref, nw_ref, w_ref, zp_ref, zs_ref, h_scr, *, tn, n):
    i = pl.program_id(0)
    j = pl.program_id(1)

    @pl.when(j == 0)
    def _():
        h_scr[...] = _rms(xp_ref[...], nw_ref[...]).astype(bf16)

    w = w_ref[...].astype(bf16)
    zp_ref[...] = _dot(h_scr[...], w)

    @pl.when(i == 0)
    def _():
        hs = _rms(xs_ref[...], nw_ref[...]).astype(bf16)
        col = pl.multiple_of(j * tn, LANES)
        zs = _dot(hs, w)
        cols = col + lax.broadcasted_iota(jnp.int32, zs.shape, 1)
        zs_ref[:, pl.ds(col, tn)] = jnp.where(cols < n, zs, 0.0)


def _inproj(xp, xs, nw, w_all, layer, *, tm=2048, tn=512):
    m, d = xp.shape
    n = w_all.shape[2]
    nb = xs.shape[0]
    nj = pl.cdiv(n, tn)
    return pl.pallas_call(
        functools.partial(_inproj_body, tn=tn, n=n),
        grid=(m // tm, nj),
        in_specs=[
            pl.BlockSpec((tm, d), lambda i, j: (i, 0), pipeline_mode=pl.Buffered(1)),
            pl.BlockSpec((nb, d), lambda i, j: (0, 0)),
            pl.BlockSpec((None, 1, d), lambda i, j: (layer, 0, 0)),
            pl.BlockSpec((None, d, tn), lambda i, j: (layer, 0, j)),
        ],
        out_specs=[
            pl.BlockSpec((tm, tn), lambda i, j: (i, j)),
            pl.BlockSpec((nb, nj * tn), lambda i, j: (0, 0)),
        ],
        out_shape=[jax.ShapeDtypeStruct((m, n), f32), jax.ShapeDtypeStruct((nb, nj * tn), f32)],
        scratch_shapes=[pltpu.VMEM((tm, d), bf16)],
        compiler_params=pltpu.CompilerParams(
            dimension_semantics=("arbitrary", "arbitrary"), vmem_limit_bytes=VMEM_LIMIT),
        name="inproj",
    )(xp, xs, nw, w_all)


def _outproj_body(*refs, tn, widths):
    n = len(widths)
    xp_ref = refs[0]
    y_refs = refs[1:1 + n]
    xs_ref, ms_ref, w_ref, op_ref, os_ref = refs[1 + n:]
    i = pl.program_id(0)
    j = pl.program_id(1)
    w = w_ref[...].astype(bf16)
    acc = None
    row = 0
    for y_ref, width in zip(y_refs, widths):
        part = _dot(y_ref[...], w[row:row + width])
        acc = part if acc is None else acc + part
        row += width
    op_ref[...] = xp_ref[...] + acc

    @pl.when(i == 0)
    def _():
        col = pl.multiple_of(j * tn, LANES)
        os_ref[:, pl.ds(col, tn)] = xs_ref[:, pl.ds(col, tn)] + _dot(ms_ref[...].astype(bf16), w)


def _outproj(xp, ys, xs, ms, w_all, layer, *, tm=2048, tn=512):
    m, d = xp.shape
    nb = xs.shape[0]
    dm = w_all.shape[1]
    widths = tuple(y.shape[1] for y in ys)
    assert sum(widths) == dm
    return pl.pallas_call(
        functools.partial(_outproj_body, tn=tn, widths=widths),
        grid=(m // tm, d // tn),
        in_specs=[pl.BlockSpec((tm, tn), lambda i, j: (i, j))]
        + [pl.BlockSpec((tm, width), lambda i, j: (i, 0)) for width in widths]
        + [
            pl.BlockSpec((nb, d), lambda i, j: (0, 0)),
            pl.BlockSpec((nb, dm), lambda i, j: (0, 0)),
            pl.BlockSpec((None, dm, tn), lambda i, j: (layer, 0, j)),
        ],
        out_specs=[
            pl.BlockSpec((tm, tn), lambda i, j: (i, j)),
            pl.BlockSpec((nb, d), lambda i, j: (0, 0)),
        ],
        out_shape=[jax.ShapeDtypeStruct((m, d), f32), jax.ShapeDtypeStruct((nb, d), f32)],
        compiler_params=pltpu.CompilerParams(
            dimension_semantics=("arbitrary", "arbitrary"), vmem_limit_bytes=VMEM_LIMIT),
        name="outproj",
    )(xp, *ys, xs, ms, w_all)


_P_MU, _P_W0, _P_A0, _P_KK, _P_KA, _P_RK, _P_LNW, _P_LNB = 0, 5, 6, 7, 8, 9, 10, 11


def _rwkv_chunk_stages(at, bt, kt, rt, v, wcs, consts):
    m0, gmask, bdmask, eye = consts
    m0w = jnp.concatenate([m0, m0], axis=1)
    zb64 = jnp.zeros((CHUNK, LANES), bf16)
    n = len(wcs)

    def rows(x, c):
        return x[c * CHUNK:(c + 1) * CHUNK]

    def split_heads(xb, m):
        zero = jnp.zeros_like(xb)
        return jnp.concatenate([jnp.where(m, xb, zero), jnp.where(m, zero, xb)], axis=0)

    g, p, x, mc, nc, rp, ov = {}, {}, {}, {}, {}, {}, {}

    def stage_scores(chs):
        for c in chs:
            lhs = jnp.concatenate([rows(at, c), rows(rt, c)], axis=0).astype(bf16)
            rhs = jnp.concatenate([split_heads(rows(bt, c).astype(bf16), m0),
                                   split_heads(rows(kt, c).astype(bf16), m0)], axis=0)
            g[c] = jnp.where(gmask, _dot_nt(lhs, rhs), 0.0)
            p[c] = g[c][0:CHUNK, 0:LANES]

    def stage_av(chs):
        for c in chs:
            vs = split_heads(rows(v, c).astype(bf16), m0)
            av = _dot(g[c][0:CHUNK, LANES:].astype(bf16), vs)
            x[c] = jnp.concatenate([av, rows(at, c)], axis=1)

    def stage_level(chs, lev):
        for c in chs:
            xx = split_heads(x[c].astype(bf16), m0w)
            pb = p[c].astype(bf16)
            if lev < 5:
                out = _dot(pb, jnp.concatenate([split_heads(pb, m0), xx], axis=1))
                p[c] = out[:, 0:LANES]
                x[c] = x[c] + out[:, LANES:]
            else:
                x[c] = x[c] + _dot(pb, xx)

    def stage_fold(chs):
        for c in chs:
            xb = x[c].astype(bf16)
            uv = xb[:, 0:LANES]
            ap = xb[:, LANES:]
            vb = rows(v, c).astype(bf16)
            tnl = jnp.concatenate([rows(bt, c) * wcs[c], rows(kt, c) * wcs[c]], axis=0).astype(bf16)
            tnr = jnp.concatenate(
                [jnp.concatenate([ap, uv], axis=1), jnp.concatenate([zb64, vb], axis=1)], axis=0)
            mn = jnp.where(bdmask, _dot_tn(tnl, tnr), 0.0)
            mc[c] = mn[:, 0:LANES] + jnp.where(eye, jnp.broadcast_to(wcs[c], (LANES, LANES)), 0.0)
            nc[c] = mn[:, LANES:]
            l2 = g[c][CHUNK:].astype(bf16)
            r2 = jnp.concatenate([
                split_heads(jnp.concatenate([ap, uv], axis=1), m0w),
                split_heads(jnp.concatenate([zb64, vb], axis=1), m0w)], axis=0)
            ro = _dot(l2, r2)
            rp[c] = rows(rt, c) + ro[:, 0:LANES]
            ov[c] = ro[:, LANES:]

    def carry(c, st):
        seq = _dot(jnp.concatenate([rp[c], mc[c]], axis=0).astype(bf16), st.astype(bf16))
        return seq[0:CHUNK] + ov[c], seq[CHUNK:] + nc[c]

    chs = list(range(n))
    stages = ([functools.partial(stage_scores, chs), functools.partial(stage_av, chs)]
              + [functools.partial(stage_level, chs, lev) for lev in range(6)]
              + [functools.partial(stage_fold, chs)])
    return stages, carry


def _rwkv_body(zr_ref, zk_ref, zv_ref, zg_ref, zl_ref, pp_ref, wl_ref, y_ref, s_ref,
               sh_scr, st_scr, *, tt, npr):
    tb = pl.program_id(2)
    nt = pl.num_programs(2)
    nch = tt // CHUNK
    nsh = 4 * npr + 1

    @pl.when(tb == 0)
    def _():
        st_scr[...] = jnp.zeros_like(st_scr)
        for i in range(nsh):
            sh_scr[i, 7:8, :] = jnp.zeros((1, LANES), f32)

    @pl.when(tb > 0)
    def _():
        for i in range(nsh):
            sh_scr[i, 7:8, :] = sh_scr[i, 7 + tt:8 + tt, :]

    def shifted(idx, z, mu):
        sh_scr[idx, 8:8 + tt, :] = z
        zprev = sh_scr[idx, pl.ds(7, tt), :]
        return z + (zprev - z) * mu

    m0t = _pair_consts(tt)
    ones_bd = _ones_blockdiag()
    lo = shifted(4 * npr, zl_ref[0], pp_ref[0, _P_MU + 4:_P_MU + 5])
    xl = jnp.where(m0t, jnp.tanh(lo), lo).astype(bf16)

    m0 = _pair_consts(CHUNK)
    gr = lax.broadcasted_iota(jnp.int32, (2 * CHUNK, 2 * LANES), 0)
    gc = lax.broadcasted_iota(jnp.int32, (2 * CHUNK, 2 * LANES), 1)
    gt = gr % CHUNK
    gs = gc % CHUNK
    gmask = (gt > gs) | ((gr >= CHUNK) & (gt == gs))
    br = lax.broadcasted_iota(jnp.int32, (LANES, 2 * LANES), 0)
    bc = lax.broadcasted_iota(jnp.int32, (LANES, 2 * LANES), 1)
    bdmask = (br < HEAD) == ((bc % LANES) < HEAD)
    er = lax.broadcasted_iota(jnp.int32, (LANES, LANES), 0)
    ec = lax.broadcasted_iota(jnp.int32, (LANES, LANES), 1)
    eye = er == ec
    consts = (m0, gmask, bdmask, eye)
    tr = lax.broadcasted_iota(jnp.int32, (CHUNK, CHUNK), 0)
    tc = lax.broadcasted_iota(jnp.int32, (CHUNK, CHUNK), 1)
    tril = jnp.where(tc <= tr, 1.0, 0.0).astype(bf16)

    def prepare(s, out):
        pp = pp_ref[s]
        ls = slice(s * LANES, (s + 1) * LANES)
        r, k, v, g = [shifted(4 * s + i, ref[0, :, ls], pp[_P_MU + i:_P_MU + i + 1])
                      for i, ref in enumerate((zr_ref, zk_ref, zv_ref, zg_ref))]
        la = _dot(xl, wl_ref[s].astype(bf16))
        kk = k * pp[_P_KK:_P_KK + 1]
        kk_ss = _segsum(kk * kk, ones_bd)
        yield
        w_log = -jax.nn.softplus(-(pp[_P_W0:_P_W0 + 1] + la[:, 0:LANES])) - 0.5
        ld = -jnp.exp(w_log)
        a = jax.nn.sigmoid(pp[_P_A0:_P_A0 + 1] + la[:, LANES:])
        kk = kk / jnp.maximum(jnp.sqrt(kk_ss), 1e-12)
        k2 = k * (1.0 + (a - 1.0) * pp[_P_KA:_P_KA + 1])
        h3 = _split3(ld)
        cl = jnp.concatenate([
            sum(_dot(tril, h[c * CHUNK:(c + 1) * CHUNK]) for h in h3) for c in range(nch)],
            axis=0)
        bonus = _segsum(r * k2 * pp[_P_RK:_P_RK + 1], ones_bd) * v
        yield
        e_in = jnp.exp(cl)
        e_neg = jnp.exp(-cl)
        wcs = [e_in[(c + 1) * CHUNK - 1:(c + 1) * CHUNK] for c in range(nch)]
        out["stages"], out["carry"] = _rwkv_chunk_stages(
            -kk * jnp.exp(cl - ld), kk * a * e_neg, k2 * e_neg, r * e_in, v, wcs, consts)
        out["epi"] = (pp, g, bonus)

    def finish(s, outs, st, epi):
        pp, g, bonus = epi
        o = jnp.concatenate(outs, axis=0)
        mean = _segsum(o, ones_bd) * (1.0 / HEAD)
        dl = o - mean
        var = _segsum(dl * dl, ones_bd) * (1.0 / HEAD)
        on = dl * lax.rsqrt(var + GN_EPS) * pp[_P_LNW:_P_LNW + 1] + pp[_P_LNB:_P_LNB + 1] + bonus
        y_ref[0, :, s * LANES:(s + 1) * LANES] = (on * _silu(g)).astype(bf16)
        st_scr[s] = st

    def drain(gen):
        for _ in gen:
            pass

    preps = [dict() for _ in range(npr)]
    drain(prepare(0, preps[0]))
    prev = None
    for s in range(npr):
        nxt = prepare(s + 1, preps[s + 1]) if s + 1 < npr else iter(())
        for stage in preps[s]["stages"]:
            stage()
            if prev is not None and prev["todo"]:
                prev["step"]()
            next(nxt, None)
        drain(nxt)
        if prev is not None:
            while prev["todo"]:
                prev["step"]()
            finish(prev["s"], prev["outs"], prev["st"][0], prev["epi"])
        cur = {"s": s, "outs": [], "st": [st_scr[s]], "epi": preps[s]["epi"], "todo": list(range(nch))}

        def step(cur=cur, carry=preps[s]["carry"]):
            o_c, cur["st"][0] = carry(cur["todo"].pop(0), cur["st"][0])
            cur["outs"].append(o_c)

        cur["step"] = step
        prev = cur
    while prev["todo"]:
        prev["step"]()
    finish(prev["s"], prev["outs"], prev["st"][0], prev["epi"])

    @pl.when(tb == nt - 1)
    def _():
        for s in range(npr):
            stt = st_scr[s].T
            s_ref[0, 2 * s] = stt[0:HEAD, 0:HEAD]
            s_ref[0, 2 * s + 1] = stt[HEAD:, HEAD:]


def _rwkv_prompt(z3, pp, wl, layer, *, tt=512, npr=3):
    b, t, _ = z3.shape
    npair = RWKV_HEADS // 2
    ngrp = npair // npr
    wide = npr * LANES

    def zspec(part):
        return pl.BlockSpec((1, tt, wide), lambda bi, p, tb, part=part: (bi, tb, part * ngrp + p))

    lora_blk = 4 * RWKV_W // LANES
    return pl.pallas_call(
        functools.partial(_rwkv_body, tt=tt, npr=npr),
        grid=(b, ngrp, t // tt),
        in_specs=[
            zspec(0), zspec(1), zspec(2), zspec(3),
            pl.BlockSpec((1, tt, LANES), lambda bi, p, tb: (bi, tb, lora_blk)),
            pl.BlockSpec((None, npr, 12, LANES), lambda bi, p, tb: (layer, p, 0, 0)),
            pl.BlockSpec((None, npr, LANES, 2 * LANES), lambda bi, p, tb: (layer, p, 0, 0)),
        ],
        out_specs=[
            pl.BlockSpec((1, tt, wide), lambda bi, p, tb: (bi, tb, p)),
            pl.BlockSpec((1, 2 * npr, HEAD, HEAD), lambda bi, p, tb: (bi, p, 0, 0)),
        ],
        out_shape=[jax.ShapeDtypeStruct((b, t, RWKV_W), bf16),
                   jax.ShapeDtypeStruct((b, RWKV_HEADS, HEAD, HEAD), f32)],
        scratch_shapes=[pltpu.VMEM((4 * npr + 1, tt + 8, LANES), f32), pltpu.VMEM((npr, LANES, LANES), f32)],
        compiler_params=pltpu.CompilerParams(
            dimension_semantics=("arbitrary", "arbitrary", "arbitrary"), vmem_limit_bytes=VMEM_LIMIT),
        name="rwkv_prompt",
    )(z3, z3, z3, z3, z3, pp, wl)


def _rope_pair(x, cos, sa, sb):
    return x * cos + pltpu.roll(x, LANES - ROPE_DIMS // 2, 1) * sa + pltpu.roll(x, ROPE_DIMS // 2, 1) * sb


def _attn_body(*refs, t):
    qkvg = [refs[4 * gi:4 * gi + 4] for gi in range(3)]
    cos_ref, sin_ref, qnw_ref, knw_ref = refs[12:16]
    y_refs = refs[16:19]
    kv_refs = [refs[19 + 2 * gi:21 + 2 * gi] for gi in range(3)]
    qn_scr, kn_scr, o_scr, lse_scr = refs[25:29]

    rb = 256
    ones_bd = _ones_blockdiag()
    m0q = _pair_consts(QBLK)
    scale = HEAD ** -0.5
    half = ROPE_DIMS // 2
    pj = lax.broadcasted_iota(jnp.int32, (LANES, LANES), 0)
    pi = lax.broadcasted_iota(jnp.int32, (LANES, LANES), 1)
    pin = pi % HEAD
    perm = jnp.where(((pin < half) & (pj == pi + half)) | ((pin >= half) & (pin < ROPE_DIMS) & (pj == pi - half)),
                     1.0, 0.0).astype(bf16)

    for gi, (window, dil) in enumerate(ATT_GROUPS):
        q_ref, k_ref, v_ref, g_ref = qkvg[gi]
        pk_ref, pv_ref = kv_refs[gi]
        keep = min(window, t)

        def norm_rows(i, carry, q_ref=q_ref, k_ref=k_ref):
            jobs = []
            for u in range(2):
                rows = pl.ds(pl.multiple_of((2 * i + u) * rb, rb), rb)
                for src, nw, dst in ((q_ref, qnw_ref, qn_scr), (k_ref, knw_ref, kn_scr)):
                    x = src[0, rows, :]
                    jobs.append((x, _segsum(x * x, ones_bd), nw, dst, rows))
            for x, ss, nw, dst, rows in jobs:
                xn = x * lax.rsqrt(ss * (1.0 / HEAD) + NORM_EPS) * nw[...]
                hi, lo = _split2(xn)
                swapped = _dot(hi, perm) + _dot(lo, perm)
                dst[rows, :] = xn * cos_ref[rows, :] + swapped * sin_ref[rows, :]
            return carry

        lax.fori_loop(0, t // (2 * rb), norm_rows, 0)
        pk_ref[0] = kn_scr[t - keep:t, :]
        pv_ref[0] = v_ref[0, t - keep:t, :]

        n_sub = t // dil
        n_blk = n_sub // QBLK

        def tiles(starts, nk, first, v_ref=v_ref, dil=dil, gi=gi):
            qi = lax.broadcasted_iota(jnp.int32, (QBLK, nk), 0)
            kj = lax.broadcasted_iota(jnp.int32, (QBLK, nk), 1)
            mask = (kj <= qi) if first else ((kj >= qi) & (kj <= qi + QBLK))
            scores, vts = [], []
            for q0, k0 in starts:
                qt = qn_scr[pl.ds(q0, QBLK, stride=dil), :] * scale
                kt = kn_scr[pl.ds(k0, nk, stride=dil), :].astype(bf16)
                vts.append(v_ref[0, pl.ds(k0, nk, stride=dil), :].astype(bf16))
                for hh in range(2):
                    qh = jnp.where(m0q, qt, 0.0) if hh == 0 else jnp.where(m0q, 0.0, qt)
                    scores.append(_dot_nt(qh.astype(bf16), kt))
            probs, sums, lses = [], [], []
            for s in scores:
                s = jnp.where(mask, s, -jnp.inf)
                m = jnp.max(s, axis=-1, keepdims=True)
                p = jnp.exp(s - m)
                l = jnp.sum(p, axis=-1, keepdims=True)
                probs.append(p.astype(bf16))
                sums.append(l)
                lses.append(m + jnp.log(l))
            for ti, (q0, _) in enumerate(starts):
                oh = [_dot(probs[2 * ti + hh], vts[ti]) / sums[2 * ti + hh] for hh in range(2)]
                o_scr[gi, pl.ds(q0, QBLK, stride=dil), :] = jnp.where(m0q, oh[0], oh[1])
                lse_scr[gi, pl.ds(q0, QBLK, stride=dil), :] = jnp.where(
                    m0q, jnp.broadcast_to(lses[2 * ti], (QBLK, LANES)),
                    jnp.broadcast_to(lses[2 * ti + 1], (QBLK, LANES)))

        per_first = min(4, dil)

        def first_tiles(i, carry, tiles=tiles, per=per_first):
            tiles([(i * per + u, i * per + u) for u in range(per)], QBLK, True)
            return carry

        lax.fori_loop(0, dil // per_first, first_tiles, 0)

        if n_blk > 1:
            per_later = 3
            assert (dil * (n_blk - 1)) % per_later == 0

            def later_tiles(i, carry, tiles=tiles, dil=dil, per=per_later):
                starts = []
                for u in range(per):
                    idx = i * per + u
                    r = idx % dil
                    blk = idx // dil + 1
                    starts.append((r + blk * QBLK * dil, r + (blk - 1) * QBLK * dil))
                tiles(starts, 2 * QBLK, False)
                return carry

            lax.fori_loop(0, dil * (n_blk - 1) // per_later, later_tiles, 0)

    def combine(i, carry):
        rows = pl.ds(pl.multiple_of(i * rb, rb), rb)
        ls = [lse_scr[gi, rows, :] for gi in range(3)]
        mx = jnp.maximum(jnp.maximum(ls[0], ls[1]), ls[2])
        es = [jnp.exp(l - mx) for l in ls]
        inv = 1.0 / (es[0] + es[1] + es[2])
        for gi in range(3):
            gate = qkvg[gi][3][0, rows, :]
            y_refs[gi][0, rows, :] = (o_scr[gi, rows, :] * (es[gi] * inv) * _silu(gate)).astype(bf16)
        return carry

    lax.fori_loop(0, t // rb, combine, 0)


def _attn_prompt(z3, cos, sin, qnw, knw, layer):
    b, t, _ = z3.shape
    in_specs = []
    for gi in range(3):
        for part in range(4):
            off = ATT_BLK0 + part * 6 + 2 * gi
            in_specs.append(pl.BlockSpec((1, t, LANES), lambda bi, jp, off=off: (bi, 0, off + jp)))
    in_specs += [pl.BlockSpec((t, LANES), lambda bi, jp: (0, 0))] * 2
    in_specs += [pl.BlockSpec((None, 1, LANES), lambda bi, jp: (layer, 0, 0))] * 2
    out_specs = [pl.BlockSpec((1, t, LANES), lambda bi, jp: (bi, 0, jp))] * 3
    out_shape = [jax.ShapeDtypeStruct((b, t, 4 * HEAD), bf16)] * 3
    for window, _ in ATT_GROUPS:
        keep = min(window, t)
        out_specs += [pl.BlockSpec((1, keep, LANES), lambda bi, jp: (bi, 0, jp))] * 2
        out_shape += [jax.ShapeDtypeStruct((b, keep, 4 * HEAD), f32)] * 2
    outs = pl.pallas_call(
        functools.partial(_attn_body, t=t),
        grid=(b, 2),
        in_specs=in_specs,
        out_specs=out_specs,
        out_shape=out_shape,
        scratch_shapes=[pltpu.VMEM((t, LANES), f32), pltpu.VMEM((t, LANES), f32),
                        pltpu.VMEM((3, t, LANES), f32), pltpu.VMEM((3, t, LANES), f32)],
        compiler_params=pltpu.CompilerParams(
            dimension_semantics=("arbitrary", "arbitrary"), vmem_limit_bytes=VMEM_LIMIT),
        name="attn_prompt",
    )(*([z3] * 12), cos, sin, qnw, knw)
    return outs


def _pool_body(u_ref, g_ref, w_ref, sc_ref, y_ref, scr, *, tt):
    gi = pl.program_id(1)
    tb = pl.program_id(2)

    @pl.when(tb == 0)
    def _():
        scr[0:16, :] = jnp.zeros((16, LANES), f32)

    @pl.when(tb > 0)
    def _():
        scr[0:16, :] = scr[tt:tt + 16, :]

    u = u_ref[0]
    scr[16:16 + tt, :] = u
    acc = u
    sums = {}
    for k in range(1, max(POOL_WINDOWS)):
        acc = acc + scr[pl.ds(16 - k, tt), :]
        if k + 1 in POOL_WINDOWS:
            sums[k + 1] = acc
    sw = jnp.where(gi == 0, sums[2], jnp.where(gi == 1, sums[4], jnp.where(gi == 2, sums[8], sums[16])))
    wf = jnp.left_shift(2, gi).astype(f32)
    pos = (tb * tt + lax.broadcasted_iota(jnp.int32, (tt, 1), 0)).astype(f32)
    cnt = jnp.minimum(wf, pos + 1.0)
    d = sw / cnt - u
    dd = _dot(d.astype(bf16), w_ref[0].astype(bf16)) * sc_ref[...]
    y_ref[0] = (dd * _silu(g_ref[0])).astype(bf16)


def _pool_prompt(z3, pw, psc, layer, *, tt=512):
    b, t, _ = z3.shape
    ng = len(POOL_WINDOWS)
    return pl.pallas_call(
        functools.partial(_pool_body, tt=tt),
        grid=(b, ng, t // tt),
        in_specs=[
            pl.BlockSpec((1, tt, LANES), lambda bi, gi, tb: (bi, tb, POOL_BLK0 + gi)),
            pl.BlockSpec((1, tt, LANES), lambda bi, gi, tb: (bi, tb, POOL_BLK0 + ng + gi)),
            pl.BlockSpec((None, 1, LANES, LANES), lambda bi, gi, tb: (layer, gi, 0, 0)),
            pl.BlockSpec((None, 1, LANES), lambda bi, gi, tb: (layer, 0, gi)),
        ],
        out_specs=pl.BlockSpec((1, tt, LANES), lambda bi, gi, tb: (bi, tb, gi)),
        out_shape=jax.ShapeDtypeStruct((b, t, POOL_W), bf16),
        scratch_shapes=[pltpu.VMEM((tt + 16, LANES), f32)],
        compiler_params=pltpu.CompilerParams(
            dimension_semantics=("arbitrary", "arbitrary", "arbitrary"), vmem_limit_bytes=VMEM_LIMIT),
        name="pool_prompt",
    )(z3, z3, pw, psc)


def _bcast8(x):
    return jnp.broadcast_to(x, (8, x.shape[-1]))


def _sample_body(z_ref, sh_ref, wkv_ref, pool_ref, ck0, cv0, ck1, cv1, ck2, cv2,
                 mu_ref, rp_ref, wup_ref, aup_ref, qnw_ref, knw_ref, cos_ref, sa_ref, sb_ref,
                 pw_ref, psc_ref,
                 mix_ref, swkv_ref, spool_ref, sk0, sv0, sk1, sv1, sk2, sv2,
                 qk_scr, *, past_len):
    z = z_ref[0]
    za = z[:, 0:RWKV_COLS]
    zb = z[:, RWKV_COLS:RWKV_COLS + ATT_COLS]
    zc = z[:, RWKV_COLS + ATT_COLS:RWKV_COLS + ATT_COLS + 2 * POOL_W]
    er = lax.broadcasted_iota(jnp.int32, (HEAD, HEAD), 0)
    ec = lax.broadcasted_iota(jnp.int32, (HEAD, HEAD), 1)
    eye = er == ec

    zs = za + (sh_ref[0] - za) * mu_ref[...]
    w_ = RWKV_W
    r, k, v, g = (zs[:, i * w_:(i + 1) * w_] for i in range(4))
    w_dn = zs[:, 4 * w_:4 * w_ + LORA]
    a_dn = zs[:, 4 * w_ + LORA:]
    rp = rp_ref[...]
    w0, a0, k_k, k_a, r_k, ln_w, ln_b = (rp[i:i + 1] for i in range(7))
    lw = _dot(_bcast8(jnp.tanh(w_dn)).astype(bf16), wup_ref[...].astype(bf16))[0:1]
    la = _dot(_bcast8(a_dn).astype(bf16), aup_ref[...].astype(bf16))[0:1]
    decay = jnp.exp(-jnp.exp(-jax.nn.softplus(-(w0 + lw)) - 0.5))
    a = jax.nn.sigmoid(a0 + la)
    kk = k * k_k
    k2 = k * (1.0 + (a - 1.0) * k_a)
    for h in range(RWKV_HEADS):
        hs = slice(h * HEAD, (h + 1) * HEAD)
        kkh = kk[:, hs]
        kkh = kkh / jnp.maximum(jnp.sqrt(jnp.sum(kkh * kkh, axis=-1, keepdims=True)), 1e-12)
        kh, vh, rh, ah = k2[:, hs], v[:, hs], r[:, hs], a[:, hs]
        s = wkv_ref[0, h]
        sa_col = jnp.sum(s * (-kkh), axis=-1, keepdims=True)
        v_col = jnp.sum(jnp.where(eye, jnp.broadcast_to(vh, (HEAD, HEAD)), 0.0), axis=-1, keepdims=True)
        sn = s * decay[:, hs] + sa_col * (kkh * ah) + v_col * kh
        swkv_ref[0, h] = sn
        o_col = jnp.sum(sn * rh, axis=-1, keepdims=True)
        o = jnp.sum(jnp.where(eye, jnp.broadcast_to(o_col, (HEAD, HEAD)), 0.0), axis=0, keepdims=True)
        mean = jnp.mean(o, axis=-1, keepdims=True)
        dl = o - mean
        var = jnp.mean(dl * dl, axis=-1, keepdims=True)
        on = dl * lax.rsqrt(var + GN_EPS) * ln_w[:, hs] + ln_b[:, hs]
        on = on + jnp.sum(rh * kh * r_k[:, hs], axis=-1, keepdims=True) * vh
        mix_ref[0, :, hs] = on * _silu(g[:, hs])

    aw = ATT_W
    q, kx, vx, gx = (zb[:, i * aw:(i + 1) * aw] for i in range(4))
    m0 = _pair_consts(8)
    for hp in range(ATT_W // LANES):
        ls = slice(hp * LANES, (hp + 1) * LANES)
        for idx, (src, nw) in enumerate(((q, qnw_ref), (kx, knw_ref))):
            x = _bcast8(src[:, ls])
            sq = x * x
            s0 = jnp.sum(jnp.where(m0, sq, 0.0), axis=-1, keepdims=True)
            s1 = jnp.sum(jnp.where(m0, 0.0, sq), axis=-1, keepdims=True)
            ms = jnp.where(m0, s0, s1) * (1.0 / HEAD)
            xn = x * lax.rsqrt(ms + NORM_EPS) * nw[...]
            qk_scr[idx, :, ls] = _rope_pair(xn, cos_ref[...], sa_ref[...], sb_ref[...])
    qn = qk_scr[0, 0:1, :]
    kn = qk_scr[1, 0:1, :]
    scale = HEAD ** -0.5

    def heads4(row, gi):
        return jnp.concatenate([row[:, (gi * 4 + j) * HEAD:(gi * 4 + j + 1) * HEAD] for j in range(4)], axis=0)

    def to_col(row):
        return jnp.sum(jnp.where(eye, jnp.broadcast_to(row, (HEAD, HEAD)), 0.0), axis=-1, keepdims=True)

    def to_row(col):
        return jnp.sum(jnp.where(eye, jnp.broadcast_to(col, (HEAD, HEAD)), 0.0), axis=0, keepdims=True)

    caches = ((ck0, cv0, sk0, sv0), (ck1, cv1, sk1, sv1), (ck2, cv2, sk2, sv2))
    o_all, lse_all = [], []
    for gi, (ck, cv, sk, sv) in enumerate(caches):
        dil = ATT_GROUPS[gi][1]
        sk[0, 0] = heads4(kn, gi)
        sv[0, 0] = heads4(vx, gi)
        n_buf = ck.shape[-1]
        row_id = lax.broadcasted_iota(jnp.int32, (1, n_buf), 1)
        in_window = (row_id % dil) == 0
        og, lg = [], []
        for j in range(4):
            hd = gi * 4 + j
            hs = slice(hd * HEAD, (hd + 1) * HEAD)
            qh = qn[:, hs] * scale
            kt = ck[0, j]
            vt = cv[0, j]
            s = jnp.sum(kt * to_col(qh), axis=0, keepdims=True)
            s = jnp.where(in_window, s, -jnp.inf)
            s_new = jnp.sum(kn[:, hs] * qh, axis=-1, keepdims=True)
            m = jnp.maximum(jnp.max(s, axis=-1, keepdims=True), s_new)
            p = jnp.exp(s - m)
            p_new = jnp.exp(s_new - m)
            l = jnp.sum(p, axis=-1, keepdims=True) + p_new
            o_col = jnp.sum(vt * p, axis=-1, keepdims=True)
            og.append((to_row(o_col) + p_new * vx[:, hs]) / l)
            lg.append(m + jnp.log(l))
        o_all.append(og)
        lse_all.append(lg)
    for j in range(4):
        mx = jnp.maximum(jnp.maximum(lse_all[0][j], lse_all[1][j]), lse_all[2][j])
        es = [jnp.exp(lse_all[gi][j] - mx) for gi in range(3)]
        inv = 1.0 / (es[0] + es[1] + es[2])
        for gi in range(3):
            hd = gi * 4 + j
            hs = slice(hd * HEAD, (hd + 1) * HEAD)
            mix_ref[0, :, RWKV_W + hd * HEAD:RWKV_W + (hd + 1) * HEAD] = (
                o_all[gi][j] * (es[gi] * inv) * _silu(gx[:, hs]))

    u = zc[:, 0:POOL_W]
    gate = zc[:, POOL_W:]
    prev = pool_ref[0]
    for gi, w in enumerate(POOL_WINDOWS):
        cs = slice(gi * LANES, (gi + 1) * LANES)
        ug = u[:, cs]
        sw = jnp.sum(prev[POOL_BUF - (w - 1):POOL_BUF, cs], axis=0, keepdims=True) + ug
        cnt = min(float(w), float(past_len) + 1.0)
        d = sw / cnt - ug
        dd = _dot(_bcast8(d).astype(bf16), pw_ref[gi].astype(bf16))[0:1] * psc_ref[:, cs]
        mix_ref[0, :, RWKV_W + ATT_W + gi * LANES:RWKV_W + ATT_W + (gi + 1) * LANES] = dd * _silu(gate[:, cs])
    spool_ref[0, 0:POOL_BUF - 1, :] = prev[1:POOL_BUF]
    spool_ref[0, POOL_BUF - 1:POOL_BUF, :] = u


def _sample_step(zs, sh_all, wkv_all, pool_all, caches_all, layer, mu, rp, wup, aup, qnw, knw, cos, sa, sb,
                 pw, psc, *, past_len):
    nb = zs.shape[0]
    d_in = zs.shape[-1]
    d_mix = RWKV_W + ATT_W + POOL_W

    def full(shape):
        nd = len(shape)
        return pl.BlockSpec(shape, lambda b, nd=nd: (0,) * nd)

    in_specs = [
        pl.BlockSpec((1, 1, d_in), lambda b: (b, 0, 0)),
        pl.BlockSpec((None, 1, 1, RWKV_COLS), lambda b: (layer, b, 0, 0)),
        pl.BlockSpec((None, 1, RWKV_HEADS, HEAD, HEAD), lambda b: (layer, b, 0, 0, 0)),
        pl.BlockSpec((None, 1, POOL_BUF, POOL_W), lambda b: (layer, b, 0, 0)),
    ]
    in_specs += [pl.BlockSpec((None, 1, 4, HEAD, c.shape[-1]), lambda b: (layer, b, 0, 0, 0)) for c in caches_all]
    def of_layer(x):
        nd = x.ndim - 1
        return pl.BlockSpec((None,) + x.shape[1:], lambda b, nd=nd: (layer,) + (0,) * nd)

    in_specs += [of_layer(mu), of_layer(rp), of_layer(wup), of_layer(aup), of_layer(qnw), of_layer(knw),
                 full(cos.shape), full(sa.shape), full(sb.shape), of_layer(pw), of_layer(psc)]
    out_specs = [
        pl.BlockSpec((1, 1, d_mix), lambda b: (b, 0, 0)),
        pl.BlockSpec((1, RWKV_HEADS, HEAD, HEAD), lambda b: (b, 0, 0, 0)),
        pl.BlockSpec((1, POOL_BUF, POOL_W), lambda b: (b, 0, 0)),
    ] + [pl.BlockSpec((1, 1, 4, HEAD), lambda b: (b, 0, 0, 0))] * 6
    out_shape = [
        jax.ShapeDtypeStruct((nb, 1, d_mix), f32),
        jax.ShapeDtypeStruct((nb, RWKV_HEADS, HEAD, HEAD), f32),
        jax.ShapeDtypeStruct((nb, POOL_BUF, POOL_W), f32),
    ] + [jax.ShapeDtypeStruct((nb, 1, 4, HEAD), f32)] * 6
    return pl.pallas_call(
        functools.partial(_sample_body, past_len=past_len),
        grid=(nb,),
        in_specs=in_specs,
        out_specs=out_specs,
        out_shape=out_shape,
        scratch_shapes=[pltpu.VMEM((2, 8, ATT_W), f32)],
        compiler_params=pltpu.CompilerParams(
            dimension_semantics=("arbitrary",), vmem_limit_bytes=VMEM_LIMIT),
        name="sample_step",
    )(zs, sh_all, wkv_all, pool_all, *caches_all, mu, rp, wup, aup, qnw, knw, cos, sa, sb, pw, psc)


def _rope_tables(pos):
    half = ROPE_DIMS // 2
    inv = jnp.power(jnp.float32(ROPE_THETA), -jnp.arange(half, dtype=f32) * 2.0 / ROPE_DIMS)
    ang = pos[:, None] * inv[None, :]
    cos, sin = jnp.cos(ang), jnp.sin(ang)
    n = pos.shape[0]
    pad = jnp.zeros((n, HEAD - ROPE_DIMS), f32)
    zero = jnp.zeros((n, half), f32)
    c_head = jnp.concatenate([cos, cos, pad + 1.0], axis=1)
    a_head = jnp.concatenate([-sin, zero, pad], axis=1)
    b_head = jnp.concatenate([zero, sin, pad], axis=1)
    return tuple(jnp.concatenate([x, x], axis=1) for x in (c_head, a_head, b_head))


def kernel(x_prompt, x_sample, state_wkv, state_shift, state_pool,
           cache_k_w128, cache_v_w128, cache_k_w512, cache_v_w512, cache_k_w2048, cache_v_w2048,
           norm_w, w_in, w_out, rwkv_mu, rwkv_w0, rwkv_w_up, rwkv_a0, rwkv_a_up,
           rwkv_k_k, rwkv_k_a, rwkv_r_k, rwkv_ln_w, rwkv_ln_b, q_norm_w, k_norm_w, pool_w, pool_scale):
    b, t, d = x_prompt.shape
    nb, ts, _ = x_sample.shape
    depth = w_in.shape[0]
    d_in = w_in.shape[2]
    assert ts == 1 and t % 512 == 0
    past_len = 16384
    caches_in = ((cache_k_w128, cache_v_w128), (cache_k_w512, cache_v_w512), (cache_k_w2048, cache_v_w2048))
    for (window, _), (ck, _) in zip(ATT_GROUPS, caches_in):
        assert ck.shape[2] == window, "window buffers are expected to be full"

    cos_p, sa_p, sb_p = _rope_tables(jnp.arange(t, dtype=f32))
    cos_s, sa_s, sb_s = _rope_tables(past_len + jnp.arange(1, dtype=f32))

    hp = x_prompt.reshape(b * t, d)
    hs = x_sample.reshape(nb, d)
    npair = RWKV_HEADS // 2
    p_out = [[] for _ in range(9)]
    s_out = [[] for _ in range(9)]
    sh_all = state_shift.reshape(depth, nb, 1, RWKV_COLS)
    caches_all = [jnp.transpose(c, (0, 1, 3, 4, 2)) for pair in caches_in for c in pair]

    nw_all = norm_w.reshape(depth, 1, d)
    mu_all = rwkv_mu.reshape(depth, 1, RWKV_COLS)
    mu4 = rwkv_mu[:, :4 * RWKV_W].reshape(depth, 4, npair, LANES).transpose(0, 2, 1, 3)
    mul = jnp.broadcast_to(rwkv_mu[:, 4 * RWKV_W:].reshape(depth, 1, 1, LANES), (depth, npair, 1, LANES))
    vecs_all = jnp.stack([rwkv_w0, rwkv_a0, rwkv_k_k, rwkv_k_a, rwkv_r_k.reshape(depth, -1),
                          rwkv_ln_w, rwkv_ln_b], axis=1)
    pp_all = jnp.concatenate(
        [mu4, mul, vecs_all.reshape(depth, 7, npair, LANES).transpose(0, 2, 1, 3)], axis=2)
    wup = rwkv_w_up.reshape(depth, LORA, npair, LANES).transpose(0, 2, 1, 3)
    aup = rwkv_a_up.reshape(depth, LORA, npair, LANES).transpose(0, 2, 1, 3)
    zl = jnp.zeros_like(wup)
    wl_all = jnp.concatenate(
        [jnp.concatenate([wup, zl], axis=3), jnp.concatenate([zl, aup], axis=3)], axis=2)
    qnw_all = jnp.tile(q_norm_w, (1, 2)).reshape(depth, 1, LANES)
    knw_all = jnp.tile(k_norm_w, (1, 2)).reshape(depth, 1, LANES)
    psc_all = pool_scale.reshape(depth, 1, POOL_W)
    sin_p = sa_p + sb_p

    for l in range(depth):
        zp, zs = _inproj(hp, hs, nw_all, w_in, l)
        z3 = zp.reshape(b, t, d_in)

        ya, p_wkv = _rwkv_prompt(z3, pp_all, wl_all, l)
        att = _attn_prompt(z3, cos_p, sin_p, qnw_all, knw_all, l)
        yc = _pool_prompt(z3, pool_w, psc_all, l)

        sample = _sample_step(
            zs.reshape(nb, 1, -1), sh_all, state_wkv, state_pool, caches_all, l,
            mu_all, vecs_all, rwkv_w_up, rwkv_a_up, qnw_all, knw_all, cos_s, sa_s, sb_s,
            pool_w, psc_all, past_len=past_len)
        mix_s = sample[0].reshape(nb, -1)

        ys = [ya.reshape(b * t, RWKV_W)] + [y.reshape(b * t, 4 * HEAD) for y in att[:3]] + [yc.reshape(b * t, POOL_W)]
        hp, hs = _outproj(hp, ys, hs, mix_s, w_out, l)

        p_out[0].append(p_wkv)
        p_out[1].append(z3[:, -1, :RWKV_COLS])
        p_out[2].append(z3[:, t - POOL_BUF:, RWKV_COLS + ATT_COLS:RWKV_COLS + ATT_COLS + POOL_W])
        for gi in range(3):
            keep = att[3 + 2 * gi].shape[1]
            p_out[3 + 2 * gi].append(att[3 + 2 * gi].reshape(b, keep, 4, HEAD))
            p_out[4 + 2 * gi].append(att[4 + 2 * gi].reshape(b, keep, 4, HEAD))
        s_out[0].append(sample[1])
        s_out[1].append(zs[:, :RWKV_COLS])
        s_out[2].append(sample[2])
        for i in range(6):
            s_out[3 + i].append(sample[3 + i])

    return (hp.reshape(b, t, d), hs.reshape(nb, 1, d),
            *[jnp.stack(x) for x in p_out], *[jnp.stack(x) for x in s_out])
```

```python
import functools

import jax
import jax.numpy as jnp
from jax import lax
from jax.experimental import pallas as pl
from jax.experimental.pallas import tpu as pltpu

f32 = jnp.float32
bf16 = jnp.bfloat16

HEAD = 64
LANES = 128
NORM_EPS = 1e-6
GN_EPS = 64e-5
RWKV_HEADS = 12
RWKV_W = RWKV_HEADS * HEAD
LORA = 64
RWKV_COLS = 4 * RWKV_W + 2 * LORA
ATT_GROUPS = ((128, 1), (512, 4), (2048, 16))
ATT_W = 12 * HEAD
ATT_COLS = 4 * ATT_W
ROPE_THETA = 500000.0
ROPE_DIMS = 16
QBLK = 128
POOL_WINDOWS = (2, 4, 8, 16)
POOL_W = 512
POOL_BUF = 15
CHUNK = 64
VMEM_LIMIT = 56 * 1024 * 1024

ATT_BLK0 = RWKV_COLS // LANES
POOL_BLK0 = (RWKV_COLS + ATT_COLS) // LANES


def _dot(a, b):
    return jnp.dot(a, b, preferred_element_type=f32)


def _dot_nt(a, b):
    return lax.dot_general(a, b, (((1,), (1,)), ((), ())), preferred_element_type=f32)


def _dot_tn(a, b):
    return lax.dot_general(a, b, (((0,), (0,)), ((), ())), preferred_element_type=f32)


def _split2(x):
    hi = x.astype(bf16)
    lo = (x - hi.astype(f32)).astype(bf16)
    return hi, lo


def _split3(x):
    hi = x.astype(bf16)
    r1 = x - hi.astype(f32)
    mid = r1.astype(bf16)
    lo = (r1 - mid.astype(f32)).astype(bf16)
    return hi, mid, lo


def _silu(x):
    return x * jax.nn.sigmoid(x)


def _rms(x, w):
    ms = jnp.mean(x * x, axis=-1, keepdims=True)
    return x * lax.rsqrt(ms + NORM_EPS) * w


def _pair_consts(n):
    lane = lax.broadcasted_iota(jnp.int32, (n, LANES), 1)
    return lane < HEAD


def _ones_blockdiag():
    r = lax.broadcasted_iota(jnp.int32, (LANES, LANES), 0)
    c = lax.broadcasted_iota(jnp.int32, (LANES, LANES), 1)
    return jnp.where((r < HEAD) == (c < HEAD), 1.0, 0.0).astype(bf16)


def _segsum(x, ones_bd):
    hi, lo = _split2(x)
    return _dot(hi, ones_bd) + _dot(lo, ones_bd)


def _inproj_body(xp_ref, xs_ref, nw_ref, w_ref, zp_ref, zs_ref, h_scr, *, tn, n):
    i = pl.program_id(0)
    j = pl.program_id(1)

    @pl.when(j == 0)
    def _():
        h_scr[...] = _rms(xp_ref[...], nw_ref[...]).astype(bf16)

    w = w_ref[...].astype(bf16)
    zp_ref[...] = _dot(h_scr[...], w)

    @pl.when(i == 0)
    def _():
        hs = _rms(xs_ref[...], nw_ref[...]).astype(bf16)
        col = pl.multiple_of(j * tn, LANES)
        zs = _dot(hs, w)
        cols = col + lax.broadcasted_iota(jnp.int32, zs.shape, 1)
        zs_ref[:, pl.ds(col, tn)] = jnp.where(cols < n, zs, 0.0)


def _inproj(xp, xs, nw, w_all, layer, *, tm=2048, tn=512):
    m, d = xp.shape
    n = w_all.shape[2]
    nb = xs.shape[0]
    nj = pl.cdiv(n, tn)
    return pl.pallas_call(
        functools.partial(_inproj_body, tn=tn, n=n),
        grid=(m // tm, nj),
        in_specs=[
            pl.BlockSpec((tm, d), lambda i, j: (i, 0), pipeline_mode=pl.Buffered(1)),
            pl.BlockSpec((nb, d), lambda i, j: (0, 0)),
            pl.BlockSpec((None, 1, d), lambda i, j: (layer, 0, 0)),
            pl.BlockSpec((None, d, tn), lambda i, j: (layer, 0, j)),
        ],
        out_specs=[
            pl.BlockSpec((tm, tn), lambda i, j: (i, j)),
            pl.BlockSpec((nb, nj * tn), lambda i, j: (0, 0)),
        ],
        out_shape=[jax.ShapeDtypeStruct((m, n), f32), jax.ShapeDtypeStruct((nb, nj * tn), f32)],
        scratch_shapes=[pltpu.VMEM((tm, d), bf16)],
        compiler_params=pltpu.CompilerParams(
            dimension_semantics=("arbitrary", "arbitrary"), vmem_limit_bytes=VMEM_LIMIT),
        name="inproj",
    )(xp, xs, nw, w_all)


def _outproj_body(*refs, tn, widths):
    n = len(widths)
    xp_ref = refs[0]
    y_refs = refs[1:1 + n]
    xs_ref, ms_ref, w_ref, op_ref, os_ref = refs[1 + n:]
    i = pl.program_id(0)
    j = pl.program_id(1)
    w = w_ref[...].astype(bf16)
    acc = None
    row = 0
    for y_ref, width in zip(y_refs, widths):
        part = _dot(y_ref[...], w[row:row + width])
        acc = part if acc is None else acc + part
        row += width
    op_ref[...] = xp_ref[...] + acc

    @pl.when(i == 0)
    def _():
        col = pl.multiple_of(j * tn, LANES)
        os_ref[:, pl.ds(col, tn)] = xs_ref[:, pl.ds(col, tn)] + _dot(ms_ref[...].astype(bf16), w)


def _outproj(xp, ys, xs, ms, w_all, layer, *, tm=2048, tn=512):
    m, d = xp.shape
    nb = xs.shape[0]
    dm = w_all.shape[1]
    widths = tuple(y.shape[1] for y in ys)
    assert sum(widths) == dm
    return pl.pallas_call(
        functools.partial(_outproj_body, tn=tn, widths=widths),
        grid=(m // tm, d // tn),
        in_specs=[pl.BlockSpec((tm, tn), lambda i, j: (i, j))]
        + [pl.BlockSpec((tm, width), lambda i, j: (i, 0)) for width in widths]
        + [
            pl.BlockSpec((nb, d), lambda i, j: (0, 0)),
            pl.BlockSpec((nb, dm), lambda i, j: (0, 0)),
            pl.BlockSpec((None, dm, tn), lambda i, j: (layer, 0, j)),
        ],
        out_specs=[
            pl.BlockSpec((tm, tn), lambda i, j: (i, j)),
            pl.BlockSpec((nb, d), lambda i, j: (0, 0)),
        ],
        out_shape=[jax.ShapeDtypeStruct((m, d), f32), jax.ShapeDtypeStruct((nb, d), f32)],
        compiler_params=pltpu.CompilerParams(
            dimension_semantics=("arbitrary", "arbitrary"), vmem_limit_bytes=VMEM_LIMIT),
        name="outproj",
    )(xp, *ys, xs, ms, w_all)


_DECAY_SCALE = 0.6065306597126334

_P_MU, _P_W0, _P_A0, _P_KK, _P_KA, _P_RK, _P_LNW, _P_LNB = 0, 5, 6, 7, 8, 9, 10, 11


def _rwkv_chunk_stages(at, bt, kt, rt, v, wcs, consts):
    m0, gmask, bdmask, eye = consts
    m0w = jnp.concatenate([m0, m0], axis=1)
    eye_pair = jnp.where(lax.broadcasted_iota(jnp.int32, (CHUNK, LANES), 1) % CHUNK
                         == lax.broadcasted_iota(jnp.int32, (CHUNK, LANES), 0), 1.0, 0.0)
    zb64 = jnp.zeros((CHUNK, LANES), bf16)
    n = len(wcs)

    def rows(x, c):
        return x[c * CHUNK:(c + 1) * CHUNK]

    def split_heads(xb, m):
        zero = jnp.zeros_like(xb)
        return jnp.concatenate([jnp.where(m, xb, zero), jnp.where(m, zero, xb)], axis=0)

    g, p, x, mc, nc, rp, ov = {}, {}, {}, {}, {}, {}, {}

    def stage_scores(chs):
        for c in chs:
            lhs = jnp.concatenate([rows(at, c), rows(rt, c)], axis=0).astype(bf16)
            rhs = jnp.concatenate([split_heads(rows(bt, c).astype(bf16), m0),
                                   split_heads(rows(kt, c).astype(bf16), m0)], axis=0)
            g[c] = jnp.where(gmask, _dot_nt(lhs, rhs), 0.0)
            p[c] = g[c][0:CHUNK, 0:LANES]

    av, tinv = {}, {}

    def stage_level(chs, lev):
        for c in chs:
            pb = p[c].astype(bf16)
            if lev == 0:
                vs = split_heads(rows(v, c).astype(bf16), m0)
                av[c] = _dot(g[c][0:CHUNK, LANES:].astype(bf16), vs)
                tinv[c] = eye_pair + p[c]
                p[c] = _dot(pb, split_heads(pb, m0))
            elif lev < 5:
                tt_ = split_heads(tinv[c].astype(bf16), m0)
                out = _dot(pb, jnp.concatenate([split_heads(pb, m0), tt_], axis=1))
                p[c] = out[:, 0:LANES]
                tinv[c] = tinv[c] + out[:, LANES:]
            else:
                tinv[c] = tinv[c] + _dot(pb, split_heads(tinv[c].astype(bf16), m0))

    def stage_apply(chs):
        for c in chs:
            rhs = split_heads(jnp.concatenate([av[c], rows(at, c)], axis=1).astype(bf16), m0w)
            x[c] = _dot(tinv[c].astype(bf16), rhs)

    def stage_fold(chs):
        for c in chs:
            xb = x[c].astype(bf16)
            uv = xb[:, 0:LANES]
            ap = xb[:, LANES:]
            vb = rows(v, c).astype(bf16)
            tnl = jnp.concatenate([rows(bt, c) * wcs[c], rows(kt, c) * wcs[c]], axis=0).astype(bf16)
            tnr = jnp.concatenate(
                [jnp.concatenate([ap, uv], axis=1), jnp.concatenate([zb64, vb], axis=1)], axis=0)
            mn = jnp.where(bdmask, _dot_tn(tnl, tnr), 0.0)
            mc[c] = mn[:, 0:LANES] + jnp.where(eye, jnp.broadcast_to(wcs[c], (LANES, LANES)), 0.0)
            nc[c] = mn[:, LANES:]
            l2 = g[c][CHUNK:].astype(bf16)
            r2 = jnp.concatenate([
                split_heads(jnp.concatenate([ap, uv], axis=1), m0w),
                split_heads(jnp.concatenate([zb64, vb], axis=1), m0w)], axis=0)
            ro = _dot(l2, r2)
            rp[c] = rows(rt, c) + ro[:, 0:LANES]
            ov[c] = ro[:, LANES:]

    def carry(c, st):
        seq = _dot(jnp.concatenate([rp[c], mc[c]], axis=0).astype(bf16), st.astype(bf16))
        return seq[0:CHUNK] + ov[c], seq[CHUNK:] + nc[c]

    chs = list(range(n))
    stages = ([functools.partial(stage_scores, chs)]
              + [functools.partial(stage_level, chs, lev) for lev in range(6)]
              + [functools.partial(stage_apply, chs), functools.partial(stage_fold, chs)])
    return stages, carry


def _rwkv_body(zr_ref, zk_ref, zv_ref, zg_ref, zl_ref, pp_ref, wl_ref, y_ref, s_ref,
               sh_scr, st_scr, *, tt, npr, lock):
    tb = pl.program_id(2)
    nt = pl.num_programs(2)
    nch = tt // CHUNK
    nsh = 4 * npr + 1

    @pl.when(tb == 0)
    def _():
        st_scr[...] = jnp.zeros_like(st_scr)
        for i in range(nsh):
            sh_scr[i, 7:8, :] = jnp.zeros((1, LANES), f32)

    @pl.when(tb > 0)
    def _():
        for i in range(nsh):
            sh_scr[i, 7:8, :] = sh_scr[i, 7 + tt:8 + tt, :]

    def shifted(idx, z, mu):
        sh_scr[idx, 8:8 + tt, :] = z
        zprev = sh_scr[idx, pl.ds(7, tt), :]
        return z + (zprev - z) * mu

    m0t = _pair_consts(tt)
    ones_bd = _ones_blockdiag()
    lo = shifted(4 * npr, zl_ref[0], pp_ref[0, _P_MU + 4:_P_MU + 5])
    xl = jnp.where(m0t, jnp.tanh(lo), lo).astype(bf16)

    m0 = _pair_consts(CHUNK)
    gr = lax.broadcasted_iota(jnp.int32, (2 * CHUNK, 2 * LANES), 0)
    gc = lax.broadcasted_iota(jnp.int32, (2 * CHUNK, 2 * LANES), 1)
    gt = gr % CHUNK
    gs = gc % CHUNK
    gmask = (gt > gs) | ((gr >= CHUNK) & (gt == gs))
    br = lax.broadcasted_iota(jnp.int32, (LANES, 2 * LANES), 0)
    bc = lax.broadcasted_iota(jnp.int32, (LANES, 2 * LANES), 1)
    bdmask = (br < HEAD) == ((bc % LANES) < HEAD)
    er = lax.broadcasted_iota(jnp.int32, (LANES, LANES), 0)
    ec = lax.broadcasted_iota(jnp.int32, (LANES, LANES), 1)
    eye = er == ec
    consts = (m0, gmask, bdmask, eye)
    tr = lax.broadcasted_iota(jnp.int32, (CHUNK, CHUNK), 0)
    tc = lax.broadcasted_iota(jnp.int32, (CHUNK, CHUNK), 1)
    tril = jnp.where(tc <= tr, 1.0, 0.0).astype(bf16)

    def prepare(s, out):
        pp = pp_ref[s]
        ls = slice(s * LANES, (s + 1) * LANES)
        r, k, v, g = [shifted(4 * s + i, ref[0, :, ls], pp[_P_MU + i:_P_MU + i + 1])
                      for i, ref in enumerate((zr_ref, zk_ref, zv_ref, zg_ref))]
        la = _dot(xl, wl_ref[s].astype(bf16))
        kk = k * pp[_P_KK:_P_KK + 1]
        kk_ss = _segsum(kk * kk, ones_bd)
        yield
        ld = -_DECAY_SCALE * jax.nn.sigmoid(pp[_P_W0:_P_W0 + 1] + la[:, 0:LANES])
        a = jax.nn.sigmoid(pp[_P_A0:_P_A0 + 1] + la[:, LANES:])
        kk = kk * jnp.minimum(lax.rsqrt(kk_ss), 1e12)
        k2 = k * (1.0 + (a - 1.0) * pp[_P_KA:_P_KA + 1])
        h3 = _split3(ld)
        cl = jnp.concatenate([
            sum(_dot(tril, h[c * CHUNK:(c + 1) * CHUNK]) for h in h3) for c in range(nch)],
            axis=0)
        bonus = _segsum(r * k2 * pp[_P_RK:_P_RK + 1], ones_bd) * v
        yield
        e_in = jnp.exp(cl)
        e_neg = jnp.exp(-cl)
        wcs = [e_in[(c + 1) * CHUNK - 1:(c + 1) * CHUNK] for c in range(nch)]
        out["stages"], out["carry"] = _rwkv_chunk_stages(
            -kk * jnp.exp(cl - ld), kk * a * e_neg, k2 * e_neg, r * e_in, v, wcs, consts)
        out["epi"] = (pp, g, bonus)

    def finish(s, outs, st, epi):
        pp, g, bonus = epi
        o = jnp.concatenate(outs, axis=0)
        mean = _segsum(o, ones_bd) * (1.0 / HEAD)
        dl = o - mean
        var = _segsum(dl * dl, ones_bd) * (1.0 / HEAD)
        on = dl * lax.rsqrt(var + GN_EPS) * pp[_P_LNW:_P_LNW + 1] + pp[_P_LNB:_P_LNB + 1] + bonus
        y_ref[0, :, s * LANES:(s + 1) * LANES] = (on * _silu(g)).astype(bf16)
        st_scr[s] = st

    def drain(gen):
        for _ in gen:
            pass

    def lockstep(gens):
        gens = list(gens)
        while gens:
            gens = [g for g in gens if next(g, StopIteration) is not StopIteration]
            yield

    preps = [dict() for _ in range(npr)]
    units = [list(range(i, min(i + lock, npr))) for i in range(0, npr, lock)]
    drain(lockstep(prepare(s, preps[s]) for s in units[0]))
    prev = None
    for ui, unit in enumerate(units):
        nxt = (lockstep(prepare(s, preps[s]) for s in units[ui + 1]) if ui + 1 < len(units) else iter(()))
        for stage_row in zip(*[preps[s]["stages"] for s in unit]):
            for stage in stage_row:
                stage()
            if prev is not None:
                for cur in prev:
                    if cur["todo"]:
                        cur["step"]()
            next(nxt, None)
        drain(nxt)
        if prev is not None:
            while any(cur["todo"] for cur in prev):
                for cur in prev:
                    if cur["todo"]:
                        cur["step"]()
            for cur in prev:
                finish(cur["s"], cur["outs"], cur["st"][0], cur["epi"])
        prev = []
        for s in unit:
            cur = {"s": s, "outs": [], "st": [st_scr[s]], "epi": preps[s]["epi"], "todo": list(range(nch))}

            def step(cur=cur, carry=preps[s]["carry"]):
                o_c, cur["st"][0] = carry(cur["todo"].pop(0), cur["st"][0])
                cur["outs"].append(o_c)

            cur["step"] = step
            prev.append(cur)
    while any(cur["todo"] for cur in prev):
        for cur in prev:
            if cur["todo"]:
                cur["step"]()
    for cur in prev:
        finish(cur["s"], cur["outs"], cur["st"][0], cur["epi"])

    @pl.when(tb == nt - 1)
    def _():
        for s in range(npr):
            stt = st_scr[s].T
            s_ref[0, 2 * s] = stt[0:HEAD, 0:HEAD]
            s_ref[0, 2 * s + 1] = stt[HEAD:, HEAD:]


def _rwkv_prompt(z3, pp, wl, layer, *, tt=512, npr=6, lock=2):
    b, t, _ = z3.shape
    npair = RWKV_HEADS // 2
    ngrp = npair // npr
    wide = npr * LANES

    def zspec(part):
        return pl.BlockSpec((1, tt, wide), lambda bi, p, tb, part=part: (bi, tb, part * ngrp + p))

    lora_blk = 4 * RWKV_W // LANES
    return pl.pallas_call(
        functools.partial(_rwkv_body, tt=tt, npr=npr, lock=lock),
        grid=(b, ngrp, t // tt),
        in_specs=[
            zspec(0), zspec(1), zspec(2), zspec(3),
            pl.BlockSpec((1, tt, LANES), lambda bi, p, tb: (bi, tb, lora_blk)),
            pl.BlockSpec((None, npr, 12, LANES), lambda bi, p, tb: (layer, p, 0, 0)),
            pl.BlockSpec((None, npr, LANES, 2 * LANES), lambda bi, p, tb: (layer, p, 0, 0)),
        ],
        out_specs=[
            pl.BlockSpec((1, tt, wide), lambda bi, p, tb: (bi, tb, p)),
            pl.BlockSpec((1, 2 * npr, HEAD, HEAD), lambda bi, p, tb: (bi, p, 0, 0)),
        ],
        out_shape=[jax.ShapeDtypeStruct((b, t, RWKV_W), bf16),
                   jax.ShapeDtypeStruct((b, RWKV_HEADS, HEAD, HEAD), f32)],
        scratch_shapes=[pltpu.VMEM((4 * npr + 1, tt + 8, LANES), f32), pltpu.VMEM((npr, LANES, LANES), f32)],
        compiler_params=pltpu.CompilerParams(
            dimension_semantics=("arbitrary", "arbitrary", "arbitrary"), vmem_limit_bytes=VMEM_LIMIT),
        name="rwkv_prompt",
    )(z3, z3, z3, z3, z3, pp, wl)


def _rope_pair(x, cos, sa, sb):
    return x * cos + pltpu.roll(x, LANES - ROPE_DIMS // 2, 1) * sa + pltpu.roll(x, ROPE_DIMS // 2, 1) * sb


def _attn_body(*refs, t):
    qkvg = [refs[4 * gi:4 * gi + 4] for gi in range(3)]
    cos_ref, sin_ref, qnw_ref, knw_ref = refs[12:16]
    y_refs = refs[16:19]
    kv_refs = [refs[19 + 2 * gi:21 + 2 * gi] for gi in range(3)]
    qn_scr, kn_scr, o_scr, lse_scr = refs[25:29]

    rb = 256
    ones_bd = _ones_blockdiag()
    m0q = _pair_consts(QBLK)
    scale = HEAD ** -0.5
    half = ROPE_DIMS // 2
    pj = lax.broadcasted_iota(jnp.int32, (LANES, LANES), 0)
    pi = lax.broadcasted_iota(jnp.int32, (LANES, LANES), 1)
    pin = pi % HEAD
    perm = jnp.where(((pin < half) & (pj == pi + half)) | ((pin >= half) & (pin < ROPE_DIMS) & (pj == pi - half)),
                     1.0, 0.0).astype(bf16)
    zpad = jnp.zeros((LANES, LANES), bf16)
    sum_swap = jnp.concatenate([jnp.concatenate([ones_bd, zpad], axis=1),
                                jnp.concatenate([zpad, perm], axis=1)], axis=0)

    for gi, (window, dil) in enumerate(ATT_GROUPS):
        q_ref, k_ref, v_ref, g_ref = qkvg[gi]
        pk_ref, pv_ref = kv_refs[gi]
        keep = min(window, t)

        def norm_rows(i, carry, q_ref=q_ref, k_ref=k_ref):
            jobs = []
            for u in range(2):
                rows = pl.ds(pl.multiple_of((2 * i + u) * rb, rb), rb)
                for src, nw, dst in ((q_ref, qnw_ref, qn_scr), (k_ref, knw_ref, kn_scr)):
                    x = src[0, rows, :]
                    y = x * nw[...]
                    hi, lo = _split2(jnp.concatenate([x * x, y], axis=1))
                    jobs.append((y, _dot(hi, sum_swap) + _dot(lo, sum_swap), dst, rows))
            for y, res, dst, rows in jobs:
                rs = lax.rsqrt(res[:, 0:LANES] * (1.0 / HEAD) + NORM_EPS)
                dst[rows, :] = (y * cos_ref[rows, :] + res[:, LANES:] * sin_ref[rows, :]) * rs
            return carry

        lax.fori_loop(0, t // (2 * rb), norm_rows, 0)
        pk_ref[0] = kn_scr[t - keep:t, :]
        pv_ref[0] = v_ref[0, t - keep:t, :]

        n_sub = t // dil
        n_blk = n_sub // QBLK

        def tiles(starts, nk, first, v_ref=v_ref, dil=dil, gi=gi):
            qi = lax.broadcasted_iota(jnp.int32, (QBLK, nk), 0)
            kj = lax.broadcasted_iota(jnp.int32, (QBLK, nk), 1)
            mask = (kj <= qi) if first else ((kj >= qi) & (kj <= qi + QBLK))
            scores, vts = [], []
            for q0, k0 in starts:
                qt = qn_scr[pl.ds(q0, QBLK, stride=dil), :] * scale
                kt = kn_scr[pl.ds(k0, nk, stride=dil), :].astype(bf16)
                vts.append(v_ref[0, pl.ds(k0, nk, stride=dil), :].astype(bf16))
                for hh in range(2):
                    qh = jnp.where(m0q, qt, 0.0) if hh == 0 else jnp.where(m0q, 0.0, qt)
                    scores.append(_dot_nt(qh.astype(bf16), kt))
            probs, sums, lses = [], [], []
            for s in scores:
                s = jnp.where(mask, s, -jnp.inf)
                m = jnp.max(s, axis=-1, keepdims=True)
                p = jnp.exp(s - m)
                l = jnp.sum(p, axis=-1, keepdims=True)
                probs.append(p.astype(bf16))
                sums.append(l)
                lses.append(m + jnp.log(l))
            for ti, (q0, _) in enumerate(starts):
                oh = [_dot(probs[2 * ti + hh], vts[ti]) / sums[2 * ti + hh] for hh in range(2)]
                o_scr[gi, pl.ds(q0, QBLK, stride=dil), :] = jnp.where(m0q, oh[0], oh[1])
                lse_scr[gi, pl.ds(q0, QBLK, stride=dil), :] = jnp.where(
                    m0q, jnp.broadcast_to(lses[2 * ti], (QBLK, LANES)),
                    jnp.broadcast_to(lses[2 * ti + 1], (QBLK, LANES)))

        per_first = min(8, dil)

        def first_tiles(i, carry, tiles=tiles, per=per_first):
            tiles([(i * per + u, i * per + u) for u in range(per)], QBLK, True)
            return carry

        lax.fori_loop(0, dil // per_first, first_tiles, 0)

        if n_blk > 1:
            n_later = dil * (n_blk - 1)
            per_later = max(p for p in range(1, 7) if n_later % p == 0)

            def later_tiles(i, carry, tiles=tiles, dil=dil, per=per_later):
                starts = []
                for u in range(per):
                    idx = i * per + u
                    r = idx % dil
                    blk = idx // dil + 1
                    starts.append((r + blk * QBLK * dil, r + (blk - 1) * QBLK * dil))
                tiles(starts, 2 * QBLK, False)
                return carry

            lax.fori_loop(0, dil * (n_blk - 1) // per_later, later_tiles, 0)

    def combine(i, carry):
        rows = pl.ds(pl.multiple_of(i * rb, rb), rb)
        ls = [lse_scr[gi, rows, :] for gi in range(3)]
        mx = jnp.maximum(jnp.maximum(ls[0], ls[1]), ls[2])
        es = [jnp.exp(l - mx) for l in ls]
        inv = 1.0 / (es[0] + es[1] + es[2])
        for gi in range(3):
            gate = qkvg[gi][3][0, rows, :]
            y_refs[gi][0, rows, :] = (o_scr[gi, rows, :] * (es[gi] * inv) * _silu(gate)).astype(bf16)
        return carry

    lax.fori_loop(0, t // rb, combine, 0)


def _attn_prompt(z3, cos, sin, qnw, knw, layer):
    b, t, _ = z3.shape
    in_specs = []
    for gi in range(3):
        for part in range(4):
            off = ATT_BLK0 + part * 6 + 2 * gi
            in_specs.append(pl.BlockSpec((1, t, LANES), lambda bi, jp, off=off: (bi, 0, off + jp)))
    in_specs += [pl.BlockSpec((t, LANES), lambda bi, jp: (0, 0))] * 2
    in_specs += [pl.BlockSpec((None, 1, LANES), lambda bi, jp: (layer, 0, 0))] * 2
    out_specs = [pl.BlockSpec((1, t, LANES), lambda bi, jp: (bi, 0, jp))] * 3
    out_shape = [jax.ShapeDtypeStruct((b, t, 4 * HEAD), bf16)] * 3
    for window, _ in ATT_GROUPS:
        keep = min(window, t)
        out_specs += [pl.BlockSpec((1, keep, LANES), lambda bi, jp: (bi, 0, jp))] * 2
        out_shape += [jax.ShapeDtypeStruct((b, keep, 4 * HEAD), f32)] * 2
    outs = pl.pallas_call(
        functools.partial(_attn_body, t=t),
        grid=(b, 2),
        in_specs=in_specs,
        out_specs=out_specs,
        out_shape=out_shape,
        scratch_shapes=[pltpu.VMEM((t, LANES), f32), pltpu.VMEM((t, LANES), f32),
                        pltpu.VMEM((3, t, LANES), f32), pltpu.VMEM((3, t, LANES), f32)],
        compiler_params=pltpu.CompilerParams(
            dimension_semantics=("arbitrary", "arbitrary"), vmem_limit_bytes=VMEM_LIMIT),
        name="attn_prompt",
    )(*([z3] * 12), cos, sin, qnw, knw)
    return outs


def _pool_body(*refs, tt):
    ng = len(POOL_WINDOWS)
    u_refs, g_refs = refs[0:ng], refs[ng:2 * ng]
    w_ref, sc_ref, y_ref, scr = refs[2 * ng:]
    tb = pl.program_id(1)
    pad = max(POOL_WINDOWS)

    @pl.when(tb == 0)
    def _():
        for gi in range(ng):
            scr[gi, 0:pad, :] = jnp.zeros((pad, LANES), f32)

    @pl.when(tb > 0)
    def _():
        for gi in range(ng):
            scr[gi, 0:pad, :] = scr[gi, tt:tt + pad, :]

    pos = (tb * tt + lax.broadcasted_iota(jnp.int32, (tt, 1), 0)).astype(f32)
    for gi, w in enumerate(POOL_WINDOWS):
        u = u_refs[gi][0]
        scr[gi, pad:pad + tt, :] = u
        acc = u
        for k in range(1, w):
            acc = acc + scr[gi, pl.ds(pad - k, tt), :]
        d = acc / jnp.minimum(float(w), pos + 1.0) - u
        dd = _dot(d.astype(bf16), w_ref[gi].astype(bf16)) * sc_ref[:, gi * LANES:(gi + 1) * LANES]
        y_ref[0, :, gi * LANES:(gi + 1) * LANES] = (dd * _silu(g_refs[gi][0])).astype(bf16)


def _pool_prompt(z3, pw, psc, layer, *, tt=512):
    b, t, _ = z3.shape
    ng = len(POOL_WINDOWS)
    pad = max(POOL_WINDOWS)
    return pl.pallas_call(
        functools.partial(_pool_body, tt=tt),
        grid=(b, t // tt),
        in_specs=[pl.BlockSpec((1, tt, LANES), lambda bi, tb, off=POOL_BLK0 + i: (bi, tb, off))
                  for i in range(2 * ng)]
        + [
            pl.BlockSpec((None, ng, LANES, LANES), lambda bi, tb: (layer, 0, 0, 0)),
            pl.BlockSpec((None, 1, POOL_W), lambda bi, tb: (layer, 0, 0)),
        ],
        out_specs=pl.BlockSpec((1, tt, POOL_W), lambda bi, tb: (bi, tb, 0)),
        out_shape=jax.ShapeDtypeStruct((b, t, POOL_W), bf16),
        scratch_shapes=[pltpu.VMEM((ng, tt + pad, LANES), f32)],
        compiler_params=pltpu.CompilerParams(
            dimension_semantics=("arbitrary", "arbitrary"), vmem_limit_bytes=VMEM_LIMIT),
        name="pool_prompt",
    )(*([z3] * (2 * ng)), pw, psc)


def _bcast8(x):
    return jnp.broadcast_to(x, (8, x.shape[-1]))


def _sample_body(z_ref, sh_ref, wkv_ref, pool_ref, ck0, cv0, ck1, cv1, ck2, cv2,
                 mu_ref, rp_ref, wup_ref, aup_ref, qnw_ref, knw_ref, cos_ref, sa_ref, sb_ref,
                 pw_ref, psc_ref,
                 mix_ref, swkv_ref, spool_ref, sk0, sv0, sk1, sv1, sk2, sv2,
                 qk_scr, *, past_len):
    z = z_ref[0]
    za = z[:, 0:RWKV_COLS]
    zb = z[:, RWKV_COLS:RWKV_COLS + ATT_COLS]
    zc = z[:, RWKV_COLS + ATT_COLS:RWKV_COLS + ATT_COLS + 2 * POOL_W]
    er = lax.broadcasted_iota(jnp.int32, (HEAD, HEAD), 0)
    ec = lax.broadcasted_iota(jnp.int32, (HEAD, HEAD), 1)
    eye = er == ec

    zs = za + (sh_ref[0] - za) * mu_ref[...]
    w_ = RWKV_W
    r, k, v, g = (zs[:, i * w_:(i + 1) * w_] for i in range(4))
    w_dn = zs[:, 4 * w_:4 * w_ + LORA]
    a_dn = zs[:, 4 * w_ + LORA:]
    rp = rp_ref[...]
    w0, a0, k_k, k_a, r_k, ln_w, ln_b = (rp[i:i + 1] for i in range(7))
    lw = _dot(_bcast8(jnp.tanh(w_dn)).astype(bf16), wup_ref[...].astype(bf16))[0:1]
    la = _dot(_bcast8(a_dn).astype(bf16), aup_ref[...].astype(bf16))[0:1]
    decay = jnp.exp(-jnp.exp(-jax.nn.softplus(-(w0 + lw)) - 0.5))
    a = jax.nn.sigmoid(a0 + la)
    kk = k * k_k
    k2 = k * (1.0 + (a - 1.0) * k_a)
    heads = range(RWKV_HEADS)
    hsl = [slice(h * HEAD, (h + 1) * HEAD) for h in heads]

    def lane_sum(x):
        return jnp.sum(x, axis=-1, keepdims=True)

    kk_ss = [lane_sum(kk[:, hs] * kk[:, hs]) for hs in hsl]
    bonus = [lane_sum(r[:, hs] * k2[:, hs] * r_k[:, hs]) for hs in hsl]
    v_col = [lane_sum(jnp.where(eye, jnp.broadcast_to(v[:, hs], (HEAD, HEAD)), 0.0)) for hs in hsl]
    kkn = [kk[:, hs] / jnp.maximum(jnp.sqrt(ss), 1e-12) for hs, ss in zip(hsl, kk_ss)]
    sa_col = [lane_sum(wkv_ref[0, h] * (-kkn[h])) for h in heads]
    sn = [wkv_ref[0, h] * decay[:, hsl[h]] + sa_col[h] * (kkn[h] * a[:, hsl[h]]) + v_col[h] * k2[:, hsl[h]]
          for h in heads]
    for h in heads:
        swkv_ref[0, h] = sn[h]
    o_col = [lane_sum(sn[h] * r[:, hsl[h]]) for h in heads]
    o = [jnp.sum(jnp.where(eye, jnp.broadcast_to(oc, (HEAD, HEAD)), 0.0), axis=0, keepdims=True) for oc in o_col]
    mean = [jnp.mean(x, axis=-1, keepdims=True) for x in o]
    dl = [x - mu_ for x, mu_ in zip(o, mean)]
    var = [jnp.mean(x * x, axis=-1, keepdims=True) for x in dl]
    for h in heads:
        hs = hsl[h]
        on = dl[h] * lax.rsqrt(var[h] + GN_EPS) * ln_w[:, hs] + ln_b[:, hs] + bonus[h] * v[:, hs]
        mix_ref[0, :, hs] = on * _silu(g[:, hs])

    aw = ATT_W
    q, kx, vx, gx = (zb[:, i * aw:(i + 1) * aw] for i in range(4))
    m0 = _pair_consts(8)
    for hp in range(ATT_W // LANES):
        ls = slice(hp * LANES, (hp + 1) * LANES)
        for idx, (src, nw) in enumerate(((q, qnw_ref), (kx, knw_ref))):
            x = _bcast8(src[:, ls])
            sq = x * x
            s0 = jnp.sum(jnp.where(m0, sq, 0.0), axis=-1, keepdims=True)
            s1 = jnp.sum(jnp.where(m0, 0.0, sq), axis=-1, keepdims=True)
            ms = jnp.where(m0, s0, s1) * (1.0 / HEAD)
            xn = x * lax.rsqrt(ms + NORM_EPS) * nw[...]
            qk_scr[idx, :, ls] = _rope_pair(xn, cos_ref[...], sa_ref[...], sb_ref[...])
    qn = qk_scr[0, 0:1, :]
    kn = qk_scr[1, 0:1, :]
    scale = HEAD ** -0.5

    def heads4(row, gi):
        return jnp.concatenate([row[:, (gi * 4 + j) * HEAD:(gi * 4 + j + 1) * HEAD] for j in range(4)], axis=0)

    def to_col(row):
        return jnp.sum(jnp.where(eye, jnp.broadcast_to(row, (HEAD, HEAD)), 0.0), axis=-1, keepdims=True)

    def to_row(col):
        return jnp.sum(jnp.where(eye, jnp.broadcast_to(col, (HEAD, HEAD)), 0.0), axis=0, keepdims=True)

    caches = ((ck0, cv0, sk0, sv0), (ck1, cv1, sk1, sv1), (ck2, cv2, sk2, sv2))
    windows = []
    for gi, (ck, cv, sk, sv) in enumerate(caches):
        sk[0, 0] = heads4(kn, gi)
        sv[0, 0] = heads4(vx, gi)
        row_id = lax.broadcasted_iota(jnp.int32, (1, ck.shape[-1]), 1)
        windows.append((row_id % ATT_GROUPS[gi][1]) == 0)
    combos = [(gi, j) for gi in range(3) for j in range(4)]
    asl = [slice((gi * 4 + j) * HEAD, (gi * 4 + j + 1) * HEAD) for gi, j in combos]
    qh = [qn[:, hs] * scale for hs in asl]
    q_col = [to_col(x) for x in qh]
    s_new = [jnp.sum(kn[:, hs] * x, axis=-1, keepdims=True) for hs, x in zip(asl, qh)]
    s_all = [jnp.where(windows[gi], jnp.sum(caches[gi][0][0, j] * qc, axis=0, keepdims=True), -jnp.inf)
             for (gi, j), qc in zip(combos, q_col)]
    m_all = [jnp.maximum(jnp.max(s, axis=-1, keepdims=True), sn_) for s, sn_ in zip(s_all, s_new)]
    p_all = [jnp.exp(s - m) for s, m in zip(s_all, m_all)]
    p_new = [jnp.exp(sn_ - m) for sn_, m in zip(s_new, m_all)]
    l_all = [jnp.sum(p, axis=-1, keepdims=True) + pn for p, pn in zip(p_all, p_new)]
    o_colv = [jnp.sum(caches[gi][1][0, j] * p, axis=-1, keepdims=True) for (gi, j), p in zip(combos, p_all)]
    o_rows = [(to_row(oc) + pn * vx[:, hs]) / l for oc, pn, hs, l in zip(o_colv, p_new, asl, l_all)]
    lse = [m + jnp.log(l) for m, l in zip(m_all, l_all)]
    for j in range(4):
        mx = jnp.maximum(jnp.maximum(lse[j], lse[4 + j]), lse[8 + j])
        es = [jnp.exp(lse[gi * 4 + j] - mx) for gi in range(3)]
        inv = 1.0 / (es[0] + es[1] + es[2])
        for gi in range(3):
            ci = gi * 4 + j
            mix_ref[0, :, RWKV_W + ci * HEAD:RWKV_W + (ci + 1) * HEAD] = (
                o_rows[ci] * (es[gi] * inv) * _silu(gx[:, asl[ci]]))

    u = zc[:, 0:POOL_W]
    gate = zc[:, POOL_W:]
    prev = pool_ref[0]
    for gi, w in enumerate(POOL_WINDOWS):
        cs = slice(gi * LANES, (gi + 1) * LANES)
        ug = u[:, cs]
        sw = jnp.sum(prev[POOL_BUF - (w - 1):POOL_BUF, cs], axis=0, keepdims=True) + ug
        cnt = min(float(w), float(past_len) + 1.0)
        d = sw / cnt - ug
        dd = _dot(_bcast8(d).astype(bf16), pw_ref[gi].astype(bf16))[0:1] * psc_ref[:, cs]
        mix_ref[0, :, RWKV_W + ATT_W + gi * LANES:RWKV_W + ATT_W + (gi + 1) * LANES] = dd * _silu(gate[:, cs])
    spool_ref[0, 0:POOL_BUF - 1, :] = prev[1:POOL_BUF]
    spool_ref[0, POOL_BUF - 1:POOL_BUF, :] = u


def _sample_step(zs, sh_all, wkv_all, pool_all, caches_all, layer, mu, rp, wup, aup, qnw, knw, cos, sa, sb,
                 pw, psc, *, past_len):
    nb = zs.shape[0]
    d_in = zs.shape[-1]
    d_mix = RWKV_W + ATT_W + POOL_W

    def full(shape):
        nd = len(shape)
        return pl.BlockSpec(shape, lambda b, nd=nd: (0,) * nd)

    in_specs = [
        pl.BlockSpec((1, 1, d_in), lambda b: (b, 0, 0)),
        pl.BlockSpec((None, 1, 1, RWKV_COLS), lambda b: (layer, b, 0, 0)),
        pl.BlockSpec((None, 1, RWKV_HEADS, HEAD, HEAD), lambda b: (layer, b, 0, 0, 0)),
        pl.BlockSpec((None, 1, POOL_BUF, POOL_W), lambda b: (layer, b, 0, 0)),
    ]
    in_specs += [pl.BlockSpec((None, 1, 4, HEAD, c.shape[-1]), lambda b: (layer, b, 0, 0, 0)) for c in caches_all]
    def of_layer(x):
        nd = x.ndim - 1
        return pl.BlockSpec((None,) + x.shape[1:], lambda b, nd=nd: (layer,) + (0,) * nd)

    in_specs += [of_layer(mu), of_layer(rp), of_layer(wup), of_layer(aup), of_layer(qnw), of_layer(knw),
                 full(cos.shape), full(sa.shape), full(sb.shape), of_layer(pw), of_layer(psc)]
    out_specs = [
        pl.BlockSpec((1, 1, d_mix), lambda b: (b, 0, 0)),
        pl.BlockSpec((1, RWKV_HEADS, HEAD, HEAD), lambda b: (b, 0, 0, 0)),
        pl.BlockSpec((1, POOL_BUF, POOL_W), lambda b: (b, 0, 0)),
    ] + [pl.BlockSpec((1, 1, 4, HEAD), lambda b: (b, 0, 0, 0))] * 6
    out_shape = [
        jax.ShapeDtypeStruct((nb, 1, d_mix), f32),
        jax.ShapeDtypeStruct((nb, RWKV_HEADS, HEAD, HEAD), f32),
        jax.ShapeDtypeStruct((nb, POOL_BUF, POOL_W), f32),
    ] + [jax.ShapeDtypeStruct((nb, 1, 4, HEAD), f32)] * 6
    return pl.pallas_call(
        functools.partial(_sample_body, past_len=past_len),
        grid=(nb,),
        in_specs=in_specs,
        out_specs=out_specs,
        out_shape=out_shape,
        scratch_shapes=[pltpu.VMEM((2, 8, ATT_W), f32)],
        compiler_params=pltpu.CompilerParams(
            dimension_semantics=("arbitrary",), vmem_limit_bytes=VMEM_LIMIT),
        name="sample_step",
    )(zs, sh_all, wkv_all, pool_all, *caches_all, mu, rp, wup, aup, qnw, knw, cos, sa, sb, pw, psc)


def _rope_tables(pos):
    half = ROPE_DIMS // 2
    inv = jnp.power(jnp.float32(ROPE_THETA), -jnp.arange(half, dtype=f32) * 2.0 / ROPE_DIMS)
    ang = pos[:, None] * inv[None, :]
    cos, sin = jnp.cos(ang), jnp.sin(ang)
    n = pos.shape[0]
    pad = jnp.zeros((n, HEAD - ROPE_DIMS), f32)
    zero = jnp.zeros((n, half), f32)
    c_head = jnp.concatenate([cos, cos, pad + 1.0], axis=1)
    a_head = jnp.concatenate([-sin, zero, pad], axis=1)
    b_head = jnp.concatenate([zero, sin, pad], axis=1)
    return tuple(jnp.concatenate([x, x], axis=1) for x in (c_head, a_head, b_head))


def kernel(x_prompt, x_sample, state_wkv, state_shift, state_pool,
           cache_k_w128, cache_v_w128, cache_k_w512, cache_v_w512, cache_k_w2048, cache_v_w2048,
           norm_w, w_in, w_out, rwkv_mu, rwkv_w0, rwkv_w_up, rwkv_a0, rwkv_a_up,
           rwkv_k_k, rwkv_k_a, rwkv_r_k, rwkv_ln_w, rwkv_ln_b, q_norm_w, k_norm_w, pool_w, pool_scale):
    b, t, d = x_prompt.shape
    nb, ts, _ = x_sample.shape
    depth = w_in.shape[0]
    d_in = w_in.shape[2]
    assert ts == 1 and t % 512 == 0
    past_len = 16384
    caches_in = ((cache_k_w128, cache_v_w128), (cache_k_w512, cache_v_w512), (cache_k_w2048, cache_v_w2048))
    for (window, _), (ck, _) in zip(ATT_GROUPS, caches_in):
        assert ck.shape[2] == window, "window buffers are expected to be full"

    cos_p, sa_p, sb_p = _rope_tables(jnp.arange(t, dtype=f32))
    cos_s, sa_s, sb_s = _rope_tables(past_len + jnp.arange(1, dtype=f32))

    hp = x_prompt.reshape(b * t, d)
    hs = x_sample.reshape(nb, d)
    npair = RWKV_HEADS // 2
    p_out = [[] for _ in range(9)]
    s_out = [[] for _ in range(9)]
    sh_all = state_shift.reshape(depth, nb, 1, RWKV_COLS)
    caches_all = [jnp.transpose(c, (0, 1, 3, 4, 2)) for pair in caches_in for c in pair]

    nw_all = norm_w.reshape(depth, 1, d)
    mu_all = rwkv_mu.reshape(depth, 1, RWKV_COLS)
    mu4 = rwkv_mu[:, :4 * RWKV_W].reshape(depth, 4, npair, LANES).transpose(0, 2, 1, 3)
    mul = jnp.broadcast_to(rwkv_mu[:, 4 * RWKV_W:].reshape(depth, 1, 1, LANES), (depth, npair, 1, LANES))
    vecs_all = jnp.stack([rwkv_w0, rwkv_a0, rwkv_k_k, rwkv_k_a, rwkv_r_k.reshape(depth, -1),
                          rwkv_ln_w, rwkv_ln_b], axis=1)
    pp_all = jnp.concatenate(
        [mu4, mul, vecs_all.reshape(depth, 7, npair, LANES).transpose(0, 2, 1, 3)], axis=2)
    wup = rwkv_w_up.reshape(depth, LORA, npair, LANES).transpose(0, 2, 1, 3)
    aup = rwkv_a_up.reshape(depth, LORA, npair, LANES).transpose(0, 2, 1, 3)
    zl = jnp.zeros_like(wup)
    wl_all = jnp.concatenate(
        [jnp.concatenate([wup, zl], axis=3), jnp.concatenate([zl, aup], axis=3)], axis=2)
    qnw_all = jnp.tile(q_norm_w, (1, 2)).reshape(depth, 1, LANES)
    knw_all = jnp.tile(k_norm_w, (1, 2)).reshape(depth, 1, LANES)
    psc_all = pool_scale.reshape(depth, 1, POOL_W)
    sin_p = sa_p + sb_p

    for l in range(depth):
        zp, zs = _inproj(hp, hs, nw_all, w_in, l)
        z3 = zp.reshape(b, t, d_in)

        ya, p_wkv = _rwkv_prompt(z3, pp_all, wl_all, l)
        att = _attn_prompt(z3, cos_p, sin_p, qnw_all, knw_all, l)
        yc = _pool_prompt(z3, pool_w, psc_all, l)

        sample = _sample_step(
            zs.reshape(nb, 1, -1), sh_all, state_wkv, state_pool, caches_all, l,
            mu_all, vecs_all, rwkv_w_up, rwkv_a_up, qnw_all, knw_all, cos_s, sa_s, sb_s,
            pool_w, psc_all, past_len=past_len)
        mix_s = sample[0].reshape(nb, -1)

        ys = [ya.reshape(b * t, RWKV_W)] + [y.reshape(b * t, 4 * HEAD) for y in att[:3]] + [yc.reshape(b * t, POOL_W)]
        hp, hs = _outproj(hp, ys, hs, mix_s, w_out, l)

        p_out[0].append(p_wkv)
        p_out[1].append(z3[:, -1, :RWKV_COLS])
        p_out[2].append(z3[:, t - POOL_BUF:, RWKV_COLS + ATT_COLS:RWKV_COLS + ATT_COLS + POOL_W])
        for gi in range(3):
            keep = att[3 + 2 * gi].shape[1]
            p_out[3 + 2 * gi].append(att[3 + 2 * gi].reshape(b, keep, 4, HEAD))
            p_out[4 + 2 * gi].append(att[4 + 2 * gi].reshape(b, keep, 4, HEAD))
        s_out[0].append(sample[1])
        s_out[1].append(zs[:, :RWKV_COLS])
        s_out[2].append(sample[2])
        for i in range(6):
            s_out[3 + i].append(sample[3 + i])

    return (hp.reshape(b, t, d), hs.reshape(nb, 1, d),
            *[jnp.stack(x) for x in p_out], *[jnp.stack(x) for x in s_out])
```

```python
import functools

import jax
import jax.numpy as jnp
from jax import lax
from jax.experimental import pallas as pl
from jax.experimental.pallas import tpu as pltpu

f32 = jnp.float32
bf16 = jnp.bfloat16

HEAD = 64
LANES = 128
NORM_EPS = 1e-6
GN_EPS = 64e-5
RWKV_HEADS = 12
RWKV_W = RWKV_HEADS * HEAD
LORA = 64
RWKV_COLS = 4 * RWKV_W + 2 * LORA
ATT_GROUPS = ((128, 1), (512, 4), (2048, 16))
ATT_W = 12 * HEAD
ATT_COLS = 4 * ATT_W
ROPE_THETA = 500000.0
ROPE_DIMS = 16
QBLK = 128
POOL_WINDOWS = (2, 4, 8, 16)
POOL_W = 512
POOL_BUF = 15
CHUNK = 64
VMEM_LIMIT = 56 * 1024 * 1024

ATT_BLK0 = RWKV_COLS // LANES
POOL_BLK0 = (RWKV_COLS + ATT_COLS) // LANES


def _dot(a, b):
    return jnp.dot(a, b, preferred_element_type=f32)


def _dot_nt(a, b):
    return lax.dot_general(a, b, (((1,), (1,)), ((), ())), preferred_element_type=f32)


def _dot_tn(a, b):
    return lax.dot_general(a, b, (((0,), (0,)), ((), ())), preferred_element_type=f32)


def _split2(x):
    hi = x.astype(bf16)
    lo = (x - hi.astype(f32)).astype(bf16)
    return hi, lo


def _split3(x):
    hi = x.astype(bf16)
    r1 = x - hi.astype(f32)
    mid = r1.astype(bf16)
    lo = (r1 - mid.astype(f32)).astype(bf16)
    return hi, mid, lo


def _silu(x):
    return x * jax.nn.sigmoid(x)


def _rms(x, w):
    ms = jnp.mean(x * x, axis=-1, keepdims=True)
    return x * lax.rsqrt(ms + NORM_EPS) * w


def _pair_consts(n):
    lane = lax.broadcasted_iota(jnp.int32, (n, LANES), 1)
    return lane < HEAD


def _ones_blockdiag():
    r = lax.broadcasted_iota(jnp.int32, (LANES, LANES), 0)
    c = lax.broadcasted_iota(jnp.int32, (LANES, LANES), 1)
    return jnp.where((r < HEAD) == (c < HEAD), 1.0, 0.0).astype(bf16)


def _segsum(x, ones_bd):
    hi, lo = _split2(x)
    return _dot(jnp.concatenate([hi, lo], axis=1), jnp.concatenate([ones_bd, ones_bd], axis=0))


def _inproj_body(xp_ref, xs_ref, nw_ref, w_ref, zp_ref, zs_ref, h_scr, *, tn, n):
    i = pl.program_id(0)
    j = pl.program_id(1)

    @pl.when(j == 0)
    def _():
        h_scr[...] = _rms(xp_ref[...], nw_ref[...]).astype(bf16)

    w = w_ref[...].astype(bf16)
    acc = _dot(h_scr[...], w)
    for c in range(tn // LANES):
        zp_ref[c] = acc[:, c * LANES:(c + 1) * LANES]

    @pl.when(i == 0)
    def _():
        hs = _rms(xs_ref[...], nw_ref[...]).astype(bf16)
        col = pl.multiple_of(j * tn, LANES)
        zs = _dot(hs, w)
        cols = col + lax.broadcasted_iota(jnp.int32, zs.shape, 1)
        zs_ref[:, pl.ds(col, tn)] = jnp.where(cols < n, zs, 0.0)


def _inproj(xp, xs, nw, w_all, layer, *, tm=2048, tn=512):
    m, d = xp.shape
    n = w_all.shape[2]
    nb = xs.shape[0]
    nj = pl.cdiv(n, tn)
    assert n % LANES == 0 and tn % LANES == 0
    return pl.pallas_call(
        functools.partial(_inproj_body, tn=tn, n=n),
        grid=(m // tm, nj),
        in_specs=[
            pl.BlockSpec((tm, d), lambda i, j: (i, 0), pipeline_mode=pl.Buffered(1)),
            pl.BlockSpec((nb, d), lambda i, j: (0, 0)),
            pl.BlockSpec((None, 1, d), lambda i, j: (layer, 0, 0)),
            pl.BlockSpec((None, d, tn), lambda i, j: (layer, 0, j)),
        ],
        out_specs=[
            pl.BlockSpec((tn // LANES, tm, LANES), lambda i, j: (j, i, 0)),
            pl.BlockSpec((nb, nj * tn), lambda i, j: (0, 0)),
        ],
        out_shape=[jax.ShapeDtypeStruct((n // LANES, m, LANES), f32),
                   jax.ShapeDtypeStruct((nb, nj * tn), f32)],
        scratch_shapes=[pltpu.VMEM((tm, d), bf16)],
        compiler_params=pltpu.CompilerParams(
            dimension_semantics=("arbitrary", "arbitrary"), vmem_limit_bytes=VMEM_LIMIT),
        name="inproj",
    )(xp, xs, nw, w_all)


def _outproj_body(*refs, tn, widths):
    n = len(widths)
    xp_ref = refs[0]
    y_refs = refs[1:1 + n]
    xs_ref, ms_ref, w_ref, op_ref, os_ref = refs[1 + n:]
    i = pl.program_id(0)
    j = pl.program_id(1)
    w = w_ref[...].astype(bf16)
    acc = None
    row = 0
    for y_ref, width in zip(y_refs, widths):
        part = _dot(y_ref[...], w[row:row + width])
        acc = part if acc is None else acc + part
        row += width
    op_ref[...] = xp_ref[...] + acc

    @pl.when(i == 0)
    def _():
        col = pl.multiple_of(j * tn, LANES)
        os_ref[:, pl.ds(col, tn)] = xs_ref[:, pl.ds(col, tn)] + _dot(ms_ref[...].astype(bf16), w)


def _outproj(xp, ys, xs, ms, w_all, layer, *, tm=2048, tn=512):
    m, d = xp.shape
    nb = xs.shape[0]
    dm = w_all.shape[1]
    widths = tuple(y.shape[1] for y in ys)
    assert sum(widths) == dm
    return pl.pallas_call(
        functools.partial(_outproj_body, tn=tn, widths=widths),
        grid=(m // tm, d // tn),
        in_specs=[pl.BlockSpec((tm, tn), lambda i, j: (i, j))]
        + [pl.BlockSpec((tm, width), lambda i, j: (i, 0)) for width in widths]
        + [
            pl.BlockSpec((nb, d), lambda i, j: (0, 0)),
            pl.BlockSpec((nb, dm), lambda i, j: (0, 0)),
            pl.BlockSpec((None, dm, tn), lambda i, j: (layer, 0, j)),
        ],
        out_specs=[
            pl.BlockSpec((tm, tn), lambda i, j: (i, j)),
            pl.BlockSpec((nb, d), lambda i, j: (0, 0)),
        ],
        out_shape=[jax.ShapeDtypeStruct((m, d), f32), jax.ShapeDtypeStruct((nb, d), f32)],
        compiler_params=pltpu.CompilerParams(
            dimension_semantics=("arbitrary", "arbitrary"), vmem_limit_bytes=VMEM_LIMIT),
        name="outproj",
    )(xp, *ys, xs, ms, w_all)


_DECAY_SCALE = 0.6065306597126334

_P_MU, _P_W0, _P_A0, _P_KK, _P_KA, _P_RK, _P_LNW, _P_LNB = 0, 5, 6, 7, 8, 9, 10, 11


def _rwkv_chunk_stages(at, bt, kt, rt, v, wcs, consts):
    m0, gmask, bdmask, eye = consts
    m0w = jnp.concatenate([m0, m0], axis=1)
    eye_pair = jnp.where(lax.broadcasted_iota(jnp.int32, (CHUNK, LANES), 1) % CHUNK
                         == lax.broadcasted_iota(jnp.int32, (CHUNK, LANES), 0), 1.0, 0.0)
    zb64 = jnp.zeros((CHUNK, LANES), bf16)
    n = len(wcs)

    def rows(x, c):
        return x[c * CHUNK:(c + 1) * CHUNK]

    def split_heads(xb, m):
        zero = jnp.zeros_like(xb)
        return jnp.concatenate([jnp.where(m, xb, zero), jnp.where(m, zero, xb)], axis=0)

    g, p, x, mc, nc, rp, ov = {}, {}, {}, {}, {}, {}, {}

    def stage_scores(chs):
        for c in chs:
            lhs = jnp.concatenate([rows(at, c), rows(rt, c)], axis=0).astype(bf16)
            rhs = jnp.concatenate([split_heads(rows(bt, c).astype(bf16), m0),
                                   split_heads(rows(kt, c).astype(bf16), m0)], axis=0)
            g[c] = jnp.where(gmask, _dot_nt(lhs, rhs), 0.0)
            p[c] = g[c][0:CHUNK, 0:LANES]

    av, tinv = {}, {}

    def stage_level(chs, lev):
        for c in chs:
            pb = p[c].astype(bf16)
            if lev == 0:
                vs = split_heads(rows(v, c).astype(bf16), m0)
                av[c] = _dot(g[c][0:CHUNK, LANES:].astype(bf16), vs)
                tinv[c] = eye_pair + p[c]
                p[c] = _dot(pb, split_heads(pb, m0))
            elif lev < 5:
                tt_ = split_heads(tinv[c].astype(bf16), m0)
                out = _dot(pb, jnp.concatenate([split_heads(pb, m0), tt_], axis=1))
                p[c] = out[:, 0:LANES]
                tinv[c] = tinv[c] + out[:, LANES:]
            else:
                tinv[c] = tinv[c] + _dot(pb, split_heads(tinv[c].astype(bf16), m0))

    def stage_apply(chs):
        for c in chs:
            rhs = split_heads(jnp.concatenate([av[c], rows(at, c)], axis=1).astype(bf16), m0w)
            x[c] = _dot(tinv[c].astype(bf16), rhs)

    def stage_fold(chs):
        for c in chs:
            xb = x[c].astype(bf16)
            uv = xb[:, 0:LANES]
            ap = xb[:, LANES:]
            vb = rows(v, c).astype(bf16)
            tnl = jnp.concatenate([rows(bt, c) * wcs[c], rows(kt, c) * wcs[c]], axis=0).astype(bf16)
            tnr = jnp.concatenate(
                [jnp.concatenate([ap, uv], axis=1), jnp.concatenate([zb64, vb], axis=1)], axis=0)
            mn = jnp.where(bdmask, _dot_tn(tnl, tnr), 0.0)
            mc[c] = mn[:, 0:LANES] + jnp.where(eye, jnp.broadcast_to(wcs[c], (LANES, LANES)), 0.0)
            nc[c] = mn[:, LANES:]
            l2 = g[c][CHUNK:].astype(bf16)
            r2 = jnp.concatenate([
                split_heads(jnp.concatenate([ap, uv], axis=1), m0w),
                split_heads(jnp.concatenate([zb64, vb], axis=1), m0w)], axis=0)
            ro = _dot(l2, r2)
            rp[c] = rows(rt, c) + ro[:, 0:LANES]
            ov[c] = ro[:, LANES:]

    def carry(c, st):
        seq = _dot(jnp.concatenate([rp[c], mc[c]], axis=0).astype(bf16), st.astype(bf16))
        return seq[0:CHUNK] + ov[c], seq[CHUNK:] + nc[c]

    chs = list(range(n))
    stages = ([functools.partial(stage_scores, chs)]
              + [functools.partial(stage_level, chs, lev) for lev in range(6)]
              + [functools.partial(stage_apply, chs), functools.partial(stage_fold, chs)])
    return stages, carry


def _rwkv_body(zr_ref, zk_ref, zv_ref, zg_ref, zl_ref, pp_ref, wl_ref, y_ref, s_ref,
               sh_scr, st_scr, *, tt, npr, lock):
    tb = pl.program_id(2)
    nt = pl.num_programs(2)
    nch = tt // CHUNK
    nsh = 4 * npr + 1

    @pl.when(tb == 0)
    def _():
        st_scr[...] = jnp.zeros_like(st_scr)
        for i in range(nsh):
            sh_scr[i, 7:8, :] = jnp.zeros((1, LANES), f32)

    @pl.when(tb > 0)
    def _():
        for i in range(nsh):
            sh_scr[i, 7:8, :] = sh_scr[i, 7 + tt:8 + tt, :]

    def shifted(idx, z, mu):
        sh_scr[idx, 8:8 + tt, :] = z
        zprev = sh_scr[idx, pl.ds(7, tt), :]
        return z + (zprev - z) * mu

    m0t = _pair_consts(tt)
    ones_bd = _ones_blockdiag()
    lo = shifted(4 * npr, zl_ref[0, 0], pp_ref[0, _P_MU + 4:_P_MU + 5])
    xl = jnp.where(m0t, jnp.tanh(lo), lo).astype(bf16)

    m0 = _pair_consts(CHUNK)
    gr = lax.broadcasted_iota(jnp.int32, (2 * CHUNK, 2 * LANES), 0)
    gc = lax.broadcasted_iota(jnp.int32, (2 * CHUNK, 2 * LANES), 1)
    gt = gr % CHUNK
    gs = gc % CHUNK
    gmask = (gt > gs) | ((gr >= CHUNK) & (gt == gs))
    br = lax.broadcasted_iota(jnp.int32, (LANES, 2 * LANES), 0)
    bc = lax.broadcasted_iota(jnp.int32, (LANES, 2 * LANES), 1)
    bdmask = (br < HEAD) == ((bc % LANES) < HEAD)
    er = lax.broadcasted_iota(jnp.int32, (LANES, LANES), 0)
    ec = lax.broadcasted_iota(jnp.int32, (LANES, LANES), 1)
    eye = er == ec
    consts = (m0, gmask, bdmask, eye)
    tr = lax.broadcasted_iota(jnp.int32, (CHUNK, CHUNK), 0)
    tc = lax.broadcasted_iota(jnp.int32, (CHUNK, CHUNK), 1)
    tril = jnp.where(tc <= tr, 1.0, 0.0).astype(bf16)
    tril3 = jnp.concatenate([tril, tril, tril], axis=1)

    def prepare(s, out):
        pp = pp_ref[s]
        r, k, v, g = [shifted(4 * s + i, ref[s, 0], pp[_P_MU + i:_P_MU + i + 1])
                      for i, ref in enumerate((zr_ref, zk_ref, zv_ref, zg_ref))]
        la = _dot(xl, wl_ref[s].astype(bf16))
        kk = k * pp[_P_KK:_P_KK + 1]
        kk_ss = _segsum(kk * kk, ones_bd)
        yield
        ld = -_DECAY_SCALE * jax.nn.sigmoid(pp[_P_W0:_P_W0 + 1] + la[:, 0:LANES])
        a = jax.nn.sigmoid(pp[_P_A0:_P_A0 + 1] + la[:, LANES:])
        kk = kk * jnp.minimum(lax.rsqrt(kk_ss), 1e12)
        k2 = k * (1.0 + (a - 1.0) * pp[_P_KA:_P_KA + 1])
        h3 = _split3(ld)
        cl = jnp.concatenate([
            _dot(tril3, jnp.concatenate([h[c * CHUNK:(c + 1) * CHUNK] for h in h3], axis=0))
            for c in range(nch)], axis=0)
        bonus = _segsum(r * k2 * pp[_P_RK:_P_RK + 1], ones_bd) * v
        yield
        e_in = jnp.exp(cl)
        e_neg = jnp.exp(-cl)
        wcs = [e_in[(c + 1) * CHUNK - 1:(c + 1) * CHUNK] for c in range(nch)]
        out["stages"], out["carry"] = _rwkv_chunk_stages(
            -kk * jnp.exp(cl - ld), kk * a * e_neg, k2 * e_neg, r * e_in, v, wcs, consts)
        out["epi"] = (pp, g, bonus)

    def finish(s, outs, st, epi):
        pp, g, bonus = epi
        o = jnp.concatenate(outs, axis=0)
        mean = _segsum(o, ones_bd) * (1.0 / HEAD)
        dl = o - mean
        var = _segsum(dl * dl, ones_bd) * (1.0 / HEAD)
        on = dl * lax.rsqrt(var + GN_EPS) * pp[_P_LNW:_P_LNW + 1] + pp[_P_LNB:_P_LNB + 1] + bonus
        y_ref[0, :, s * LANES:(s + 1) * LANES] = (on * _silu(g)).astype(bf16)
        st_scr[s] = st

    def drain(gen):
        for _ in gen:
            pass

    def lockstep(gens):
        gens = list(gens)
        while gens:
            gens = [g for g in gens if next(g, StopIteration) is not StopIteration]
            yield

    preps = [dict() for _ in range(npr)]
    units = [list(range(i, min(i + lock, npr))) for i in range(0, npr, lock)]
    drain(lockstep(prepare(s, preps[s]) for s in units[0]))
    prev = None
    for ui, unit in enumerate(units):
        nxt = (lockstep(prepare(s, preps[s]) for s in units[ui + 1]) if ui + 1 < len(units) else iter(()))
        for stage_row in zip(*[preps[s]["stages"] for s in unit]):
            for stage in stage_row:
                stage()
            if prev is not None:
                for cur in prev:
                    if cur["todo"]:
                        cur["step"]()
            next(nxt, None)
        drain(nxt)
        if prev is not None:
            while any(cur["todo"] for cur in prev):
                for cur in prev:
                    if cur["todo"]:
                        cur["step"]()
            for cur in prev:
                finish(cur["s"], cur["outs"], cur["st"][0], cur["epi"])
        prev = []
        for s in unit:
            cur = {"s": s, "outs": [], "st": [st_scr[s]], "epi": preps[s]["epi"], "todo": list(range(nch))}

            def step(cur=cur, carry=preps[s]["carry"]):
                o_c, cur["st"][0] = carry(cur["todo"].pop(0), cur["st"][0])
                cur["outs"].append(o_c)

            cur["step"] = step
            prev.append(cur)
    while any(cur["todo"] for cur in prev):
        for cur in prev:
            if cur["todo"]:
                cur["step"]()
    for cur in prev:
        finish(cur["s"], cur["outs"], cur["st"][0], cur["epi"])

    @pl.when(tb == nt - 1)
    def _():
        for s in range(npr):
            stt = st_scr[s].T
            s_ref[0, 2 * s] = stt[0:HEAD, 0:HEAD]
            s_ref[0, 2 * s + 1] = stt[HEAD:, HEAD:]


def _rwkv_prompt(z3, pp, wl, layer, *, tt=512, npr=6, lock=2):
    _, b, t, _ = z3.shape
    npair = RWKV_HEADS // 2
    ngrp = npair // npr
    wide = npr * LANES

    def zspec(part):
        return pl.BlockSpec((npr, 1, tt, LANES), lambda bi, p, tb, part=part: (part * ngrp + p, bi, tb, 0))

    lora_blk = 4 * RWKV_W // LANES
    return pl.pallas_call(
        functools.partial(_rwkv_body, tt=tt, npr=npr, lock=lock),
        grid=(b, ngrp, t // tt),
        in_specs=[
            zspec(0), zspec(1), zspec(2), zspec(3),
            pl.BlockSpec((1, 1, tt, LANES), lambda bi, p, tb: (lora_blk, bi, tb, 0)),
            pl.BlockSpec((None, npr, 12, LANES), lambda bi, p, tb: (layer, p, 0, 0)),
            pl.BlockSpec((None, npr, LANES, 2 * LANES), lambda bi, p, tb: (layer, p, 0, 0)),
        ],
        out_specs=[
            pl.BlockSpec((1, tt, wide), lambda bi, p, tb: (bi, tb, p)),
            pl.BlockSpec((1, 2 * npr, HEAD, HEAD), lambda bi, p, tb: (bi, p, 0, 0)),
        ],
        out_shape=[jax.ShapeDtypeStruct((b, t, RWKV_W), bf16),
                   jax.ShapeDtypeStruct((b, RWKV_HEADS, HEAD, HEAD), f32)],
        scratch_shapes=[pltpu.VMEM((4 * npr + 1, tt + 8, LANES), f32), pltpu.VMEM((npr, LANES, LANES), f32)],
        compiler_params=pltpu.CompilerParams(
            dimension_semantics=("arbitrary", "arbitrary", "arbitrary"), vmem_limit_bytes=VMEM_LIMIT),
        name="rwkv_prompt",
    )(z3, z3, z3, z3, z3, pp, wl)


def _rope_pair(x, cos, sa, sb):
    return x * cos + pltpu.roll(x, LANES - ROPE_DIMS // 2, 1) * sa + pltpu.roll(x, ROPE_DIMS // 2, 1) * sb


def _attn_body(*refs, t):
    qkvg = [[r.at[0] for r in refs[4 * gi:4 * gi + 4]] for gi in range(3)]
    cos_ref, sin_ref, qnw_ref, knw_ref = refs[12:16]
    y_refs = refs[16:19]
    kv_refs = [refs[19 + 2 * gi:21 + 2 * gi] for gi in range(3)]
    qn_scr, kn_scr, o_scr, lse_scr = refs[25:29]

    rb = 256
    ones_bd = _ones_blockdiag()
    m0q = _pair_consts(QBLK)
    scale = HEAD ** -0.5
    half = ROPE_DIMS // 2
    pj = lax.broadcasted_iota(jnp.int32, (LANES, LANES), 0)
    pi = lax.broadcasted_iota(jnp.int32, (LANES, LANES), 1)
    pin = pi % HEAD
    perm = jnp.where(((pin < half) & (pj == pi + half)) | ((pin >= half) & (pin < ROPE_DIMS) & (pj == pi - half)),
                     1.0, 0.0).astype(bf16)
    zpad = jnp.zeros((LANES, LANES), bf16)
    sum_swap = jnp.concatenate([jnp.concatenate([ones_bd, zpad], axis=1),
                                jnp.concatenate([zpad, perm], axis=1)], axis=0)

    for gi, (window, dil) in enumerate(ATT_GROUPS):
        q_ref, k_ref, v_ref, g_ref = qkvg[gi]
        pk_ref, pv_ref = kv_refs[gi]
        keep = min(window, t)

        def norm_rows(i, carry, q_ref=q_ref, k_ref=k_ref):
            jobs = []
            for u in range(2):
                rows = pl.ds(pl.multiple_of((2 * i + u) * rb, rb), rb)
                for src, nw, dst in ((q_ref, qnw_ref, qn_scr), (k_ref, knw_ref, kn_scr)):
                    x = src[0, rows, :]
                    y = x * nw[...]
                    hi, lo = _split2(jnp.concatenate([x * x, y], axis=1))
                    jobs.append((y, _dot(hi, sum_swap) + _dot(lo, sum_swap), dst, rows))
            for y, res, dst, rows in jobs:
                rs = lax.rsqrt(res[:, 0:LANES] * (1.0 / HEAD) + NORM_EPS)
                dst[rows, :] = (y * cos_ref[rows, :] + res[:, LANES:] * sin_ref[rows, :]) * rs
            return carry

        lax.fori_loop(0, t // (2 * rb), norm_rows, 0)
        pk_ref[0] = kn_scr[t - keep:t, :]
        pv_ref[0] = v_ref[0, t - keep:t, :]

        n_sub = t // dil
        n_blk = n_sub // QBLK

        def tiles(starts, nk, first, v_ref=v_ref, dil=dil, gi=gi):
            qi = lax.broadcasted_iota(jnp.int32, (QBLK, nk), 0)
            kj = lax.broadcasted_iota(jnp.int32, (QBLK, nk), 1)
            mask = (kj <= qi) if first else ((kj >= qi) & (kj <= qi + QBLK))
            scores, vts = [], []
            for q0, k0 in starts:
                qt = qn_scr[pl.ds(q0, QBLK, stride=dil), :] * scale
                kt = kn_scr[pl.ds(k0, nk, stride=dil), :].astype(bf16)
                vts.append(v_ref[0, pl.ds(k0, nk, stride=dil), :].astype(bf16))
                for hh in range(2):
                    qh = jnp.where(m0q, qt, 0.0) if hh == 0 else jnp.where(m0q, 0.0, qt)
                    scores.append(_dot_nt(qh.astype(bf16), kt))
            probs, sums, lses = [], [], []
            for s in scores:
                s = jnp.where(mask, s, -jnp.inf)
                m = jnp.max(s, axis=-1, keepdims=True)
                p = jnp.exp(s - m)
                l = jnp.sum(p, axis=-1, keepdims=True)
                probs.append(p.astype(bf16))
                sums.append(l)
                lses.append(m + jnp.log(l))
            for ti, (q0, _) in enumerate(starts):
                oh = [_dot(probs[2 * ti + hh], vts[ti]) / sums[2 * ti + hh] for hh in range(2)]
                o_scr[gi, pl.ds(q0, QBLK, stride=dil), :] = jnp.where(m0q, oh[0], oh[1])
                lse_scr[gi, pl.ds(q0, QBLK, stride=dil), :] = jnp.where(
                    m0q, jnp.broadcast_to(lses[2 * ti], (QBLK, LANES)),
                    jnp.broadcast_to(lses[2 * ti + 1], (QBLK, LANES)))

        per_first = min(8, dil)

        def first_tiles(i, carry, tiles=tiles, per=per_first):
            tiles([(i * per + u, i * per + u) for u in range(per)], QBLK, True)
            return carry

        lax.fori_loop(0, dil // per_first, first_tiles, 0)

        if n_blk > 1:
            n_later = dil * (n_blk - 1)
            per_later = max(p for p in range(1, 7) if n_later % p == 0)

            def later_tiles(i, carry, tiles=tiles, dil=dil, per=per_later):
                starts = []
                for u in range(per):
                    idx = i * per + u
                    r = idx % dil
                    blk = idx // dil + 1
                    starts.append((r + blk * QBLK * dil, r + (blk - 1) * QBLK * dil))
                tiles(starts, 2 * QBLK, False)
                return carry

            lax.fori_loop(0, dil * (n_blk - 1) // per_later, later_tiles, 0)

    def combine(i, carry):
        rows = pl.ds(pl.multiple_of(i * rb, rb), rb)
        ls = [lse_scr[gi, rows, :] for gi in range(3)]
        mx = jnp.maximum(jnp.maximum(ls[0], ls[1]), ls[2])
        es = [jnp.exp(l - mx) for l in ls]
        inv = 1.0 / (es[0] + es[1] + es[2])
        for gi in range(3):
            gate = qkvg[gi][3][0, rows, :]
            y_refs[gi][0, rows, :] = (o_scr[gi, rows, :] * (es[gi] * inv) * _silu(gate)).astype(bf16)
        return carry

    lax.fori_loop(0, t // rb, combine, 0)


def _attn_prompt(z3, cos, sin, qnw, knw, layer):
    _, b, t, _ = z3.shape
    in_specs = []
    for gi in range(3):
        for part in range(4):
            off = ATT_BLK0 + part * 6 + 2 * gi
            in_specs.append(pl.BlockSpec((1, 1, t, LANES), lambda bi, jp, off=off: (off + jp, bi, 0, 0)))
    in_specs += [pl.BlockSpec((t, LANES), lambda bi, jp: (0, 0))] * 2
    in_specs += [pl.BlockSpec((None, 1, LANES), lambda bi, jp: (layer, 0, 0))] * 2
    out_specs = [pl.BlockSpec((1, t, LANES), lambda bi, jp: (bi, 0, jp))] * 3
    out_shape = [jax.ShapeDtypeStruct((b, t, 4 * HEAD), bf16)] * 3
    for window, _ in ATT_GROUPS:
        keep = min(window, t)
        out_specs += [pl.BlockSpec((1, keep, LANES), lambda bi, jp: (bi, 0, jp))] * 2
        out_shape += [jax.ShapeDtypeStruct((b, keep, 4 * HEAD), f32)] * 2
    outs = pl.pallas_call(
        functools.partial(_attn_body, t=t),
        grid=(b, 2),
        in_specs=in_specs,
        out_specs=out_specs,
        out_shape=out_shape,
        scratch_shapes=[pltpu.VMEM((t, LANES), f32), pltpu.VMEM((t, LANES), f32),
                        pltpu.VMEM((3, t, LANES), f32), pltpu.VMEM((3, t, LANES), f32)],
        compiler_params=pltpu.CompilerParams(
            dimension_semantics=("arbitrary", "arbitrary"), vmem_limit_bytes=VMEM_LIMIT),
        name="attn_prompt",
    )(*([z3] * 12), cos, sin, qnw, knw)
    return outs


def _pool_body(*refs, tt):
    ng = len(POOL_WINDOWS)
    u_refs = [r.at[0] for r in refs[0:ng]]
    g_refs = [r.at[0] for r in refs[ng:2 * ng]]
    w_ref, sc_ref, y_ref, scr = refs[2 * ng:]
    tb = pl.program_id(1)
    pad = max(POOL_WINDOWS)

    @pl.when(tb == 0)
    def _():
        for gi in range(ng):
            scr[gi, 0:pad, :] = jnp.zeros((pad, LANES), f32)

    @pl.when(tb > 0)
    def _():
        for gi in range(ng):
            scr[gi, 0:pad, :] = scr[gi, tt:tt + pad, :]

    pos = (tb * tt + lax.broadcasted_iota(jnp.int32, (tt, 1), 0)).astype(f32)
    for gi, w in enumerate(POOL_WINDOWS):
        u = u_refs[gi][0]
        scr[gi, pad:pad + tt, :] = u
        acc = u
        for k in range(1, w):
            acc = acc + scr[gi, pl.ds(pad - k, tt), :]
        d = acc / jnp.minimum(float(w), pos + 1.0) - u
        dd = _dot(d.astype(bf16), w_ref[gi].astype(bf16)) * sc_ref[:, gi * LANES:(gi + 1) * LANES]
        y_ref[0, :, gi * LANES:(gi + 1) * LANES] = (dd * _silu(g_refs[gi][0])).astype(bf16)


def _pool_prompt(z3, pw, psc, layer, *, tt=512):
    _, b, t, _ = z3.shape
    ng = len(POOL_WINDOWS)
    pad = max(POOL_WINDOWS)
    return pl.pallas_call(
        functools.partial(_pool_body, tt=tt),
        grid=(b, t // tt),
        in_specs=[pl.BlockSpec((1, 1, tt, LANES), lambda bi, tb, off=POOL_BLK0 + i: (off, bi, tb, 0))
                  for i in range(2 * ng)]
        + [
            pl.BlockSpec((None, ng, LANES, LANES), lambda bi, tb: (layer, 0, 0, 0)),
            pl.BlockSpec((None, 1, POOL_W), lambda bi, tb: (layer, 0, 0)),
        ],
        out_specs=pl.BlockSpec((1, tt, POOL_W), lambda bi, tb: (bi, tb, 0)),
        out_shape=jax.ShapeDtypeStruct((b, t, POOL_W), bf16),
        scratch_shapes=[pltpu.VMEM((ng, tt + pad, LANES), f32)],
        compiler_params=pltpu.CompilerParams(
            dimension_semantics=("arbitrary", "arbitrary"), vmem_limit_bytes=VMEM_LIMIT),
        name="pool_prompt",
    )(*([z3] * (2 * ng)), pw, psc)


def _bcast8(x):
    return jnp.broadcast_to(x, (8, x.shape[-1]))


def _sample_body(z_ref, sh_ref, wkv_ref, pool_ref, ck0, cv0, ck1, cv1, ck2, cv2,
                 mu_ref, rp_ref, wup_ref, aup_ref, qnw_ref, knw_ref, cos_ref, sa_ref, sb_ref,
                 pw_ref, psc_ref,
                 mix_ref, swkv_ref, spool_ref, sk0, sv0, sk1, sv1, sk2, sv2,
                 qk_scr, *, past_len):
    z = z_ref[0]
    za = z[:, 0:RWKV_COLS]
    zb = z[:, RWKV_COLS:RWKV_COLS + ATT_COLS]
    zc = z[:, RWKV_COLS + ATT_COLS:RWKV_COLS + ATT_COLS + 2 * POOL_W]
    er = lax.broadcasted_iota(jnp.int32, (HEAD, HEAD), 0)
    ec = lax.broadcasted_iota(jnp.int32, (HEAD, HEAD), 1)
    eye = er == ec

    zs = za + (sh_ref[0] - za) * mu_ref[...]
    w_ = RWKV_W
    r, k, v, g = (zs[:, i * w_:(i + 1) * w_] for i in range(4))
    w_dn = zs[:, 4 * w_:4 * w_ + LORA]
    a_dn = zs[:, 4 * w_ + LORA:]
    rp = rp_ref[...]
    w0, a0, k_k, k_a, r_k, ln_w, ln_b = (rp[i:i + 1] for i in range(7))
    lw = _dot(_bcast8(jnp.tanh(w_dn)).astype(bf16), wup_ref[...].astype(bf16))[0:1]
    la = _dot(_bcast8(a_dn).astype(bf16), aup_ref[...].astype(bf16))[0:1]
    decay = jnp.exp(-jnp.exp(-jax.nn.softplus(-(w0 + lw)) - 0.5))
    a = jax.nn.sigmoid(a0 + la)
    kk = k * k_k
    k2 = k * (1.0 + (a - 1.0) * k_a)
    heads = range(RWKV_HEADS)
    hsl = [slice(h * HEAD, (h + 1) * HEAD) for h in heads]

    def lane_sum(x):
        return jnp.sum(x, axis=-1, keepdims=True)

    kk_ss = [lane_sum(kk[:, hs] * kk[:, hs]) for hs in hsl]
    bonus = [lane_sum(r[:, hs] * k2[:, hs] * r_k[:, hs]) for hs in hsl]
    v_col = [lane_sum(jnp.where(eye, jnp.broadcast_to(v[:, hs], (HEAD, HEAD)), 0.0)) for hs in hsl]
    kkn = [kk[:, hs] / jnp.maximum(jnp.sqrt(ss), 1e-12) for hs, ss in zip(hsl, kk_ss)]
    sa_col = [lane_sum(wkv_ref[0, h] * (-kkn[h])) for h in heads]
    sn = [wkv_ref[0, h] * decay[:, hsl[h]] + sa_col[h] * (kkn[h] * a[:, hsl[h]]) + v_col[h] * k2[:, hsl[h]]
          for h in heads]
    for h in heads:
        swkv_ref[0, h] = sn[h]
    o_col = [lane_sum(sn[h] * r[:, hsl[h]]) for h in heads]
    o = [jnp.sum(jnp.where(eye, jnp.broadcast_to(oc, (HEAD, HEAD)), 0.0), axis=0, keepdims=True) for oc in o_col]
    mean = [jnp.mean(x, axis=-1, keepdims=True) for x in o]
    dl = [x - mu_ for x, mu_ in zip(o, mean)]
    var = [jnp.mean(x * x, axis=-1, keepdims=True) for x in dl]
    for h in heads:
        hs = hsl[h]
        on = dl[h] * lax.rsqrt(var[h] + GN_EPS) * ln_w[:, hs] + ln_b[:, hs] + bonus[h] * v[:, hs]
        mix_ref[0, :, hs] = on * _silu(g[:, hs])

    aw = ATT_W
    q, kx, vx, gx = (zb[:, i * aw:(i + 1) * aw] for i in range(4))
    m0 = _pair_consts(8)
    for hp in range(ATT_W // LANES):
        ls = slice(hp * LANES, (hp + 1) * LANES)
        for idx, (src, nw) in enumerate(((q, qnw_ref), (kx, knw_ref))):
            x = _bcast8(src[:, ls])
            sq = x * x
            s0 = jnp.sum(jnp.where(m0, sq, 0.0), axis=-1, keepdims=True)
            s1 = jnp.sum(jnp.where(m0, 0.0, sq), axis=-1, keepdims=True)
            ms = jnp.where(m0, s0, s1) * (1.0 / HEAD)
            xn = x * lax.rsqrt(ms + NORM_EPS) * nw[...]
            qk_scr[idx, :, ls] = _rope_pair(xn, cos_ref[...], sa_ref[...], sb_ref[...])
    qn = qk_scr[0, 0:1, :]
    kn = qk_scr[1, 0:1, :]
    scale = HEAD ** -0.5

    def heads4(row, gi):
        return jnp.concatenate([row[:, (gi * 4 + j) * HEAD:(gi * 4 + j + 1) * HEAD] for j in range(4)], axis=0)

    def to_col(row):
        return jnp.sum(jnp.where(eye, jnp.broadcast_to(row, (HEAD, HEAD)), 0.0), axis=-1, keepdims=True)

    def to_row(col):
        return jnp.sum(jnp.where(eye, jnp.broadcast_to(col, (HEAD, HEAD)), 0.0), axis=0, keepdims=True)

    caches = ((ck0, cv0, sk0, sv0), (ck1, cv1, sk1, sv1), (ck2, cv2, sk2, sv2))
    windows = []
    for gi, (ck, cv, sk, sv) in enumerate(caches):
        sk[0, 0] = heads4(kn, gi)
        sv[0, 0] = heads4(vx, gi)
        row_id = lax.broadcasted_iota(jnp.int32, (1, ck.shape[-1]), 1)
        windows.append((row_id % ATT_GROUPS[gi][1]) == 0)
    combos = [(gi, j) for gi in range(3) for j in range(4)]
    asl = [slice((gi * 4 + j) * HEAD, (gi * 4 + j + 1) * HEAD) for gi, j in combos]
    qh = [qn[:, hs] * scale for hs in asl]
    q_col = [to_col(x) for x in qh]
    s_new = [jnp.sum(kn[:, hs] * x, axis=-1, keepdims=True) for hs, x in zip(asl, qh)]
    s_all = [jnp.where(windows[gi], jnp.sum(caches[gi][0][0, j] * qc, axis=0, keepdims=True), -jnp.inf)
             for (gi, j), qc in zip(combos, q_col)]
    m_all = [jnp.maximum(jnp.max(s, axis=-1, keepdims=True), sn_) for s, sn_ in zip(s_all, s_new)]
    p_all = [jnp.exp(s - m) for s, m in zip(s_all, m_all)]
    p_new = [jnp.exp(sn_ - m) for sn_, m in zip(s_new, m_all)]
    l_all = [jnp.sum(p, axis=-1, keepdims=True) + pn for p, pn in zip(p_all, p_new)]
    o_colv = [jnp.sum(caches[gi][1][0, j] * p, axis=-1, keepdims=True) for (gi, j), p in zip(combos, p_all)]
    o_rows = [(to_row(oc) + pn * vx[:, hs]) / l for oc, pn, hs, l in zip(o_colv, p_new, asl, l_all)]
    lse = [m + jnp.log(l) for m, l in zip(m_all, l_all)]
    for j in range(4):
        mx = jnp.maximum(jnp.maximum(lse[j], lse[4 + j]), lse[8 + j])
        es = [jnp.exp(lse[gi * 4 + j] - mx) for gi in range(3)]
        inv = 1.0 / (es[0] + es[1] + es[2])
        for gi in range(3):
            ci = gi * 4 + j
            mix_ref[0, :, RWKV_W + ci * HEAD:RWKV_W + (ci + 1) * HEAD] = (
                o_rows[ci] * (es[gi] * inv) * _silu(gx[:, asl[ci]]))

    u = zc[:, 0:POOL_W]
    gate = zc[:, POOL_W:]
    prev = pool_ref[0]
    for gi, w in enumerate(POOL_WINDOWS):
        cs = slice(gi * LANES, (gi + 1) * LANES)
        ug = u[:, cs]
        sw = jnp.sum(prev[POOL_BUF - (w - 1):POOL_BUF, cs], axis=0, keepdims=True) + ug
        cnt = min(float(w), float(past_len) + 1.0)
        d = sw / cnt - ug
        dd = _dot(_bcast8(d).astype(bf16), pw_ref[gi].astype(bf16))[0:1] * psc_ref[:, cs]
        mix_ref[0, :, RWKV_W + ATT_W + gi * LANES:RWKV_W + ATT_W + (gi + 1) * LANES] = dd * _silu(gate[:, cs])
    spool_ref[0, 0:POOL_BUF - 1, :] = prev[1:POOL_BUF]
    spool_ref[0, POOL_BUF - 1:POOL_BUF, :] = u


def _sample_step(zs, sh_all, wkv_all, pool_all, caches_all, layer, mu, rp, wup, aup, qnw, knw, cos, sa, sb,
                 pw, psc, *, past_len):
    nb = zs.shape[0]
    d_in = zs.shape[-1]
    d_mix = RWKV_W + ATT_W + POOL_W

    def full(shape):
        nd = len(shape)
        return pl.BlockSpec(shape, lambda b, nd=nd: (0,) * nd)

    in_specs = [
        pl.BlockSpec((1, 1, d_in), lambda b: (b, 0, 0)),
        pl.BlockSpec((None, 1, 1, RWKV_COLS), lambda b: (layer, b, 0, 0)),
        pl.BlockSpec((None, 1, RWKV_HEADS, HEAD, HEAD), lambda b: (layer, b, 0, 0, 0)),
        pl.BlockSpec((None, 1, POOL_BUF, POOL_W), lambda b: (layer, b, 0, 0)),
    ]
    in_specs += [pl.BlockSpec((None, 1, 4, HEAD, c.shape[-1]), lambda b: (layer, b, 0, 0, 0)) for c in caches_all]
    def of_layer(x):
        nd = x.ndim - 1
        return pl.BlockSpec((None,) + x.shape[1:], lambda b, nd=nd: (layer,) + (0,) * nd)

    in_specs += [of_layer(mu), of_layer(rp), of_layer(wup), of_layer(aup), of_layer(qnw), of_layer(knw),
                 full(cos.shape), full(sa.shape), full(sb.shape), of_layer(pw), of_layer(psc)]
    out_specs = [
        pl.BlockSpec((1, 1, d_mix), lambda b: (b, 0, 0)),
        pl.BlockSpec((1, RWKV_HEADS, HEAD, HEAD), lambda b: (b, 0, 0, 0)),
        pl.BlockSpec((1, POOL_BUF, POOL_W), lambda b: (b, 0, 0)),
    ] + [pl.BlockSpec((1, 1, 4, HEAD), lambda b: (b, 0, 0, 0))] * 6
    out_shape = [
        jax.ShapeDtypeStruct((nb, 1, d_mix), f32),
        jax.ShapeDtypeStruct((nb, RWKV_HEADS, HEAD, HEAD), f32),
        jax.ShapeDtypeStruct((nb, POOL_BUF, POOL_W), f32),
    ] + [jax.ShapeDtypeStruct((nb, 1, 4, HEAD), f32)] * 6
    return pl.pallas_call(
        functools.partial(_sample_body, past_len=past_len),
        grid=(nb,),
        in_specs=in_specs,
        out_specs=out_specs,
        out_shape=out_shape,
        scratch_shapes=[pltpu.VMEM((2, 8, ATT_W), f32)],
        compiler_params=pltpu.CompilerParams(
            dimension_semantics=("arbitrary",), vmem_limit_bytes=VMEM_LIMIT),
        name="sample_step",
    )(zs, sh_all, wkv_all, pool_all, *caches_all, mu, rp, wup, aup, qnw, knw, cos, sa, sb, pw, psc)


def _rope_tables(pos):
    half = ROPE_DIMS // 2
    inv = jnp.power(jnp.float32(ROPE_THETA), -jnp.arange(half, dtype=f32) * 2.0 / ROPE_DIMS)
    ang = pos[:, None] * inv[None, :]
    cos, sin = jnp.cos(ang), jnp.sin(ang)
    n = pos.shape[0]
    pad = jnp.zeros((n, HEAD - ROPE_DIMS), f32)
    zero = jnp.zeros((n, half), f32)
    c_head = jnp.concatenate([cos, cos, pad + 1.0], axis=1)
    a_head = jnp.concatenate([-sin, zero, pad], axis=1)
    b_head = jnp.concatenate([zero, sin, pad], axis=1)
    return tuple(jnp.concatenate([x, x], axis=1) for x in (c_head, a_head, b_head))


def kernel(x_prompt, x_sample, state_wkv, state_shift, state_pool,
           cache_k_w128, cache_v_w128, cache_k_w512, cache_v_w512, cache_k_w2048, cache_v_w2048,
           norm_w, w_in, w_out, rwkv_mu, rwkv_w0, rwkv_w_up, rwkv_a0, rwkv_a_up,
           rwkv_k_k, rwkv_k_a, rwkv_r_k, rwkv_ln_w, rwkv_ln_b, q_norm_w, k_norm_w, pool_w, pool_scale):
    b, t, d = x_prompt.shape
    nb, ts, _ = x_sample.shape
    depth = w_in.shape[0]
    d_in = w_in.shape[2]
    assert ts == 1 and t % 512 == 0
    past_len = 16384
    caches_in = ((cache_k_w128, cache_v_w128), (cache_k_w512, cache_v_w512), (cache_k_w2048, cache_v_w2048))
    for (window, _), (ck, _) in zip(ATT_GROUPS, caches_in):
        assert ck.shape[2] == window, "window buffers are expected to be full"

    cos_p, sa_p, sb_p = _rope_tables(jnp.arange(t, dtype=f32))
    cos_s, sa_s, sb_s = _rope_tables(past_len + jnp.arange(1, dtype=f32))

    hp = x_prompt.reshape(b * t, d)
    hs = x_sample.reshape(nb, d)
    npair = RWKV_HEADS // 2
    p_out = [[] for _ in range(9)]
    s_out = [[] for _ in range(9)]
    sh_all = state_shift.reshape(depth, nb, 1, RWKV_COLS)
    caches_all = [jnp.transpose(c, (0, 1, 3, 4, 2)) for pair in caches_in for c in pair]

    nw_all = norm_w.reshape(depth, 1, d)
    mu_all = rwkv_mu.reshape(depth, 1, RWKV_COLS)
    mu4 = rwkv_mu[:, :4 * RWKV_W].reshape(depth, 4, npair, LANES).transpose(0, 2, 1, 3)
    mul = jnp.broadcast_to(rwkv_mu[:, 4 * RWKV_W:].reshape(depth, 1, 1, LANES), (depth, npair, 1, LANES))
    vecs_all = jnp.stack([rwkv_w0, rwkv_a0, rwkv_k_k, rwkv_k_a, rwkv_r_k.reshape(depth, -1),
                          rwkv_ln_w, rwkv_ln_b], axis=1)
    pp_all = jnp.concatenate(
        [mu4, mul, vecs_all.reshape(depth, 7, npair, LANES).transpose(0, 2, 1, 3)], axis=2)
    wup = rwkv_w_up.reshape(depth, LORA, npair, LANES).transpose(0, 2, 1, 3)
    aup = rwkv_a_up.reshape(depth, LORA, npair, LANES).transpose(0, 2, 1, 3)
    zl = jnp.zeros_like(wup)
    wl_all = jnp.concatenate(
        [jnp.concatenate([wup, zl], axis=3), jnp.concatenate([zl, aup], axis=3)], axis=2)
    qnw_all = jnp.tile(q_norm_w, (1, 2)).reshape(depth, 1, LANES)
    knw_all = jnp.tile(k_norm_w, (1, 2)).reshape(depth, 1, LANES)
    psc_all = pool_scale.reshape(depth, 1, POOL_W)
    sin_p = sa_p + sb_p

    for l in range(depth):
        zp, zs = _inproj(hp, hs, nw_all, w_in, l)
        z3 = zp.reshape(d_in // LANES, b, t, LANES)

        ya, p_wkv = _rwkv_prompt(z3, pp_all, wl_all, l)
        att = _attn_prompt(z3, cos_p, sin_p, qnw_all, knw_all, l)
        yc = _pool_prompt(z3, pool_w, psc_all, l)

        sample = _sample_step(
            zs.reshape(nb, 1, -1), sh_all, state_wkv, state_pool, caches_all, l,
            mu_all, vecs_all, rwkv_w_up, rwkv_a_up, qnw_all, knw_all, cos_s, sa_s, sb_s,
            pool_w, psc_all, past_len=past_len)
        mix_s = sample[0].reshape(nb, -1)

        ys = [ya.reshape(b * t, RWKV_W)] + [y.reshape(b * t, 4 * HEAD) for y in att[:3]] + [yc.reshape(b * t, POOL_W)]
        hp, hs = _outproj(hp, ys, hs, mix_s, w_out, l)

        p_out[0].append(p_wkv)
        p_out[1].append(z3[:ATT_BLK0, :, -1, :].transpose(1, 0, 2).reshape(b, RWKV_COLS))
        p_out[2].append(z3[POOL_BLK0:POOL_BLK0 + POOL_W // LANES, :, t - POOL_BUF:, :]
                        .transpose(1, 2, 0, 3).reshape(b, POOL_BUF, POOL_W))
        for gi in range(3):
            keep = att[3 + 2 * gi].shape[1]
            p_out[3 + 2 * gi].append(att[3 + 2 * gi].reshape(b, keep, 4, HEAD))
            p_out[4 + 2 * gi].append(att[4 + 2 * gi].reshape(b, keep, 4, HEAD))
        s_out[0].append(sample[1])
        s_out[1].append(zs[:, :RWKV_COLS])
        s_out[2].append(sample[2])
        for i in range(6):
            s_out[3 + i].append(sample[3 + i])

    return (hp.reshape(b, t, d), hs.reshape(nb, 1, d),
            *[jnp.stack(x) for x in p_out], *[jnp.stack(x) for x in s_out])
```

```python
import functools

import jax
import jax.numpy as jnp
from jax import lax
from jax.experimental import pallas as pl
from jax.experimental.pallas import tpu as pltpu

f32 = jnp.float32
bf16 = jnp.bfloat16

HEAD = 64
LANES = 128
NORM_EPS = 1e-6
GN_EPS = 64e-5
RWKV_HEADS = 12
RWKV_W = RWKV_HEADS * HEAD
LORA = 64
RWKV_COLS = 4 * RWKV_W + 2 * LORA
ATT_GROUPS = ((128, 1), (512, 4), (2048, 16))
ATT_W = 12 * HEAD
ATT_COLS = 4 * ATT_W
ROPE_THETA = 500000.0
ROPE_DIMS = 16
QBLK = 128
POOL_WINDOWS = (2, 4, 8, 16)
POOL_W = 512
POOL_BUF = 15
CHUNK = 64
VMEM_LIMIT = 56 * 1024 * 1024

ATT_BLK0 = RWKV_COLS // LANES
POOL_BLK0 = (RWKV_COLS + ATT_COLS) // LANES


def _dot(a, b):
    return jnp.dot(a, b, preferred_element_type=f32)


def _dot_nt(a, b):
    return lax.dot_general(a, b, (((1,), (1,)), ((), ())), preferred_element_type=f32)


def _dot_tn(a, b):
    return lax.dot_general(a, b, (((0,), (0,)), ((), ())), preferred_element_type=f32)


def _split2(x):
    hi = x.astype(bf16)
    lo = (x - hi.astype(f32)).astype(bf16)
    return hi, lo


def _split3(x):
    hi = x.astype(bf16)
    r1 = x - hi.astype(f32)
    mid = r1.astype(bf16)
    lo = (r1 - mid.astype(f32)).astype(bf16)
    return hi, mid, lo


def _silu(x):
    return x * jax.nn.sigmoid(x)


def _rms(x, w):
    ms = jnp.mean(x * x, axis=-1, keepdims=True)
    return x * lax.rsqrt(ms + NORM_EPS) * w


def _pair_consts(n):
    lane = lax.broadcasted_iota(jnp.int32, (n, LANES), 1)
    return lane < HEAD


def _ones_blockdiag():
    r = lax.broadcasted_iota(jnp.int32, (LANES, LANES), 0)
    c = lax.broadcasted_iota(jnp.int32, (LANES, LANES), 1)
    return jnp.where((r < HEAD) == (c < HEAD), 1.0, 0.0).astype(bf16)


def _segsum(x, ones_bd):
    hi, lo = _split2(x)
    return _dot(jnp.concatenate([hi, lo], axis=1), jnp.concatenate([ones_bd, ones_bd], axis=0))


def _inproj_body(xp_ref, xs_ref, nw_ref, w_ref, zp_ref, zs_ref, h_scr, *, tn, n):
    i = pl.program_id(0)
    j = pl.program_id(1)

    @pl.when(j == 0)
    def _():
        h_scr[...] = _rms(xp_ref[...], nw_ref[...]).astype(bf16)

    nj = pl.num_programs(1)
    tail = n - (pl.cdiv(n, tn) - 1) * tn

    def project(width):
        w = w_ref[:, 0:width].astype(bf16)
        acc = _dot(h_scr[...], w)
        for c in range(width // LANES):
            zp_ref[c] = acc[:, c * LANES:(c + 1) * LANES]

        @pl.when(i == 0)
        def _():
            hs = _rms(xs_ref[...], nw_ref[...]).astype(bf16)
            col = pl.multiple_of(j * tn, LANES)
            zs_ref[:, pl.ds(col, width)] = _dot(hs, w)

    @pl.when(j < nj - 1)
    def _():
        project(tn)

    @pl.when(j == nj - 1)
    def _():
        project(tail)


def _inproj(xp, xs, nw, w_all, layer, *, tm=2048, tn=512):
    m, d = xp.shape
    n = w_all.shape[2]
    nb = xs.shape[0]
    nj = pl.cdiv(n, tn)
    assert n % LANES == 0 and tn % LANES == 0
    return pl.pallas_call(
        functools.partial(_inproj_body, tn=tn, n=n),
        grid=(m // tm, nj),
        in_specs=[
            pl.BlockSpec((tm, d), lambda i, j: (i, 0), pipeline_mode=pl.Buffered(1)),
            pl.BlockSpec((nb, d), lambda i, j: (0, 0)),
            pl.BlockSpec((None, 1, d), lambda i, j: (layer, 0, 0)),
            pl.BlockSpec((None, d, tn), lambda i, j: (layer, 0, j)),
        ],
        out_specs=[
            pl.BlockSpec((tn // LANES, tm, LANES), lambda i, j: (j, i, 0)),
            pl.BlockSpec((nb, n), lambda i, j: (0, 0)),
        ],
        out_shape=[jax.ShapeDtypeStruct((n // LANES, m, LANES), f32),
                   jax.ShapeDtypeStruct((nb, n), f32)],
        scratch_shapes=[pltpu.VMEM((tm, d), bf16)],
        compiler_params=pltpu.CompilerParams(
            dimension_semantics=("arbitrary", "arbitrary"), vmem_limit_bytes=VMEM_LIMIT),
        name="inproj",
    )(xp, xs, nw, w_all)


def _outproj_body(*refs, tn, widths):
    n = len(widths)
    xp_ref = refs[0]
    y_refs = refs[1:1 + n]
    xs_ref, ms_ref, w_ref, op_ref, os_ref = refs[1 + n:]
    i = pl.program_id(0)
    j = pl.program_id(1)
    w = w_ref[...].astype(bf16)
    acc = None
    row = 0
    for y_ref, width in zip(y_refs, widths):
        part = _dot(y_ref[...], w[row:row + width])
        acc = part if acc is None else acc + part
        row += width
    op_ref[...] = xp_ref[...] + acc

    @pl.when(i == 0)
    def _():
        col = pl.multiple_of(j * tn, LANES)
        os_ref[:, pl.ds(col, tn)] = xs_ref[:, pl.ds(col, tn)] + _dot(ms_ref[...].astype(bf16), w)


def _outproj(xp, ys, xs, ms, w_all, layer, *, tm=2048, tn=512):
    m, d = xp.shape
    nb = xs.shape[0]
    dm = w_all.shape[1]
    widths = tuple(y.shape[1] for y in ys)
    assert sum(widths) == dm
    return pl.pallas_call(
        functools.partial(_outproj_body, tn=tn, widths=widths),
        grid=(m // tm, d // tn),
        in_specs=[pl.BlockSpec((tm, tn), lambda i, j: (i, j))]
        + [pl.BlockSpec((tm, width), lambda i, j: (i, 0)) for width in widths]
        + [
            pl.BlockSpec((nb, d), lambda i, j: (0, 0)),
            pl.BlockSpec((nb, dm), lambda i, j: (0, 0)),
            pl.BlockSpec((None, dm, tn), lambda i, j: (layer, 0, j)),
        ],
        out_specs=[
            pl.BlockSpec((tm, tn), lambda i, j: (i, j)),
            pl.BlockSpec((nb, d), lambda i, j: (0, 0)),
        ],
        out_shape=[jax.ShapeDtypeStruct((m, d), f32), jax.ShapeDtypeStruct((nb, d), f32)],
        compiler_params=pltpu.CompilerParams(
            dimension_semantics=("arbitrary", "arbitrary"), vmem_limit_bytes=VMEM_LIMIT),
        name="outproj",
    )(xp, *ys, xs, ms, w_all)


_DECAY_SCALE = 0.6065306597126334

_P_MU, _P_W0, _P_A0, _P_KK, _P_KA, _P_RK, _P_LNW, _P_LNB = 0, 5, 6, 7, 8, 9, 10, 11


def _rwkv_chunk_stages(at, bt, kt, rt, v, wcs, consts):
    m0, gmask, bdmask, eye = consts
    m0w = jnp.concatenate([m0, m0], axis=1)
    eye_pair = jnp.where(lax.broadcasted_iota(jnp.int32, (CHUNK, LANES), 1) % CHUNK
                         == lax.broadcasted_iota(jnp.int32, (CHUNK, LANES), 0), 1.0, 0.0)
    zb64 = jnp.zeros((CHUNK, LANES), bf16)
    n = len(wcs)

    def rows(x, c):
        return x[c * CHUNK:(c + 1) * CHUNK]

    def split_heads(xb, m):
        zero = jnp.zeros_like(xb)
        return jnp.concatenate([jnp.where(m, xb, zero), jnp.where(m, zero, xb)], axis=0)

    g, p, x, mc, nc, rp, ov = {}, {}, {}, {}, {}, {}, {}

    def stage_scores(chs):
        for c in chs:
            lhs = jnp.concatenate([rows(at, c), rows(rt, c)], axis=0).astype(bf16)
            rhs = jnp.concatenate([split_heads(rows(bt, c).astype(bf16), m0),
                                   split_heads(rows(kt, c).astype(bf16), m0)], axis=0)
            g[c] = jnp.where(gmask, _dot_nt(lhs, rhs), 0.0)
            p[c] = g[c][0:CHUNK, 0:LANES]

    av, tinv = {}, {}

    def stage_level(chs, lev):
        for c in chs:
            pb = p[c].astype(bf16)
            if lev == 0:
                vs = split_heads(rows(v, c).astype(bf16), m0)
                av[c] = _dot(g[c][0:CHUNK, LANES:].astype(bf16), vs)
                tinv[c] = eye_pair + p[c]
                p[c] = _dot(pb, split_heads(pb, m0))
            elif lev < 5:
                tt_ = split_heads(tinv[c].astype(bf16), m0)
                out = _dot(pb, jnp.concatenate([split_heads(pb, m0), tt_], axis=1))
                p[c] = out[:, 0:LANES]
                tinv[c] = tinv[c] + out[:, LANES:]
            else:
                tinv[c] = tinv[c] + _dot(pb, split_heads(tinv[c].astype(bf16), m0))

    def stage_apply(chs):
        for c in chs:
            rhs = split_heads(jnp.concatenate([av[c], rows(at, c)], axis=1).astype(bf16), m0w)
            x[c] = _dot(tinv[c].astype(bf16), rhs)

    def stage_fold(chs):
        for c in chs:
            xb = x[c].astype(bf16)
            uv = xb[:, 0:LANES]
            ap = xb[:, LANES:]
            vb = rows(v, c).astype(bf16)
            tnl = jnp.concatenate([rows(bt, c) * wcs[c], rows(kt, c) * wcs[c]], axis=0).astype(bf16)
            tnr = jnp.concatenate(
                [jnp.concatenate([ap, uv], axis=1), jnp.concatenate([zb64, vb], axis=1)], axis=0)
            mn = jnp.where(bdmask, _dot_tn(tnl, tnr), 0.0)
            mc[c] = mn[:, 0:LANES] + jnp.where(eye, jnp.broadcast_to(wcs[c], (LANES, LANES)), 0.0)
            nc[c] = mn[:, LANES:]
            l2 = g[c][CHUNK:].astype(bf16)
            r2 = jnp.concatenate([
                split_heads(jnp.concatenate([ap, uv], axis=1), m0w),
                split_heads(jnp.concatenate([zb64, vb], axis=1), m0w)], axis=0)
            ro = _dot(l2, r2)
            rp[c] = rows(rt, c) + ro[:, 0:LANES]
            ov[c] = ro[:, LANES:]

    def carry(c, st):
        seq = _dot(jnp.concatenate([rp[c], mc[c]], axis=0).astype(bf16), st.astype(bf16))
        return seq[0:CHUNK] + ov[c], seq[CHUNK:] + nc[c]

    chs = list(range(n))
    stages = ([functools.partial(stage_scores, chs)]
              + [functools.partial(stage_level, chs, lev) for lev in range(6)]
              + [functools.partial(stage_apply, chs), functools.partial(stage_fold, chs)])
    return stages, carry


def _rwkv_body(zr_ref, zk_ref, zv_ref, zg_ref, zl_ref, pp_ref, wl_ref, y_ref, s_ref,
               sh_scr, st_scr, *, tt, npr, lock):
    tb = pl.program_id(2)
    nt = pl.num_programs(2)
    nch = tt // CHUNK
    nsh = 4 * npr + 1

    @pl.when(tb == 0)
    def _():
        st_scr[...] = jnp.zeros_like(st_scr)
        for i in range(nsh):
            sh_scr[i, 7:8, :] = jnp.zeros((1, LANES), f32)

    @pl.when(tb > 0)
    def _():
        for i in range(nsh):
            sh_scr[i, 7:8, :] = sh_scr[i, 7 + tt:8 + tt, :]

    def shifted(idx, z, mu):
        sh_scr[idx, 8:8 + tt, :] = z
        zprev = sh_scr[idx, pl.ds(7, tt), :]
        return z + (zprev - z) * mu

    m0t = _pair_consts(tt)
    ones_bd = _ones_blockdiag()
    lo = shifted(4 * npr, zl_ref[0, 0], pp_ref[0, _P_MU + 4:_P_MU + 5])
    xl = jnp.where(m0t, jnp.tanh(lo), lo).astype(bf16)

    m0 = _pair_consts(CHUNK)
    gr = lax.broadcasted_iota(jnp.int32, (2 * CHUNK, 2 * LANES), 0)
    gc = lax.broadcasted_iota(jnp.int32, (2 * CHUNK, 2 * LANES), 1)
    gt = gr % CHUNK
    gs = gc % CHUNK
    gmask = (gt > gs) | ((gr >= CHUNK) & (gt == gs))
    br = lax.broadcasted_iota(jnp.int32, (LANES, 2 * LANES), 0)
    bc = lax.broadcasted_iota(jnp.int32, (LANES, 2 * LANES), 1)
    bdmask = (br < HEAD) == ((bc % LANES) < HEAD)
    er = lax.broadcasted_iota(jnp.int32, (LANES, LANES), 0)
    ec = lax.broadcasted_iota(jnp.int32, (LANES, LANES), 1)
    eye = er == ec
    consts = (m0, gmask, bdmask, eye)
    tr = lax.broadcasted_iota(jnp.int32, (CHUNK, CHUNK), 0)
    tc = lax.broadcasted_iota(jnp.int32, (CHUNK, CHUNK), 1)
    tril = jnp.where(tc <= tr, 1.0, 0.0).astype(bf16)
    tril3 = jnp.concatenate([tril, tril, tril], axis=1)

    def prepare(s, out):
        pp = pp_ref[s]
        r, k, v, g = [shifted(4 * s + i, ref[s, 0], pp[_P_MU + i:_P_MU + i + 1])
                      for i, ref in enumerate((zr_ref, zk_ref, zv_ref, zg_ref))]
        la = _dot(xl, wl_ref[s].astype(bf16))
        kk = k * pp[_P_KK:_P_KK + 1]
        kk_ss = _segsum(kk * kk, ones_bd)
        yield
        ld = -_DECAY_SCALE * jax.nn.sigmoid(pp[_P_W0:_P_W0 + 1] + la[:, 0:LANES])
        a = jax.nn.sigmoid(pp[_P_A0:_P_A0 + 1] + la[:, LANES:])
        kk = kk * jnp.minimum(lax.rsqrt(kk_ss), 1e12)
        k2 = k * (1.0 + (a - 1.0) * pp[_P_KA:_P_KA + 1])
        h3 = _split3(ld)
        cl = jnp.concatenate([
            _dot(tril3, jnp.concatenate([h[c * CHUNK:(c + 1) * CHUNK] for h in h3], axis=0))
            for c in range(nch)], axis=0)
        bonus = _segsum(r * k2 * pp[_P_RK:_P_RK + 1], ones_bd) * v
        yield
        e_in = jnp.exp(cl)
        e_neg = jnp.exp(-cl)
        wcs = [e_in[(c + 1) * CHUNK - 1:(c + 1) * CHUNK] for c in range(nch)]
        out["stages"], out["carry"] = _rwkv_chunk_stages(
            -kk * jnp.exp(cl - ld), kk * a * e_neg, k2 * e_neg, r * e_in, v, wcs, consts)
        out["epi"] = (pp, g, bonus)

    def finish(s, outs, st, epi):
        pp, g, bonus = epi
        o = jnp.concatenate(outs, axis=0)
        mean = _segsum(o, ones_bd) * (1.0 / HEAD)
        dl = o - mean
        var = _segsum(dl * dl, ones_bd) * (1.0 / HEAD)
        on = dl * lax.rsqrt(var + GN_EPS) * pp[_P_LNW:_P_LNW + 1] + pp[_P_LNB:_P_LNB + 1] + bonus
        y_ref[0, :, s * LANES:(s + 1) * LANES] = (on * _silu(g)).astype(bf16)
        st_scr[s] = st

    def drain(gen):
        for _ in gen:
            pass

    def lockstep(gens):
        gens = list(gens)
        while gens:
            gens = [g for g in gens if next(g, StopIteration) is not StopIteration]
            yield

    preps = [dict() for _ in range(npr)]
    units = [list(range(i, min(i + lock, npr))) for i in range(0, npr, lock)]
    drain(lockstep(prepare(s, preps[s]) for s in units[0]))
    prev = None
    for ui, unit in enumerate(units):
        nxt = (lockstep(prepare(s, preps[s]) for s in units[ui + 1]) if ui + 1 < len(units) else iter(()))
        for stage_row in zip(*[preps[s]["stages"] for s in unit]):
            for stage in stage_row:
                stage()
            if prev is not None:
                for cur in prev:
                    if cur["todo"]:
                        cur["step"]()
            next(nxt, None)
        drain(nxt)
        if prev is not None:
            while any(cur["todo"] for cur in prev):
                for cur in prev:
                    if cur["todo"]:
                        cur["step"]()
            for cur in prev:
                finish(cur["s"], cur["outs"], cur["st"][0], cur["epi"])
        prev = []
        for s in unit:
            cur = {"s": s, "outs": [], "st": [st_scr[s]], "epi": preps[s]["epi"], "todo": list(range(nch))}

            def step(cur=cur, carry=preps[s]["carry"]):
                o_c, cur["st"][0] = carry(cur["todo"].pop(0), cur["st"][0])
                cur["outs"].append(o_c)

            cur["step"] = step
            prev.append(cur)
    while any(cur["todo"] for cur in prev):
        for cur in prev:
            if cur["todo"]:
                cur["step"]()
    for cur in prev:
        finish(cur["s"], cur["outs"], cur["st"][0], cur["epi"])

    @pl.when(tb == nt - 1)
    def _():
        for s in range(npr):
            stt = st_scr[s].T
            s_ref[0, 2 * s] = stt[0:HEAD, 0:HEAD]
            s_ref[0, 2 * s + 1] = stt[HEAD:, HEAD:]


def _rwkv_prompt(z3, pp, wl, layer, *, tt=512, npr=6, lock=2):
    _, b, t, _ = z3.shape
    npair = RWKV_HEADS // 2
    ngrp = npair // npr
    wide = npr * LANES

    def zspec(part):
        return pl.BlockSpec((npr, 1, tt, LANES), lambda bi, p, tb, part=part: (part * ngrp + p, bi, tb, 0))

    lora_blk = 4 * RWKV_W // LANES
    return pl.pallas_call(
        functools.partial(_rwkv_body, tt=tt, npr=npr, lock=lock),
        grid=(b, ngrp, t // tt),
        in_specs=[
            zspec(0), zspec(1), zspec(2), zspec(3),
            pl.BlockSpec((1, 1, tt, LANES), lambda bi, p, tb: (lora_blk, bi, tb, 0)),
            pl.BlockSpec((None, npr, 12, LANES), lambda bi, p, tb: (layer, p, 0, 0)),
            pl.BlockSpec((None, npr, LANES, 2 * LANES), lambda bi, p, tb: (layer, p, 0, 0)),
        ],
        out_specs=[
            pl.BlockSpec((1, tt, wide), lambda bi, p, tb: (bi, tb, p)),
            pl.BlockSpec((1, 2 * npr, HEAD, HEAD), lambda bi, p, tb: (bi, p, 0, 0)),
        ],
        out_shape=[jax.ShapeDtypeStruct((b, t, RWKV_W), bf16),
                   jax.ShapeDtypeStruct((b, RWKV_HEADS, HEAD, HEAD), f32)],
        scratch_shapes=[pltpu.VMEM((4 * npr + 1, tt + 8, LANES), f32), pltpu.VMEM((npr, LANES, LANES), f32)],
        compiler_params=pltpu.CompilerParams(
            dimension_semantics=("arbitrary", "arbitrary", "arbitrary"), vmem_limit_bytes=VMEM_LIMIT),
        name="rwkv_prompt",
    )(z3, z3, z3, z3, z3, pp, wl)


def _rope_pair(x, cos, sa, sb):
    return x * cos + pltpu.roll(x, LANES - ROPE_DIMS // 2, 1) * sa + pltpu.roll(x, ROPE_DIMS // 2, 1) * sb


def _attn_body(*refs, t):
    qkvg = [[r.at[0] for r in refs[4 * gi:4 * gi + 4]] for gi in range(3)]
    cos_ref, sin_ref, qnw_ref, knw_ref = refs[12:16]
    y_refs = refs[16:19]
    kv_refs = [refs[19 + 2 * gi:21 + 2 * gi] for gi in range(3)]
    qn_scr, kn_scr, o_scr, lse_scr = refs[25:29]

    rb = 256
    ones_bd = _ones_blockdiag()
    m0q = _pair_consts(QBLK)
    scale = HEAD ** -0.5
    half = ROPE_DIMS // 2
    pj = lax.broadcasted_iota(jnp.int32, (LANES, LANES), 0)
    pi = lax.broadcasted_iota(jnp.int32, (LANES, LANES), 1)
    pin = pi % HEAD
    perm = jnp.where(((pin < half) & (pj == pi + half)) | ((pin >= half) & (pin < ROPE_DIMS) & (pj == pi - half)),
                     1.0, 0.0).astype(bf16)
    zpad = jnp.zeros((LANES, LANES), bf16)
    sum_swap = jnp.concatenate([jnp.concatenate([ones_bd, zpad], axis=1),
                                jnp.concatenate([zpad, perm], axis=1)], axis=0)

    for gi, (window, dil) in enumerate(ATT_GROUPS):
        q_ref, k_ref, v_ref, g_ref = qkvg[gi]
        pk_ref, pv_ref = kv_refs[gi]
        keep = min(window, t)

        def norm_rows(i, carry, q_ref=q_ref, k_ref=k_ref):
            jobs = []
            for u in range(2):
                rows = pl.ds(pl.multiple_of((2 * i + u) * rb, rb), rb)
                for src, nw, dst in ((q_ref, qnw_ref, qn_scr), (k_ref, knw_ref, kn_scr)):
                    x = src[0, rows, :]
                    y = x * nw[...]
                    hi, lo = _split2(jnp.concatenate([x * x, y], axis=1))
                    jobs.append((y, _dot(hi, sum_swap) + _dot(lo, sum_swap), dst, rows))
            for y, res, dst, rows in jobs:
                rs = lax.rsqrt(res[:, 0:LANES] * (1.0 / HEAD) + NORM_EPS)
                dst[rows, :] = (y * cos_ref[rows, :] + res[:, LANES:] * sin_ref[rows, :]) * rs
            return carry

        lax.fori_loop(0, t // (2 * rb), norm_rows, 0)
        pk_ref[0] = kn_scr[t - keep:t, :]
        pv_ref[0] = v_ref[0, t - keep:t, :]

        n_sub = t // dil
        n_blk = n_sub // QBLK

        def tiles(starts, nk, first, v_ref=v_ref, dil=dil, gi=gi):
            qi = lax.broadcasted_iota(jnp.int32, (QBLK, nk), 0)
            kj = lax.broadcasted_iota(jnp.int32, (QBLK, nk), 1)
            mask = (kj <= qi) if first else ((kj >= qi) & (kj <= qi + QBLK))
            scores, vts = [], []
            for q0, k0 in starts:
                qt = qn_scr[pl.ds(q0, QBLK, stride=dil), :] * scale
                kt = kn_scr[pl.ds(k0, nk, stride=dil), :].astype(bf16)
                vts.append(v_ref[0, pl.ds(k0, nk, stride=dil), :].astype(bf16))
                for hh in range(2):
                    qh = jnp.where(m0q, qt, 0.0) if hh == 0 else jnp.where(m0q, 0.0, qt)
                    scores.append(_dot_nt(qh.astype(bf16), kt))
            probs, sums, lses = [], [], []
            for s in scores:
                s = jnp.where(mask, s, -jnp.inf)
                m = jnp.max(s, axis=-1, keepdims=True)
                p = jnp.exp(s - m)
                l = jnp.sum(p, axis=-1, keepdims=True)
                probs.append(p.astype(bf16))
                sums.append(l)
                lses.append(m + jnp.log(l))
            for ti, (q0, _) in enumerate(starts):
                oh = [_dot(probs[2 * ti + hh], vts[ti]) / sums[2 * ti + hh] for hh in range(2)]
                o_scr[gi, pl.ds(q0, QBLK, stride=dil), :] = jnp.where(m0q, oh[0], oh[1])
                lse_scr[gi, pl.ds(q0, QBLK, stride=dil), :] = jnp.where(
                    m0q, jnp.broadcast_to(lses[2 * ti], (QBLK, LANES)),
                    jnp.broadcast_to(lses[2 * ti + 1], (QBLK, LANES)))

        per_first = min(8, dil)

        def first_tiles(i, carry, tiles=tiles, per=per_first):
            tiles([(i * per + u, i * per + u) for u in range(per)], QBLK, True)
            return carry

        lax.fori_loop(0, dil // per_first, first_tiles, 0)

        if n_blk > 1:
            n_later = dil * (n_blk - 1)
            per_later = max(p for p in range(1, 7) if n_later % p == 0)

            def later_tiles(i, carry, tiles=tiles, dil=dil, per=per_later):
                starts = []
                for u in range(per):
                    idx = i * per + u
                    r = idx % dil
                    blk = idx // dil + 1
                    starts.append((r + blk * QBLK * dil, r + (blk - 1) * QBLK * dil))
                tiles(starts, 2 * QBLK, False)
                return carry

            lax.fori_loop(0, dil * (n_blk - 1) // per_later, later_tiles, 0)

    def combine(i, carry):
        rows = pl.ds(pl.multiple_of(i * rb, rb), rb)
        ls = [lse_scr[gi, rows, :] for gi in range(3)]
        mx = jnp.maximum(jnp.maximum(ls[0], ls[1]), ls[2])
        es = [jnp.exp(l - mx) for l in ls]
        inv = 1.0 / (es[0] + es[1] + es[2])
        for gi in range(3):
            gate = qkvg[gi][3][0, rows, :]
            y_refs[gi][0, rows, :] = (o_scr[gi, rows, :] * (es[gi] * inv) * _silu(gate)).astype(bf16)
        return carry

    lax.fori_loop(0, t // rb, combine, 0)


def _attn_prompt(z3, cos, sin, qnw, knw, layer):
    _, b, t, _ = z3.shape
    in_specs = []
    for gi in range(3):
        for part in range(4):
            off = ATT_BLK0 + part * 6 + 2 * gi
            in_specs.append(pl.BlockSpec((1, 1, t, LANES), lambda bi, jp, off=off: (off + jp, bi, 0, 0)))
    in_specs += [pl.BlockSpec((t, LANES), lambda bi, jp: (0, 0))] * 2
    in_specs += [pl.BlockSpec((None, 1, LANES), lambda bi, jp: (layer, 0, 0))] * 2
    out_specs = [pl.BlockSpec((1, t, LANES), lambda bi, jp: (bi, 0, jp))] * 3
    out_shape = [jax.ShapeDtypeStruct((b, t, 4 * HEAD), bf16)] * 3
    for window, _ in ATT_GROUPS:
        keep = min(window, t)
        out_specs += [pl.BlockSpec((1, keep, LANES), lambda bi, jp: (bi, 0, jp))] * 2
        out_shape += [jax.ShapeDtypeStruct((b, keep, 4 * HEAD), f32)] * 2
    outs = pl.pallas_call(
        functools.partial(_attn_body, t=t),
        grid=(b, 2),
        in_specs=in_specs,
        out_specs=out_specs,
        out_shape=out_shape,
        scratch_shapes=[pltpu.VMEM((t, LANES), f32), pltpu.VMEM((t, LANES), f32),
                        pltpu.VMEM((3, t, LANES), f32), pltpu.VMEM((3, t, LANES), f32)],
        compiler_params=pltpu.CompilerParams(
            dimension_semantics=("arbitrary", "arbitrary"), vmem_limit_bytes=VMEM_LIMIT),
        name="attn_prompt",
    )(*([z3] * 12), cos, sin, qnw, knw)
    return outs


def _pool_body(*refs, tt):
    ng = len(POOL_WINDOWS)
    u_refs = [r.at[0] for r in refs[0:ng]]
    g_refs = [r.at[0] for r in refs[ng:2 * ng]]
    w_ref, sc_ref, y_ref, scr = refs[2 * ng:]
    tb = pl.program_id(1)
    pad = max(POOL_WINDOWS)

    @pl.when(tb == 0)
    def _():
        for gi in range(ng):
            scr[gi, 0:pad, :] = jnp.zeros((pad, LANES), f32)

    @pl.when(tb > 0)
    def _():
        for gi in range(ng):
            scr[gi, 0:pad, :] = scr[gi, tt:tt + pad, :]

    pos = (tb * tt + lax.broadcasted_iota(jnp.int32, (tt, 1), 0)).astype(f32)
    for gi, w in enumerate(POOL_WINDOWS):
        u = u_refs[gi][0]
        scr[gi, pad:pad + tt, :] = u
        acc = u
        for k in range(1, w):
            acc = acc + scr[gi, pl.ds(pad - k, tt), :]
        d = acc / jnp.minimum(float(w), pos + 1.0) - u
        dd = _dot(d.astype(bf16), w_ref[gi].astype(bf16)) * sc_ref[:, gi * LANES:(gi + 1) * LANES]
        y_ref[0, :, gi * LANES:(gi + 1) * LANES] = (dd * _silu(g_refs[gi][0])).astype(bf16)


def _pool_prompt(z3, pw, psc, layer, *, tt=1024):
    _, b, t, _ = z3.shape
    ng = len(POOL_WINDOWS)
    pad = max(POOL_WINDOWS)
    return pl.pallas_call(
        functools.partial(_pool_body, tt=tt),
        grid=(b, t // tt),
        in_specs=[pl.BlockSpec((1, 1, tt, LANES), lambda bi, tb, off=POOL_BLK0 + i: (off, bi, tb, 0))
                  for i in range(2 * ng)]
        + [
            pl.BlockSpec((None, ng, LANES, LANES), lambda bi, tb: (layer, 0, 0, 0)),
            pl.BlockSpec((None, 1, POOL_W), lambda bi, tb: (layer, 0, 0)),
        ],
        out_specs=pl.BlockSpec((1, tt, POOL_W), lambda bi, tb: (bi, tb, 0)),
        out_shape=jax.ShapeDtypeStruct((b, t, POOL_W), bf16),
        scratch_shapes=[pltpu.VMEM((ng, tt + pad, LANES), f32)],
        compiler_params=pltpu.CompilerParams(
            dimension_semantics=("arbitrary", "arbitrary"), vmem_limit_bytes=VMEM_LIMIT),
        name="pool_prompt",
    )(*([z3] * (2 * ng)), pw, psc)


def _bcast8(x):
    return jnp.broadcast_to(x, (8, x.shape[-1]))


def _sample_body(*refs, past_len, nbs):
    gens = [_sample_one(bi, *refs, past_len=past_len) for bi in range(nbs)]
    while gens:
        gens = [g for g in gens if next(g, StopIteration) is not StopIteration]


def _sample_one(bi, z_ref, sh_ref, wkv_ref, pool_ref, ck0, cv0, ck1, cv1, ck2, cv2,
                mu_ref, rp_ref, wup_ref, aup_ref, qnw_ref, knw_ref, cos_ref, sa_ref, sb_ref,
                pw_ref, psc_ref,
                mix_ref, swkv_ref, spool_ref, sk0, sv0, sk1, sv1, sk2, sv2,
                qk_scr, *, past_len):
    z = z_ref[bi]
    za = z[:, 0:RWKV_COLS]
    zb = z[:, RWKV_COLS:RWKV_COLS + ATT_COLS]
    zc = z[:, RWKV_COLS + ATT_COLS:RWKV_COLS + ATT_COLS + 2 * POOL_W]
    er = lax.broadcasted_iota(jnp.int32, (HEAD, HEAD), 0)
    ec = lax.broadcasted_iota(jnp.int32, (HEAD, HEAD), 1)
    eye = er == ec

    zs = za + (sh_ref[bi] - za) * mu_ref[...]
    w_ = RWKV_W
    r, k, v, g = (zs[:, i * w_:(i + 1) * w_] for i in range(4))
    w_dn = zs[:, 4 * w_:4 * w_ + LORA]
    a_dn = zs[:, 4 * w_ + LORA:]
    rp = rp_ref[...]
    w0, a0, k_k, k_a, r_k, ln_w, ln_b = (rp[i:i + 1] for i in range(7))
    lw = _dot(_bcast8(jnp.tanh(w_dn)).astype(bf16), wup_ref[...].astype(bf16))[0:1]
    la = _dot(_bcast8(a_dn).astype(bf16), aup_ref[...].astype(bf16))[0:1]
    yield
    decay = jnp.exp(-_DECAY_SCALE * jax.nn.sigmoid(w0 + lw))
    a = jax.nn.sigmoid(a0 + la)
    kk = k * k_k
    k2 = k * (1.0 + (a - 1.0) * k_a)
    heads = range(RWKV_HEADS)
    hsl = [slice(h * HEAD, (h + 1) * HEAD) for h in heads]

    def lane_sum(x):
        return jnp.sum(x, axis=-1, keepdims=True)

    kk_ss = [lane_sum(kk[:, hs] * kk[:, hs]) for hs in hsl]
    bonus = [lane_sum(r[:, hs] * k2[:, hs] * r_k[:, hs]) for hs in hsl]
    v_col = [lane_sum(jnp.where(eye, jnp.broadcast_to(v[:, hs], (HEAD, HEAD)), 0.0)) for hs in hsl]
    yield
    kkn = [kk[:, hs] / jnp.maximum(jnp.sqrt(ss), 1e-12) for hs, ss in zip(hsl, kk_ss)]
    sa_col = [lane_sum(wkv_ref[bi, h] * (-kkn[h])) for h in heads]
    yield
    sn = [wkv_ref[bi, h] * decay[:, hsl[h]] + sa_col[h] * (kkn[h] * a[:, hsl[h]]) + v_col[h] * k2[:, hsl[h]]
          for h in heads]
    for h in heads:
        swkv_ref[bi, h] = sn[h]
    o_col = [lane_sum(sn[h] * r[:, hsl[h]]) for h in heads]
    yield
    o = [jnp.sum(jnp.where(eye, jnp.broadcast_to(oc, (HEAD, HEAD)), 0.0), axis=0, keepdims=True) for oc in o_col]
    mean = [jnp.mean(x, axis=-1, keepdims=True) for x in o]
    yield
    dl = [x - mu_ for x, mu_ in zip(o, mean)]
    var = [jnp.mean(x * x, axis=-1, keepdims=True) for x in dl]
    yield
    for h in heads:
        hs = hsl[h]
        on = dl[h] * lax.rsqrt(var[h] + GN_EPS) * ln_w[:, hs] + ln_b[:, hs] + bonus[h] * v[:, hs]
        mix_ref[bi, :, hs] = on * _silu(g[:, hs])

    aw = ATT_W
    q, kx, vx, gx = (zb[:, i * aw:(i + 1) * aw] for i in range(4))
    m0 = _pair_consts(8)
    for hp in range(ATT_W // LANES):
        ls = slice(hp * LANES, (hp + 1) * LANES)
        for idx, (src, nw) in enumerate(((q, qnw_ref), (kx, knw_ref))):
            x = _bcast8(src[:, ls])
            sq = x * x
            s0 = jnp.sum(jnp.where(m0, sq, 0.0), axis=-1, keepdims=True)
            s1 = jnp.sum(jnp.where(m0, 0.0, sq), axis=-1, keepdims=True)
            ms = jnp.where(m0, s0, s1) * (1.0 / HEAD)
            xn = x * lax.rsqrt(ms + NORM_EPS) * nw[...]
            qk_scr[bi, idx, :, ls] = _rope_pair(xn, cos_ref[...], sa_ref[...], sb_ref[...])
    yield
    qn = qk_scr[bi, 0, 0:1, :]
    kn = qk_scr[bi, 1, 0:1, :]
    scale = HEAD ** -0.5

    def heads4(row, gi):
        return jnp.concatenate([row[:, (gi * 4 + j) * HEAD:(gi * 4 + j + 1) * HEAD] for j in range(4)], axis=0)

    def to_col(row):
        return jnp.sum(jnp.where(eye, jnp.broadcast_to(row, (HEAD, HEAD)), 0.0), axis=-1, keepdims=True)

    def to_row(col):
        return jnp.sum(jnp.where(eye, jnp.broadcast_to(col, (HEAD, HEAD)), 0.0), axis=0, keepdims=True)

    caches = ((ck0, cv0, sk0, sv0), (ck1, cv1, sk1, sv1), (ck2, cv2, sk2, sv2))
    windows = []
    for gi, (ck, cv, sk, sv) in enumerate(caches):
        sk[bi, 0] = heads4(kn, gi)
        sv[bi, 0] = heads4(vx, gi)
        row_id = lax.broadcasted_iota(jnp.int32, (1, ck.shape[-1]), 1)
        windows.append((row_id % ATT_GROUPS[gi][1]) == 0)
    combos = [(gi, j) for gi in range(3) for j in range(4)]
    asl = [slice((gi * 4 + j) * HEAD, (gi * 4 + j + 1) * HEAD) for gi, j in combos]
    qh = [qn[:, hs] * scale for hs in asl]
    q_col = [to_col(x) for x in qh]
    s_new = [jnp.sum(kn[:, hs] * x, axis=-1, keepdims=True) for hs, x in zip(asl, qh)]
    yield
    s_all = [jnp.where(windows[gi], jnp.sum(caches[gi][0][bi, j] * qc, axis=0, keepdims=True), -jnp.inf)
             for (gi, j), qc in zip(combos, q_col)]
    m_all = [jnp.maximum(jnp.max(s, axis=-1, keepdims=True), sn_) for s, sn_ in zip(s_all, s_new)]
    yield
    p_all = [jnp.exp(s - m) for s, m in zip(s_all, m_all)]
    p_new = [jnp.exp(sn_ - m) for sn_, m in zip(s_new, m_all)]
    l_all = [jnp.sum(p, axis=-1, keepdims=True) + pn for p, pn in zip(p_all, p_new)]
    o_colv = [jnp.sum(caches[gi][1][bi, j] * p, axis=-1, keepdims=True) for (gi, j), p in zip(combos, p_all)]
    yield
    o_rows = [(to_row(oc) + pn * vx[:, hs]) / l for oc, pn, hs, l in zip(o_colv, p_new, asl, l_all)]
    lse = [m + jnp.log(l) for m, l in zip(m_all, l_all)]
    for j in range(4):
        mx = jnp.maximum(jnp.maximum(lse[j], lse[4 + j]), lse[8 + j])
        es = [jnp.exp(lse[gi * 4 + j] - mx) for gi in range(3)]
        inv = 1.0 / (es[0] + es[1] + es[2])
        for gi in range(3):
            ci = gi * 4 + j
            mix_ref[bi, :, RWKV_W + ci * HEAD:RWKV_W + (ci + 1) * HEAD] = (
                o_rows[ci] * (es[gi] * inv) * _silu(gx[:, asl[ci]]))

    u = zc[:, 0:POOL_W]
    gate = zc[:, POOL_W:]
    prev = pool_ref[bi]
    for gi, w in enumerate(POOL_WINDOWS):
        cs = slice(gi * LANES, (gi + 1) * LANES)
        ug = u[:, cs]
        sw = jnp.sum(prev[POOL_BUF - (w - 1):POOL_BUF, cs], axis=0, keepdims=True) + ug
        cnt = min(float(w), float(past_len) + 1.0)
        d = sw / cnt - ug
        dd = _dot(_bcast8(d).astype(bf16), pw_ref[gi].astype(bf16))[0:1] * psc_ref[:, cs]
        mix_ref[bi, :, RWKV_W + ATT_W + gi * LANES:RWKV_W + ATT_W + (gi + 1) * LANES] = dd * _silu(gate[:, cs])
    spool_ref[bi, 0:POOL_BUF - 1, :] = prev[1:POOL_BUF]
    spool_ref[bi, POOL_BUF - 1:POOL_BUF, :] = u


def _sample_step(zs, sh_all, wkv_all, pool_all, caches_all, layer, mu, rp, wup, aup, qnw, knw, cos, sa, sb,
                 pw, psc, *, past_len, nbs=2):
    nb = zs.shape[0]
    d_in = zs.shape[-1]
    d_mix = RWKV_W + ATT_W + POOL_W

    def full(shape):
        nd = len(shape)
        return pl.BlockSpec(shape, lambda b, nd=nd: (0,) * nd)

    assert nb % nbs == 0
    in_specs = [
        pl.BlockSpec((nbs, 1, d_in), lambda b: (b, 0, 0)),
        pl.BlockSpec((None, nbs, 1, RWKV_COLS), lambda b: (layer, b, 0, 0)),
        pl.BlockSpec((None, nbs, RWKV_HEADS, HEAD, HEAD), lambda b: (layer, b, 0, 0, 0)),
        pl.BlockSpec((None, nbs, POOL_BUF, POOL_W), lambda b: (layer, b, 0, 0)),
    ]
    in_specs += [pl.BlockSpec((None, nbs, 4, HEAD, c.shape[-1]), lambda b: (layer, b, 0, 0, 0))
                 for c in caches_all]

    def of_layer(x):
        nd = x.ndim - 1
        return pl.BlockSpec((None,) + x.shape[1:], lambda b, nd=nd: (layer,) + (0,) * nd)

    in_specs += [of_layer(mu), of_layer(rp), of_layer(wup), of_layer(aup), of_layer(qnw), of_layer(knw),
                 full(cos.shape), full(sa.shape), full(sb.shape), of_layer(pw), of_layer(psc)]
    out_specs = [
        pl.BlockSpec((nbs, 1, d_mix), lambda b: (b, 0, 0)),
        pl.BlockSpec((nbs, RWKV_HEADS, HEAD, HEAD), lambda b: (b, 0, 0, 0)),
        pl.BlockSpec((nbs, POOL_BUF, POOL_W), lambda b: (b, 0, 0)),
    ] + [pl.BlockSpec((nbs, 1, 4, HEAD), lambda b: (b, 0, 0, 0))] * 6
    out_shape = [
        jax.ShapeDtypeStruct((nb, 1, d_mix), f32),
        jax.ShapeDtypeStruct((nb, RWKV_HEADS, HEAD, HEAD), f32),
        jax.ShapeDtypeStruct((nb, POOL_BUF, POOL_W), f32),
    ] + [jax.ShapeDtypeStruct((nb, 1, 4, HEAD), f32)] * 6
    return pl.pallas_call(
        functools.partial(_sample_body, past_len=past_len, nbs=nbs),
        grid=(nb // nbs,),
        in_specs=in_specs,
        out_specs=out_specs,
        out_shape=out_shape,
        scratch_shapes=[pltpu.VMEM((nbs, 2, 8, ATT_W), f32)],
        compiler_params=pltpu.CompilerParams(
            dimension_semantics=("arbitrary",), vmem_limit_bytes=VMEM_LIMIT),
        name="sample_step",
    )(zs, sh_all, wkv_all, pool_all, *caches_all, mu, rp, wup, aup, qnw, knw, cos, sa, sb, pw, psc)


def _rope_tables(pos):
    half = ROPE_DIMS // 2
    inv = jnp.power(jnp.float32(ROPE_THETA), -jnp.arange(half, dtype=f32) * 2.0 / ROPE_DIMS)
    ang = pos[:, None] * inv[None, :]
    cos, sin = jnp.cos(ang), jnp.sin(ang)
    n = pos.shape[0]
    pad = jnp.zeros((n, HEAD - ROPE_DIMS), f32)
    zero = jnp.zeros((n, half), f32)
    c_head = jnp.concatenate([cos, cos, pad + 1.0], axis=1)
    a_head = jnp.concatenate([-sin, zero, pad], axis=1)
    b_head = jnp.concatenate([zero, sin, pad], axis=1)
    return tuple(jnp.concatenate([x, x], axis=1) for x in (c_head, a_head, b_head))


def kernel(x_prompt, x_sample, state_wkv, state_shift, state_pool,
           cache_k_w128, cache_v_w128, cache_k_w512, cache_v_w512, cache_k_w2048, cache_v_w2048,
           norm_w, w_in, w_out, rwkv_mu, rwkv_w0, rwkv_w_up, rwkv_a0, rwkv_a_up,
           rwkv_k_k, rwkv_k_a, rwkv_r_k, rwkv_ln_w, rwkv_ln_b, q_norm_w, k_norm_w, pool_w, pool_scale):
    b, t, d = x_prompt.shape
    nb, ts, _ = x_sample.shape
    depth = w_in.shape[0]
    d_in = w_in.shape[2]
    assert ts == 1 and t % 512 == 0
    past_len = 16384
    caches_in = ((cache_k_w128, cache_v_w128), (cache_k_w512, cache_v_w512), (cache_k_w2048, cache_v_w2048))
    for (window, _), (ck, _) in zip(ATT_GROUPS, caches_in):
        assert ck.shape[2] == window, "window buffers are expected to be full"

    cos_p, sa_p, sb_p = _rope_tables(jnp.arange(t, dtype=f32))
    cos_s, sa_s, sb_s = _rope_tables(past_len + jnp.arange(1, dtype=f32))

    hp = x_prompt.reshape(b * t, d)
    hs = x_sample.reshape(nb, d)
    npair = RWKV_HEADS // 2
    p_out = [[] for _ in range(9)]
    s_out = [[] for _ in range(9)]
    sh_all = state_shift.reshape(depth, nb, 1, RWKV_COLS)
    caches_all = [jnp.transpose(c, (0, 1, 3, 4, 2)) for pair in caches_in for c in pair]

    nw_all = norm_w.reshape(depth, 1, d)
    mu_all = rwkv_mu.reshape(depth, 1, RWKV_COLS)
    mu4 = rwkv_mu[:, :4 * RWKV_W].reshape(depth, 4, npair, LANES).transpose(0, 2, 1, 3)
    mul = jnp.broadcast_to(rwkv_mu[:, 4 * RWKV_W:].reshape(depth, 1, 1, LANES), (depth, npair, 1, LANES))
    vecs_all = jnp.stack([rwkv_w0, rwkv_a0, rwkv_k_k, rwkv_k_a, rwkv_r_k.reshape(depth, -1),
                          rwkv_ln_w, rwkv_ln_b], axis=1)
    pp_all = jnp.concatenate(
        [mu4, mul, vecs_all.reshape(depth, 7, npair, LANES).transpose(0, 2, 1, 3)], axis=2)
    wup = rwkv_w_up.reshape(depth, LORA, npair, LANES).transpose(0, 2, 1, 3)
    aup = rwkv_a_up.reshape(depth, LORA, npair, LANES).transpose(0, 2, 1, 3)
    zl = jnp.zeros_like(wup)
    wl_all = jnp.concatenate(
        [jnp.concatenate([wup, zl], axis=3), jnp.concatenate([zl, aup], axis=3)], axis=2)
    qnw_all = jnp.tile(q_norm_w, (1, 2)).reshape(depth, 1, LANES)
    knw_all = jnp.tile(k_norm_w, (1, 2)).reshape(depth, 1, LANES)
    psc_all = pool_scale.reshape(depth, 1, POOL_W)
    sin_p = sa_p + sb_p

    for l in range(depth):
        zp, zs = _inproj(hp, hs, nw_all, w_in, l)
        z3 = zp.reshape(d_in // LANES, b, t, LANES)

        ya, p_wkv = _rwkv_prompt(z3, pp_all, wl_all, l)
        att = _attn_prompt(z3, cos_p, sin_p, qnw_all, knw_all, l)
        yc = _pool_prompt(z3, pool_w, psc_all, l)

        sample = _sample_step(
            zs.reshape(nb, 1, -1), sh_all, state_wkv, state_pool, caches_all, l,
            mu_all, vecs_all, rwkv_w_up, rwkv_a_up, qnw_all, knw_all, cos_s, sa_s, sb_s,
            pool_w, psc_all, past_len=past_len)
        mix_s = sample[0].reshape(nb, -1)

        ys = [ya.reshape(b * t, RWKV_W)] + [y.reshape(b * t, 4 * HEAD) for y in att[:3]] + [yc.reshape(b * t, POOL_W)]
        hp, hs = _outproj(hp, ys, hs, mix_s, w_out, l)

        p_out[0].append(p_wkv)
        p_out[1].append(z3[:ATT_BLK0, :, -1, :].transpose(1, 0, 2).reshape(b, RWKV_COLS))
        p_out[2].append(z3[POOL_BLK0:POOL_BLK0 + POOL_W // LANES, :, t - POOL_BUF:, :]
                        .transpose(1, 2, 0, 3).reshape(b, POOL_BUF, POOL_W))
        for gi in range(3):
            keep = att[3 + 2 * gi].shape[1]
            p_out[3 + 2 * gi].append(att[3 + 2 * gi].reshape(b, keep, 4, HEAD))
            p_out[4 + 2 * gi].append(att[4 + 2 * gi].reshape(b, keep, 4, HEAD))
        s_out[0].append(sample[1])
        s_out[1].append(zs[:, :RWKV_COLS])
        s_out[2].append(sample[2])
        for i in range(6):
            s_out[3 + i].append(sample[3 + i])

    return (hp.reshape(b, t, d), hs.reshape(nb, 1, d),
            *[jnp.stack(x) for x in p_out], *[jnp.stack(x) for x in s_out])
```

```python
import functools

import jax
import jax.numpy as jnp
from jax import lax
from jax.experimental import pallas as pl
from jax.experimental.pallas import tpu as pltpu

f32 = jnp.float32
bf16 = jnp.bfloat16

HEAD = 64
LANES = 128
NORM_EPS = 1e-6
GN_EPS = 64e-5
RWKV_HEADS = 12
RWKV_W = RWKV_HEADS * HEAD
LORA = 64
RWKV_COLS = 4 * RWKV_W + 2 * LORA
ATT_GROUPS = ((128, 1), (512, 4), (2048, 16))
ATT_W = 12 * HEAD
ATT_COLS = 4 * ATT_W
ROPE_THETA = 500000.0
ROPE_DIMS = 16
QBLK = 128
POOL_WINDOWS = (2, 4, 8, 16)
POOL_W = 512
POOL_BUF = 15
CHUNK = 64
VMEM_LIMIT = 56 * 1024 * 1024

ATT_BLK0 = RWKV_COLS // LANES
POOL_BLK0 = (RWKV_COLS + ATT_COLS) // LANES


def _dot(a, b):
    return jnp.dot(a, b, preferred_element_type=f32)


def _dot_nt(a, b):
    return lax.dot_general(a, b, (((1,), (1,)), ((), ())), preferred_element_type=f32)


def _dot_tn(a, b):
    return lax.dot_general(a, b, (((0,), (0,)), ((), ())), preferred_element_type=f32)


def _split2(x):
    hi = x.astype(bf16)
    lo = (x - hi.astype(f32)).astype(bf16)
    return hi, lo


def _split3(x):
    hi = x.astype(bf16)
    r1 = x - hi.astype(f32)
    mid = r1.astype(bf16)
    lo = (r1 - mid.astype(f32)).astype(bf16)
    return hi, mid, lo


def _silu(x):
    return x * jax.nn.sigmoid(x)


def _rms(x, w):
    ms = jnp.mean(x * x, axis=-1, keepdims=True)
    return x * lax.rsqrt(ms + NORM_EPS) * w


def _pair_consts(n):
    lane = lax.broadcasted_iota(jnp.int32, (n, LANES), 1)
    return lane < HEAD


def _ones_blockdiag():
    r = lax.broadcasted_iota(jnp.int32, (LANES, LANES), 0)
    c = lax.broadcasted_iota(jnp.int32, (LANES, LANES), 1)
    return jnp.where((r < HEAD) == (c < HEAD), 1.0, 0.0).astype(bf16)


def _segsum(x, ones_bd):
    hi, lo = _split2(x)
    return _dot(jnp.concatenate([hi, lo], axis=1), jnp.concatenate([ones_bd, ones_bd], axis=0))


def _inproj_body(xp_ref, xs_ref, nw_ref, w_ref, zp_ref, zs_ref, h_scr, *, tn, n):
    i = pl.program_id(0)
    j = pl.program_id(1)

    @pl.when(j == 0)
    def _():
        h_scr[...] = _rms(xp_ref[...], nw_ref[...]).astype(bf16)

    nj = pl.num_programs(1)
    tail = n - (pl.cdiv(n, tn) - 1) * tn

    def project(width):
        w = w_ref[:, 0:width].astype(bf16)
        acc = _dot(h_scr[...], w)
        for c in range(width // LANES):
            zp_ref[c] = acc[:, c * LANES:(c + 1) * LANES]

        @pl.when(i == 0)
        def _():
            hs = _rms(xs_ref[...], nw_ref[...]).astype(bf16)
            col = pl.multiple_of(j * tn, LANES)
            zs_ref[:, pl.ds(col, width)] = _dot(hs, w)

    @pl.when(j < nj - 1)
    def _():
        project(tn)

    @pl.when(j == nj - 1)
    def _():
        project(tail)


def _inproj(xp, xs, nw, w_all, layer, *, tm=2048, tn=512):
    m, d = xp.shape
    n = w_all.shape[2]
    nb = xs.shape[0]
    nj = pl.cdiv(n, tn)
    assert n % LANES == 0 and tn % LANES == 0
    return pl.pallas_call(
        functools.partial(_inproj_body, tn=tn, n=n),
        grid=(m // tm, nj),
        in_specs=[
            pl.BlockSpec((tm, d), lambda i, j: (i, 0), pipeline_mode=pl.Buffered(1)),
            pl.BlockSpec((nb, d), lambda i, j: (0, 0)),
            pl.BlockSpec((None, 1, d), lambda i, j: (layer, 0, 0)),
            pl.BlockSpec((None, d, tn), lambda i, j: (layer, 0, j)),
        ],
        out_specs=[
            pl.BlockSpec((tn // LANES, tm, LANES), lambda i, j: (j, i, 0)),
            pl.BlockSpec((nb, n), lambda i, j: (0, 0)),
        ],
        out_shape=[jax.ShapeDtypeStruct((n // LANES, m, LANES), f32),
                   jax.ShapeDtypeStruct((nb, n), f32)],
        scratch_shapes=[pltpu.VMEM((tm, d), bf16)],
        compiler_params=pltpu.CompilerParams(
            dimension_semantics=("arbitrary", "arbitrary"), vmem_limit_bytes=VMEM_LIMIT),
        name="inproj",
    )(xp, xs, nw, w_all)


def _outproj_body(*refs, tn, widths):
    n = len(widths)
    xp_ref = refs[0]
    y_refs = refs[1:1 + n]
    xs_ref, ms_ref, w_ref, op_ref, os_ref, w_scr = refs[1 + n:]
    i = pl.program_id(0)
    j = pl.program_id(1)
    col = pl.multiple_of(j * tn, LANES)

    @pl.when(i == 0)
    def _():
        w_scr[:, pl.ds(col, tn)] = w_ref[...].astype(bf16)

    w = w_scr[:, pl.ds(col, tn)]
    acc = None
    row = 0
    for y_ref, width in zip(y_refs, widths):
        part = _dot(y_ref[...], w[row:row + width])
        acc = part if acc is None else acc + part
        row += width
    op_ref[...] = xp_ref[...] + acc

    @pl.when(i == 0)
    def _():
        os_ref[:, pl.ds(col, tn)] = xs_ref[:, pl.ds(col, tn)] + _dot(ms_ref[...].astype(bf16), w)


def _outproj(xp, ys, xs, ms, w_all, layer, *, tm=2048, tn=512):
    m, d = xp.shape
    nb = xs.shape[0]
    dm = w_all.shape[1]
    widths = tuple(y.shape[1] for y in ys)
    assert sum(widths) == dm
    nj = d // tn
    return pl.pallas_call(
        functools.partial(_outproj_body, tn=tn, widths=widths),
        grid=(m // tm, d // tn),
        in_specs=[pl.BlockSpec((tm, tn), lambda i, j: (i, j))]
        + [pl.BlockSpec((tm, width), lambda i, j: (i, 0)) for width in widths]
        + [
            pl.BlockSpec((nb, d), lambda i, j: (0, 0)),
            pl.BlockSpec((nb, dm), lambda i, j: (0, 0)),
            pl.BlockSpec((None, dm, tn), lambda i, j: (layer, 0, jnp.where(i == 0, j, nj - 1))),
        ],
        out_specs=[
            pl.BlockSpec((tm, tn), lambda i, j: (i, j)),
            pl.BlockSpec((nb, d), lambda i, j: (0, 0)),
        ],
        out_shape=[jax.ShapeDtypeStruct((m, d), f32), jax.ShapeDtypeStruct((nb, d), f32)],
        scratch_shapes=[pltpu.VMEM((dm, d), bf16)],
        compiler_params=pltpu.CompilerParams(
            dimension_semantics=("arbitrary", "arbitrary"), vmem_limit_bytes=VMEM_LIMIT),
        name="outproj",
    )(xp, *ys, xs, ms, w_all)


_DECAY_SCALE = 0.6065306597126334

_P_MU, _P_W0, _P_A0, _P_KK, _P_KA, _P_RK, _P_LNW, _P_LNB = 0, 5, 6, 7, 8, 9, 10, 11


def _rwkv_chunk_stages(at, bt, kt, rt, v, wcs, consts):
    m0, gmask, bdmask, eye = consts
    m0w = jnp.concatenate([m0, m0], axis=1)
    eye_pair = jnp.where(lax.broadcasted_iota(jnp.int32, (CHUNK, LANES), 1) % CHUNK
                         == lax.broadcasted_iota(jnp.int32, (CHUNK, LANES), 0), 1.0, 0.0)
    zb64 = jnp.zeros((CHUNK, LANES), bf16)
    n = len(wcs)

    def rows(x, c):
        return x[c * CHUNK:(c + 1) * CHUNK]

    def split_heads(xb, m):
        zero = jnp.zeros_like(xb)
        return jnp.concatenate([jnp.where(m, xb, zero), jnp.where(m, zero, xb)], axis=0)

    g, p, x, mc, nc, rp, ov = {}, {}, {}, {}, {}, {}, {}

    def stage_scores(chs):
        for c in chs:
            lhs = jnp.concatenate([rows(at, c), rows(rt, c)], axis=0).astype(bf16)
            rhs = jnp.concatenate([split_heads(rows(bt, c).astype(bf16), m0),
                                   split_heads(rows(kt, c).astype(bf16), m0)], axis=0)
            g[c] = jnp.where(gmask, _dot_nt(lhs, rhs), 0.0)
            p[c] = g[c][0:CHUNK, 0:LANES]

    av, tinv = {}, {}

    def stage_level(chs, lev):
        for c in chs:
            pb = p[c].astype(bf16)
            if lev == 0:
                vs = split_heads(rows(v, c).astype(bf16), m0)
                av[c] = _dot(g[c][0:CHUNK, LANES:].astype(bf16), vs)
                tinv[c] = eye_pair + p[c]
                p[c] = _dot(pb, split_heads(pb, m0))
            elif lev < 5:
                tt_ = split_heads(tinv[c].astype(bf16), m0)
                out = _dot(pb, jnp.concatenate([split_heads(pb, m0), tt_], axis=1))
                p[c] = out[:, 0:LANES]
                tinv[c] = tinv[c] + out[:, LANES:]
            else:
                tinv[c] = tinv[c] + _dot(pb, split_heads(tinv[c].astype(bf16), m0))

    def stage_apply(chs):
        for c in chs:
            rhs = split_heads(jnp.concatenate([av[c], rows(at, c)], axis=1).astype(bf16), m0w)
            x[c] = _dot(tinv[c].astype(bf16), rhs)

    def stage_fold(chs):
        for c in chs:
            xb = x[c].astype(bf16)
            uv = xb[:, 0:LANES]
            ap = xb[:, LANES:]
            vb = rows(v, c).astype(bf16)
            tnl = jnp.concatenate([rows(bt, c) * wcs[c], rows(kt, c) * wcs[c]], axis=0).astype(bf16)
            tnr = jnp.concatenate(
                [jnp.concatenate([ap, uv], axis=1), jnp.concatenate([zb64, vb], axis=1)], axis=0)
            mn = jnp.where(bdmask, _dot_tn(tnl, tnr), 0.0)
            mc[c] = mn[:, 0:LANES] + jnp.where(eye, jnp.broadcast_to(wcs[c], (LANES, LANES)), 0.0)
            nc[c] = mn[:, LANES:]
            l2 = g[c][CHUNK:].astype(bf16)
            r2 = jnp.concatenate([
                split_heads(jnp.concatenate([ap, uv], axis=1), m0w),
                split_heads(jnp.concatenate([zb64, vb], axis=1), m0w)], axis=0)
            ro = _dot(l2, r2)
            rp[c] = rows(rt, c) + ro[:, 0:LANES]
            ov[c] = ro[:, LANES:]

    def carry(c, st):
        seq = _dot(jnp.concatenate([rp[c], mc[c]], axis=0).astype(bf16), st.astype(bf16))
        return seq[0:CHUNK] + ov[c], seq[CHUNK:] + nc[c]

    chs = list(range(n))
    stages = ([functools.partial(stage_scores, chs)]
              + [functools.partial(stage_level, chs, lev) for lev in range(6)]
              + [functools.partial(stage_apply, chs), functools.partial(stage_fold, chs)])
    return stages, carry


def _rwkv_body(zr_ref, zk_ref, zv_ref, zg_ref, zl_ref, pp_ref, wl_ref, y_ref, s_ref,
               sh_scr, st_scr, *, tt, npr, lock):
    tb = pl.program_id(2)
    nt = pl.num_programs(2)
    nch = tt // CHUNK
    nsh = 4 * npr + 1

    @pl.when(tb == 0)
    def _():
        st_scr[...] = jnp.zeros_like(st_scr)
        for i in range(nsh):
            sh_scr[i, 7:8, :] = jnp.zeros((1, LANES), f32)

    @pl.when(tb > 0)
    def _():
        for i in range(nsh):
            sh_scr[i, 7:8, :] = sh_scr[i, 7 + tt:8 + tt, :]

    def shifted(idx, z, mu):
        sh_scr[idx, 8:8 + tt, :] = z
        zprev = sh_scr[idx, pl.ds(7, tt), :]
        return z + (zprev - z) * mu

    m0t = _pair_consts(tt)
    ones_bd = _ones_blockdiag()
    lo = shifted(4 * npr, zl_ref[0, 0], pp_ref[0, _P_MU + 4:_P_MU + 5])
    xl = jnp.where(m0t, jnp.tanh(lo), lo).astype(bf16)

    m0 = _pair_consts(CHUNK)
    gr = lax.broadcasted_iota(jnp.int32, (2 * CHUNK, 2 * LANES), 0)
    gc = lax.broadcasted_iota(jnp.int32, (2 * CHUNK, 2 * LANES), 1)
    gt = gr % CHUNK
    gs = gc % CHUNK
    gmask = (gt > gs) | ((gr >= CHUNK) & (gt == gs))
    br = lax.broadcasted_iota(jnp.int32, (LANES, 2 * LANES), 0)
    bc = lax.broadcasted_iota(jnp.int32, (LANES, 2 * LANES), 1)
    bdmask = (br < HEAD) == ((bc % LANES) < HEAD)
    er = lax.broadcasted_iota(jnp.int32, (LANES, LANES), 0)
    ec = lax.broadcasted_iota(jnp.int32, (LANES, LANES), 1)
    eye = er == ec
    consts = (m0, gmask, bdmask, eye)
    tr = lax.broadcasted_iota(jnp.int32, (CHUNK, CHUNK), 0)
    tc = lax.broadcasted_iota(jnp.int32, (CHUNK, CHUNK), 1)
    tril = jnp.where(tc <= tr, 1.0, 0.0).astype(bf16)
    tril3 = jnp.concatenate([tril, tril, tril], axis=1)

    def prepare(s, out):
        pp = pp_ref[s]
        r, k, v, g = [shifted(4 * s + i, ref[s, 0], pp[_P_MU + i:_P_MU + i + 1])
                      for i, ref in enumerate((zr_ref, zk_ref, zv_ref, zg_ref))]
        la = _dot(xl, wl_ref[s].astype(bf16))
        kk = k * pp[_P_KK:_P_KK + 1]
        kk_ss = _segsum(kk * kk, ones_bd)
        yield
        ld = -_DECAY_SCALE * jax.nn.sigmoid(pp[_P_W0:_P_W0 + 1] + la[:, 0:LANES])
        a = jax.nn.sigmoid(pp[_P_A0:_P_A0 + 1] + la[:, LANES:])
        kk = kk * jnp.minimum(lax.rsqrt(kk_ss), 1e12)
        k2 = k * (1.0 + (a - 1.0) * pp[_P_KA:_P_KA + 1])
        h3 = _split3(ld)
        cl = jnp.concatenate([
            _dot(tril3, jnp.concatenate([h[c * CHUNK:(c + 1) * CHUNK] for h in h3], axis=0))
            for c in range(nch)], axis=0)
        bonus = _segsum(r * k2 * pp[_P_RK:_P_RK + 1], ones_bd) * v
        yield
        e_in = jnp.exp(cl)
        e_neg = jnp.exp(-cl)
        wcs = [e_in[(c + 1) * CHUNK - 1:(c + 1) * CHUNK] for c in range(nch)]
        out["stages"], out["carry"] = _rwkv_chunk_stages(
            -kk * jnp.exp(cl - ld), kk * a * e_neg, k2 * e_neg, r * e_in, v, wcs, consts)
        out["epi"] = (pp, g, bonus)

    def finish(s, outs, st, epi):
        pp, g, bonus = epi
        st_scr[s] = st
        o = jnp.concatenate(outs, axis=0)
        mean = _segsum(o, ones_bd) * (1.0 / HEAD)
        yield
        dl = o - mean
        var = _segsum(dl * dl, ones_bd) * (1.0 / HEAD)
        yield
        on = dl * lax.rsqrt(var + GN_EPS) * pp[_P_LNW:_P_LNW + 1] + pp[_P_LNB:_P_LNB + 1] + bonus
        y_ref[0, :, s * LANES:(s + 1) * LANES] = (on * _silu(g)).astype(bf16)

    def drain(gen):
        for _ in gen:
            pass

    def lockstep(gens):
        gens = list(gens)
        while gens:
            gens = [g for g in gens if next(g, StopIteration) is not StopIteration]
            yield

    preps = [dict() for _ in range(npr)]
    units = [list(range(i, min(i + lock, npr))) for i in range(0, npr, lock)]
    drain(lockstep(prepare(s, preps[s]) for s in units[0]))
    prev = None
    closing = iter(())
    for ui, unit in enumerate(units):
        nxt = (lockstep(prepare(s, preps[s]) for s in units[ui + 1]) if ui + 1 < len(units) else iter(()))
        for stage_row in zip(*[preps[s]["stages"] for s in unit]):
            for stage in stage_row:
                stage()
            if prev is not None:
                for cur in prev:
                    if cur["todo"]:
                        cur["step"]()
            next(closing, None)
            next(nxt, None)
        drain(closing)
        drain(nxt)
        if prev is not None:
            while any(cur["todo"] for cur in prev):
                for cur in prev:
                    if cur["todo"]:
                        cur["step"]()
            closing = lockstep(finish(cur["s"], cur["outs"], cur["st"][0], cur["epi"]) for cur in prev)
        prev = []
        for s in unit:
            cur = {"s": s, "outs": [], "st": [st_scr[s]], "epi": preps[s]["epi"], "todo": list(range(nch))}

            def step(cur=cur, carry=preps[s]["carry"]):
                o_c, cur["st"][0] = carry(cur["todo"].pop(0), cur["st"][0])
                cur["outs"].append(o_c)

            cur["step"] = step
            prev.append(cur)
    while any(cur["todo"] for cur in prev):
        for cur in prev:
            if cur["todo"]:
                cur["step"]()
        next(closing, None)
    drain(closing)
    drain(lockstep(finish(cur["s"], cur["outs"], cur["st"][0], cur["epi"]) for cur in prev))

    @pl.when(tb == nt - 1)
    def _():
        for s in range(npr):
            stt = st_scr[s].T
            s_ref[0, 2 * s] = stt[0:HEAD, 0:HEAD]
            s_ref[0, 2 * s + 1] = stt[HEAD:, HEAD:]


def _rwkv_prompt(z3, pp, wl, layer, *, tt=512, npr=6, lock=2):
    _, b, t, _ = z3.shape
    npair = RWKV_HEADS // 2
    ngrp = npair // npr
    wide = npr * LANES

    def zspec(part):
        return pl.BlockSpec((npr, 1, tt, LANES), lambda bi, p, tb, part=part: (part * ngrp + p, bi, tb, 0))

    lora_blk = 4 * RWKV_W // LANES
    return pl.pallas_call(
        functools.partial(_rwkv_body, tt=tt, npr=npr, lock=lock),
        grid=(b, ngrp, t // tt),
        in_specs=[
            zspec(0), zspec(1), zspec(2), zspec(3),
            pl.BlockSpec((1, 1, tt, LANES), lambda bi, p, tb: (lora_blk, bi, tb, 0)),
            pl.BlockSpec((None, npr, 12, LANES), lambda bi, p, tb: (layer, p, 0, 0)),
            pl.BlockSpec((None, npr, LANES, 2 * LANES), lambda bi, p, tb: (layer, p, 0, 0)),
        ],
        out_specs=[
            pl.BlockSpec((1, tt, wide), lambda bi, p, tb: (bi, tb, p)),
            pl.BlockSpec((1, 2 * npr, HEAD, HEAD), lambda bi, p, tb: (bi, p, 0, 0)),
        ],
        out_shape=[jax.ShapeDtypeStruct((b, t, RWKV_W), bf16),
                   jax.ShapeDtypeStruct((b, RWKV_HEADS, HEAD, HEAD), f32)],
        scratch_shapes=[pltpu.VMEM((4 * npr + 1, tt + 8, LANES), f32), pltpu.VMEM((npr, LANES, LANES), f32)],
        compiler_params=pltpu.CompilerParams(
            dimension_semantics=("arbitrary", "arbitrary", "arbitrary"), vmem_limit_bytes=VMEM_LIMIT),
        name="rwkv_prompt",
    )(z3, z3, z3, z3, z3, pp, wl)


def _rope_pair(x, cos, sa, sb):
    return x * cos + pltpu.roll(x, LANES - ROPE_DIMS // 2, 1) * sa + pltpu.roll(x, ROPE_DIMS // 2, 1) * sb


def _attn_body(*refs, t):
    qkvg = [[r.at[0] for r in refs[4 * gi:4 * gi + 4]] for gi in range(3)]
    cos_ref, sin_ref, qnw_ref, knw_ref = refs[12:16]
    y_refs = refs[16:19]
    kv_refs = [refs[19 + 2 * gi:21 + 2 * gi] for gi in range(3)]
    qn_scr, kn_scr, o_scr, lse_scr = refs[25:29]

    rb = 256
    ones_bd = _ones_blockdiag()
    m0q = _pair_consts(QBLK)
    scale = HEAD ** -0.5
    half = ROPE_DIMS // 2
    pj = lax.broadcasted_iota(jnp.int32, (LANES, LANES), 0)
    pi = lax.broadcasted_iota(jnp.int32, (LANES, LANES), 1)
    pin = pi % HEAD
    perm = jnp.where(((pin < half) & (pj == pi + half)) | ((pin >= half) & (pin < ROPE_DIMS) & (pj == pi - half)),
                     1.0, 0.0).astype(bf16)
    zpad = jnp.zeros((LANES, LANES), bf16)
    sum_swap = jnp.concatenate([jnp.concatenate([ones_bd, zpad], axis=1),
                                jnp.concatenate([zpad, perm], axis=1)], axis=0)

    for gi, (window, dil) in enumerate(ATT_GROUPS):
        q_ref, k_ref, v_ref, g_ref = qkvg[gi]
        pk_ref, pv_ref = kv_refs[gi]
        keep = min(window, t)

        def norm_rows(i, carry, q_ref=q_ref, k_ref=k_ref):
            jobs = []
            for u in range(2):
                rows = pl.ds(pl.multiple_of((2 * i + u) * rb, rb), rb)
                for src, nw, dst in ((q_ref, qnw_ref, qn_scr), (k_ref, knw_ref, kn_scr)):
                    x = src[0, rows, :]
                    y = x * nw[...]
                    hi, lo = _split2(jnp.concatenate([x * x, y], axis=1))
                    jobs.append((y, _dot(hi, sum_swap) + _dot(lo, sum_swap), dst, rows))
            for y, res, dst, rows in jobs:
                rs = lax.rsqrt(res[:, 0:LANES] * (1.0 / HEAD) + NORM_EPS)
                dst[rows, :] = (y * cos_ref[rows, :] + res[:, LANES:] * sin_ref[rows, :]) * rs
            return carry

        lax.fori_loop(0, t // (2 * rb), norm_rows, 0)
        pk_ref[0] = kn_scr[t - keep:t, :]
        pv_ref[0] = v_ref[0, t - keep:t, :]

        n_sub = t // dil
        n_blk = n_sub // QBLK

        def tiles(starts, nk, first, v_ref=v_ref, dil=dil, gi=gi):
            qi = lax.broadcasted_iota(jnp.int32, (QBLK, nk), 0)
            kj = lax.broadcasted_iota(jnp.int32, (QBLK, nk), 1)
            mask = (kj <= qi) if first else ((kj >= qi) & (kj <= qi + QBLK))
            scores, vts = [], []
            for q0, k0 in starts:
                qt = qn_scr[pl.ds(q0, QBLK, stride=dil), :] * scale
                kt = kn_scr[pl.ds(k0, nk, stride=dil), :].astype(bf16)
                vts.append(v_ref[0, pl.ds(k0, nk, stride=dil), :].astype(bf16))
                for hh in range(2):
                    qh = jnp.where(m0q, qt, 0.0) if hh == 0 else jnp.where(m0q, 0.0, qt)
                    scores.append(_dot_nt(qh.astype(bf16), kt))
            probs, sums, lses = [], [], []
            for s in scores:
                s = jnp.where(mask, s, -jnp.inf)
                m = jnp.max(s, axis=-1, keepdims=True)
                p = jnp.exp(s - m)
                l = jnp.sum(p, axis=-1, keepdims=True)
                probs.append(p.astype(bf16))
                sums.append(l)
                lses.append(m + jnp.log(l))
            for ti, (q0, _) in enumerate(starts):
                oh = [_dot(probs[2 * ti + hh], vts[ti]) / sums[2 * ti + hh] for hh in range(2)]
                o_scr[gi, pl.ds(q0, QBLK, stride=dil), :] = jnp.where(m0q, oh[0], oh[1])
                lse_scr[gi, pl.ds(q0, QBLK, stride=dil), :] = jnp.where(
                    m0q, jnp.broadcast_to(lses[2 * ti], (QBLK, LANES)),
                    jnp.broadcast_to(lses[2 * ti + 1], (QBLK, LANES)))

        per_first = min(8, dil)

        def first_tiles(i, carry, tiles=tiles, per=per_first):
            tiles([(i * per + u, i * per + u) for u in range(per)], QBLK, True)
            return carry

        lax.fori_loop(0, dil // per_first, first_tiles, 0)

        if n_blk > 1:
            n_later = dil * (n_blk - 1)
            per_later = max(p for p in range(1, 7) if n_later % p == 0)

            def later_tiles(i, carry, tiles=tiles, dil=dil, per=per_later):
                starts = []
                for u in range(per):
                    idx = i * per + u
                    r = idx % dil
                    blk = idx // dil + 1
                    starts.append((r + blk * QBLK * dil, r + (blk - 1) * QBLK * dil))
                tiles(starts, 2 * QBLK, False)
                return carry

            lax.fori_loop(0, dil * (n_blk - 1) // per_later, later_tiles, 0)

    def combine(i, carry):
        rows = pl.ds(pl.multiple_of(i * rb, rb), rb)
        ls = [lse_scr[gi, rows, :] for gi in range(3)]
        mx = jnp.maximum(jnp.maximum(ls[0], ls[1]), ls[2])
        es = [jnp.exp(l - mx) for l in ls]
        inv = 1.0 / (es[0] + es[1] + es[2])
        for gi in range(3):
            gate = qkvg[gi][3][0, rows, :]
            y_refs[gi][0, rows, :] = (o_scr[gi, rows, :] * (es[gi] * inv) * _silu(gate)).astype(bf16)
        return carry

    lax.fori_loop(0, t // rb, combine, 0)


def _attn_prompt(z3, cos, sin, qnw, knw, layer):
    _, b, t, _ = z3.shape
    in_specs = []
    for gi in range(3):
        for part in range(4):
            off = ATT_BLK0 + part * 6 + 2 * gi
            in_specs.append(pl.BlockSpec((1, 1, t, LANES), lambda bi, jp, off=off: (off + jp, bi, 0, 0)))
    in_specs += [pl.BlockSpec((t, LANES), lambda bi, jp: (0, 0))] * 2
    in_specs += [pl.BlockSpec((None, 1, LANES), lambda bi, jp: (layer, 0, 0))] * 2
    out_specs = [pl.BlockSpec((1, t, LANES), lambda bi, jp: (bi, 0, jp))] * 3
    out_shape = [jax.ShapeDtypeStruct((b, t, 4 * HEAD), bf16)] * 3
    for window, _ in ATT_GROUPS:
        keep = min(window, t)
        out_specs += [pl.BlockSpec((1, keep, LANES), lambda bi, jp: (bi, 0, jp))] * 2
        out_shape += [jax.ShapeDtypeStruct((b, keep, 4 * HEAD), f32)] * 2
    outs = pl.pallas_call(
        functools.partial(_attn_body, t=t),
        grid=(b, 2),
        in_specs=in_specs,
        out_specs=out_specs,
        out_shape=out_shape,
        scratch_shapes=[pltpu.VMEM((t, LANES), f32), pltpu.VMEM((t, LANES), f32),
                        pltpu.VMEM((3, t, LANES), f32), pltpu.VMEM((3, t, LANES), f32)],
        compiler_params=pltpu.CompilerParams(
            dimension_semantics=("arbitrary", "arbitrary"), vmem_limit_bytes=VMEM_LIMIT),
        name="attn_prompt",
    )(*([z3] * 12), cos, sin, qnw, knw)
    return outs


def _pool_body(*refs, tt):
    ng = len(POOL_WINDOWS)
    u_refs = [r.at[0] for r in refs[0:ng]]
    g_refs = [r.at[0] for r in refs[ng:2 * ng]]
    w_ref, sc_ref, y_ref, scr = refs[2 * ng:]
    tb = pl.program_id(1)
    pad = max(POOL_WINDOWS)

    @pl.when(tb == 0)
    def _():
        for gi in range(ng):
            scr[gi, 0:pad, :] = jnp.zeros((pad, LANES), f32)

    @pl.when(tb > 0)
    def _():
        for gi in range(ng):
            scr[gi, 0:pad, :] = scr[gi, tt:tt + pad, :]

    pos = (tb * tt + lax.broadcasted_iota(jnp.int32, (tt, 1), 0)).astype(f32)
    for gi, w in enumerate(POOL_WINDOWS):
        u = u_refs[gi][0]
        scr[gi, pad:pad + tt, :] = u
        acc = u
        for k in range(1, w):
            acc = acc + scr[gi, pl.ds(pad - k, tt), :]
        d = acc / jnp.minimum(float(w), pos + 1.0) - u
        dd = _dot(d.astype(bf16), w_ref[gi].astype(bf16)) * sc_ref[:, gi * LANES:(gi + 1) * LANES]
        y_ref[0, :, gi * LANES:(gi + 1) * LANES] = (dd * _silu(g_refs[gi][0])).astype(bf16)


def _pool_prompt(z3, pw, psc, layer, *, tt=1024):
    _, b, t, _ = z3.shape
    ng = len(POOL_WINDOWS)
    pad = max(POOL_WINDOWS)
    return pl.pallas_call(
        functools.partial(_pool_body, tt=tt),
        grid=(b, t // tt),
        in_specs=[pl.BlockSpec((1, 1, tt, LANES), lambda bi, tb, off=POOL_BLK0 + i: (off, bi, tb, 0))
                  for i in range(2 * ng)]
        + [
            pl.BlockSpec((None, ng, LANES, LANES), lambda bi, tb: (layer, 0, 0, 0)),
            pl.BlockSpec((None, 1, POOL_W), lambda bi, tb: (layer, 0, 0)),
        ],
        out_specs=pl.BlockSpec((1, tt, POOL_W), lambda bi, tb: (bi, tb, 0)),
        out_shape=jax.ShapeDtypeStruct((b, t, POOL_W), bf16),
        scratch_shapes=[pltpu.VMEM((ng, tt + pad, LANES), f32)],
        compiler_params=pltpu.CompilerParams(
            dimension_semantics=("arbitrary", "arbitrary"), vmem_limit_bytes=VMEM_LIMIT),
        name="pool_prompt",
    )(*([z3] * (2 * ng)), pw, psc)


def _bcast8(x):
    return jnp.broadcast_to(x, (8, x.shape[-1]))


def _sample_body(*refs, past_len, nbs):
    gens = [_sample_one(bi, *refs, past_len=past_len) for bi in range(nbs)]
    while gens:
        gens = [g for g in gens if next(g, StopIteration) is not StopIteration]


def _sample_one(bi, z_ref, sh_ref, wkv_ref, pool_ref, ck0, cv0, ck1, cv1, ck2, cv2,
                mu_ref, rp_ref, wup_ref, aup_ref, qnw_ref, knw_ref, cos_ref, sa_ref, sb_ref,
                pw_ref, psc_ref,
                mix_ref, swkv_ref, spool_ref, sk0, sv0, sk1, sv1, sk2, sv2,
                qk_scr, *, past_len):
    z = z_ref[bi]
    za = z[:, 0:RWKV_COLS]
    zb = z[:, RWKV_COLS:RWKV_COLS + ATT_COLS]
    zc = z[:, RWKV_COLS + ATT_COLS:RWKV_COLS + ATT_COLS + 2 * POOL_W]
    er = lax.broadcasted_iota(jnp.int32, (HEAD, HEAD), 0)
    ec = lax.broadcasted_iota(jnp.int32, (HEAD, HEAD), 1)
    eye = er == ec

    zs = za + (sh_ref[bi] - za) * mu_ref[...]
    w_ = RWKV_W
    r, k, v, g = (zs[:, i * w_:(i + 1) * w_] for i in range(4))
    w_dn = zs[:, 4 * w_:4 * w_ + LORA]
    a_dn = zs[:, 4 * w_ + LORA:]
    rp = rp_ref[...]
    w0, a0, k_k, k_a, r_k, ln_w, ln_b = (rp[i:i + 1] for i in range(7))
    lw = _dot(_bcast8(jnp.tanh(w_dn)).astype(bf16), wup_ref[...].astype(bf16))[0:1]
    la = _dot(_bcast8(a_dn).astype(bf16), aup_ref[...].astype(bf16))[0:1]
    yield
    decay = jnp.exp(-_DECAY_SCALE * jax.nn.sigmoid(w0 + lw))
    a = jax.nn.sigmoid(a0 + la)
    kk = k * k_k
    k2 = k * (1.0 + (a - 1.0) * k_a)
    heads = range(RWKV_HEADS)
    hsl = [slice(h * HEAD, (h + 1) * HEAD) for h in heads]

    def lane_sum(x):
        return jnp.sum(x, axis=-1, keepdims=True)

    kk_ss = [lane_sum(kk[:, hs] * kk[:, hs]) for hs in hsl]
    bonus = [lane_sum(r[:, hs] * k2[:, hs] * r_k[:, hs]) for hs in hsl]
    v_col = [lane_sum(jnp.where(eye, jnp.broadcast_to(v[:, hs], (HEAD, HEAD)), 0.0)) for hs in hsl]
    yield
    kkn = [kk[:, hs] / jnp.maximum(jnp.sqrt(ss), 1e-12) for hs, ss in zip(hsl, kk_ss)]
    sa_col = [lane_sum(wkv_ref[bi, h] * (-kkn[h])) for h in heads]
    yield
    sn = [wkv_ref[bi, h] * decay[:, hsl[h]] + sa_col[h] * (kkn[h] * a[:, hsl[h]]) + v_col[h] * k2[:, hsl[h]]
          for h in heads]
    for h in heads:
        swkv_ref[bi, h] = sn[h]
    o_col = [lane_sum(sn[h] * r[:, hsl[h]]) for h in heads]
    yield
    o = [jnp.sum(jnp.where(eye, jnp.broadcast_to(oc, (HEAD, HEAD)), 0.0), axis=0, keepdims=True) for oc in o_col]
    mean = [jnp.mean(x, axis=-1, keepdims=True) for x in o]
    yield
    dl = [x - mu_ for x, mu_ in zip(o, mean)]
    var = [jnp.mean(x * x, axis=-1, keepdims=True) for x in dl]
    yield
    for h in heads:
        hs = hsl[h]
        on = dl[h] * lax.rsqrt(var[h] + GN_EPS) * ln_w[:, hs] + ln_b[:, hs] + bonus[h] * v[:, hs]
        mix_ref[bi, :, hs] = on * _silu(g[:, hs])

    aw = ATT_W
    q, kx, vx, gx = (zb[:, i * aw:(i + 1) * aw] for i in range(4))
    m0 = _pair_consts(8)
    for hp in range(ATT_W // LANES):
        ls = slice(hp * LANES, (hp + 1) * LANES)
        for idx, (src, nw) in enumerate(((q, qnw_ref), (kx, knw_ref))):
            x = _bcast8(src[:, ls])
            sq = x * x
            s0 = jnp.sum(jnp.where(m0, sq, 0.0), axis=-1, keepdims=True)
            s1 = jnp.sum(jnp.where(m0, 0.0, sq), axis=-1, keepdims=True)
            ms = jnp.where(m0, s0, s1) * (1.0 / HEAD)
            xn = x * lax.rsqrt(ms + NORM_EPS) * nw[...]
            qk_scr[bi, idx, :, ls] = _rope_pair(xn, cos_ref[...], sa_ref[...], sb_ref[...])
    yield
    qn = qk_scr[bi, 0, 0:1, :]
    kn = qk_scr[bi, 1, 0:1, :]
    scale = HEAD ** -0.5

    def heads4(row, gi):
        return jnp.concatenate([row[:, (gi * 4 + j) * HEAD:(gi * 4 + j + 1) * HEAD] for j in range(4)], axis=0)

    def to_col(row):
        return jnp.sum(jnp.where(eye, jnp.broadcast_to(row, (HEAD, HEAD)), 0.0), axis=-1, keepdims=True)

    def to_row(col):
        return jnp.sum(jnp.where(eye, jnp.broadcast_to(col, (HEAD, HEAD)), 0.0), axis=0, keepdims=True)

    caches = ((ck0, cv0, sk0, sv0), (ck1, cv1, sk1, sv1), (ck2, cv2, sk2, sv2))
    windows = []
    for gi, (ck, cv, sk, sv) in enumerate(caches):
        sk[bi, 0] = heads4(kn, gi)
        sv[bi, 0] = heads4(vx, gi)
        row_id = lax.broadcasted_iota(jnp.int32, (1, ck.shape[-1]), 1)
        windows.append((row_id % ATT_GROUPS[gi][1]) == 0)
    combos = [(gi, j) for gi in range(3) for j in range(4)]
    asl = [slice((gi * 4 + j) * HEAD, (gi * 4 + j + 1) * HEAD) for gi, j in combos]
    qh = [qn[:, hs] * scale for hs in asl]
    q_col = [to_col(x) for x in qh]
    s_new = [jnp.sum(kn[:, hs] * x, axis=-1, keepdims=True) for hs, x in zip(asl, qh)]
    yield
    s_all = [jnp.where(windows[gi], jnp.sum(caches[gi][0][bi, j] * qc, axis=0, keepdims=True), -jnp.inf)
             for (gi, j), qc in zip(combos, q_col)]
    m_all = [jnp.maximum(jnp.max(s, axis=-1, keepdims=True), sn_) for s, sn_ in zip(s_all, s_new)]
    yield
    p_all = [jnp.exp(s - m) for s, m in zip(s_all, m_all)]
    p_new = [jnp.exp(sn_ - m) for sn_, m in zip(s_new, m_all)]
    l_all = [jnp.sum(p, axis=-1, keepdims=True) + pn for p, pn in zip(p_all, p_new)]
    o_colv = [jnp.sum(caches[gi][1][bi, j] * p, axis=-1, keepdims=True) for (gi, j), p in zip(combos, p_all)]
    yield
    o_rows = [(to_row(oc) + pn * vx[:, hs]) / l for oc, pn, hs, l in zip(o_colv, p_new, asl, l_all)]
    lse = [m + jnp.log(l) for m, l in zip(m_all, l_all)]
    for j in range(4):
        mx = jnp.maximum(jnp.maximum(lse[j], lse[4 + j]), lse[8 + j])
        es = [jnp.exp(lse[gi * 4 + j] - mx) for gi in range(3)]
        inv = 1.0 / (es[0] + es[1] + es[2])
        for gi in range(3):
            ci = gi * 4 + j
            mix_ref[bi, :, RWKV_W + ci * HEAD:RWKV_W + (ci + 1) * HEAD] = (
                o_rows[ci] * (es[gi] * inv) * _silu(gx[:, asl[ci]]))

    u = zc[:, 0:POOL_W]
    gate = zc[:, POOL_W:]
    prev = pool_ref[bi]
    for gi, w in enumerate(POOL_WINDOWS):
        cs = slice(gi * LANES, (gi + 1) * LANES)
        ug = u[:, cs]
        sw = jnp.sum(prev[POOL_BUF - (w - 1):POOL_BUF, cs], axis=0, keepdims=True) + ug
        cnt = min(float(w), float(past_len) + 1.0)
        d = sw / cnt - ug
        dd = _dot(_bcast8(d).astype(bf16), pw_ref[gi].astype(bf16))[0:1] * psc_ref[:, cs]
        mix_ref[bi, :, RWKV_W + ATT_W + gi * LANES:RWKV_W + ATT_W + (gi + 1) * LANES] = dd * _silu(gate[:, cs])
    spool_ref[bi, 0:POOL_BUF - 1, :] = prev[1:POOL_BUF]
    spool_ref[bi, POOL_BUF - 1:POOL_BUF, :] = u


def _sample_step(zs, sh_all, wkv_all, pool_all, caches_all, layer, mu, rp, wup, aup, qnw, knw, cos, sa, sb,
                 pw, psc, *, past_len, nbs=2):
    nb = zs.shape[0]
    d_in = zs.shape[-1]
    d_mix = RWKV_W + ATT_W + POOL_W

    def full(shape):
        nd = len(shape)
        return pl.BlockSpec(shape, lambda b, nd=nd: (0,) * nd)

    assert nb % nbs == 0
    in_specs = [
        pl.BlockSpec((nbs, 1, d_in), lambda b: (b, 0, 0)),
        pl.BlockSpec((None, nbs, 1, RWKV_COLS), lambda b: (layer, b, 0, 0)),
        pl.BlockSpec((None, nbs, RWKV_HEADS, HEAD, HEAD), lambda b: (layer, b, 0, 0, 0)),
        pl.BlockSpec((None, nbs, POOL_BUF, POOL_W), lambda b: (layer, b, 0, 0)),
    ]
    in_specs += [pl.BlockSpec((None, nbs, 4, HEAD, c.shape[-1]), lambda b: (layer, b, 0, 0, 0))
                 for c in caches_all]

    def of_layer(x):
        nd = x.ndim - 1
        return pl.BlockSpec((None,) + x.shape[1:], lambda b, nd=nd: (layer,) + (0,) * nd)

    in_specs += [of_layer(mu), of_layer(rp), of_layer(wup), of_layer(aup), of_layer(qnw), of_layer(knw),
                 full(cos.shape), full(sa.shape), full(sb.shape), of_layer(pw), of_layer(psc)]
    out_specs = [
        pl.BlockSpec((nbs, 1, d_mix), lambda b: (b, 0, 0)),
        pl.BlockSpec((nbs, RWKV_HEADS, HEAD, HEAD), lambda b: (b, 0, 0, 0)),
        pl.BlockSpec((nbs, POOL_BUF, POOL_W), lambda b: (b, 0, 0)),
    ] + [pl.BlockSpec((nbs, 1, 4, HEAD), lambda b: (b, 0, 0, 0))] * 6
    out_shape = [
        jax.ShapeDtypeStruct((nb, 1, d_mix), f32),
        jax.ShapeDtypeStruct((nb, RWKV_HEADS, HEAD, HEAD), f32),
        jax.ShapeDtypeStruct((nb, POOL_BUF, POOL_W), f32),
    ] + [jax.ShapeDtypeStruct((nb, 1, 4, HEAD), f32)] * 6
    return pl.pallas_call(
        functools.partial(_sample_body, past_len=past_len, nbs=nbs),
        grid=(nb // nbs,),
        in_specs=in_specs,
        out_specs=out_specs,
        out_shape=out_shape,
        scratch_shapes=[pltpu.VMEM((nbs, 2, 8, ATT_W), f32)],
        compiler_params=pltpu.CompilerParams(
            dimension_semantics=("arbitrary",), vmem_limit_bytes=VMEM_LIMIT),
        name="sample_step",
    )(zs, sh_all, wkv_all, pool_all, *caches_all, mu, rp, wup, aup, qnw, knw, cos, sa, sb, pw, psc)


def _rope_tables(pos):
    half = ROPE_DIMS // 2
    inv = jnp.power(jnp.float32(ROPE_THETA), -jnp.arange(half, dtype=f32) * 2.0 / ROPE_DIMS)
    ang = pos[:, None] * inv[None, :]
    cos, sin = jnp.cos(ang), jnp.sin(ang)
    n = pos.shape[0]
    pad = jnp.zeros((n, HEAD - ROPE_DIMS), f32)
    zero = jnp.zeros((n, half), f32)
    c_head = jnp.concatenate([cos, cos, pad + 1.0], axis=1)
    a_head = jnp.concatenate([-sin, zero, pad], axis=1)
    b_head = jnp.concatenate([zero, sin, pad], axis=1)
    return tuple(jnp.concatenate([x, x], axis=1) for x in (c_head, a_head, b_head))


def kernel(x_prompt, x_sample, state_wkv, state_shift, state_pool,
           cache_k_w128, cache_v_w128, cache_k_w512, cache_v_w512, cache_k_w2048, cache_v_w2048,
           norm_w, w_in, w_out, rwkv_mu, rwkv_w0, rwkv_w_up, rwkv_a0, rwkv_a_up,
           rwkv_k_k, rwkv_k_a, rwkv_r_k, rwkv_ln_w, rwkv_ln_b, q_norm_w, k_norm_w, pool_w, pool_scale):
    b, t, d = x_prompt.shape
    nb, ts, _ = x_sample.shape
    depth = w_in.shape[0]
    d_in = w_in.shape[2]
    assert ts == 1 and t % 512 == 0
    past_len = 16384
    caches_in = ((cache_k_w128, cache_v_w128), (cache_k_w512, cache_v_w512), (cache_k_w2048, cache_v_w2048))
    for (window, _), (ck, _) in zip(ATT_GROUPS, caches_in):
        assert ck.shape[2] == window, "window buffers are expected to be full"

    cos_p, sa_p, sb_p = _rope_tables(jnp.arange(t, dtype=f32))
    cos_s, sa_s, sb_s = _rope_tables(past_len + jnp.arange(1, dtype=f32))

    hp = x_prompt.reshape(b * t, d)
    hs = x_sample.reshape(nb, d)
    npair = RWKV_HEADS // 2
    p_out = [[] for _ in range(9)]
    s_out = [[] for _ in range(9)]
    sh_all = state_shift.reshape(depth, nb, 1, RWKV_COLS)
    caches_all = [jnp.transpose(c, (0, 1, 3, 4, 2)) for pair in caches_in for c in pair]

    nw_all = norm_w.reshape(depth, 1, d)
    mu_all = rwkv_mu.reshape(depth, 1, RWKV_COLS)
    mu4 = rwkv_mu[:, :4 * RWKV_W].reshape(depth, 4, npair, LANES).transpose(0, 2, 1, 3)
    mul = jnp.broadcast_to(rwkv_mu[:, 4 * RWKV_W:].reshape(depth, 1, 1, LANES), (depth, npair, 1, LANES))
    vecs_all = jnp.stack([rwkv_w0, rwkv_a0, rwkv_k_k, rwkv_k_a, rwkv_r_k.reshape(depth, -1),
                          rwkv_ln_w, rwkv_ln_b], axis=1)
    pp_all = jnp.concatenate(
        [mu4, mul, vecs_all.reshape(depth, 7, npair, LANES).transpose(0, 2, 1, 3)], axis=2)
    wup = rwkv_w_up.reshape(depth, LORA, npair, LANES).transpose(0, 2, 1, 3)
    aup = rwkv_a_up.reshape(depth, LORA, npair, LANES).transpose(0, 2, 1, 3)
    zl = jnp.zeros_like(wup)
    wl_all = jnp.concatenate(
        [jnp.concatenate([wup, zl], axis=3), jnp.concatenate([zl, aup], axis=3)], axis=2)
    qnw_all = jnp.tile(q_norm_w, (1, 2)).reshape(depth, 1, LANES)
    knw_all = jnp.tile(k_norm_w, (1, 2)).reshape(depth, 1, LANES)
    psc_all = pool_scale.reshape(depth, 1, POOL_W)
    sin_p = sa_p + sb_p

    for l in range(depth):
        zp, zs = _inproj(hp, hs, nw_all, w_in, l)
        z3 = zp.reshape(d_in // LANES, b, t, LANES)

        ya, p_wkv = _rwkv_prompt(z3, pp_all, wl_all, l)
        att = _attn_prompt(z3, cos_p, sin_p, qnw_all, knw_all, l)
        yc = _pool_prompt(z3, pool_w, psc_all, l)

        sample = _sample_step(
            zs.reshape(nb, 1, -1), sh_all, state_wkv, state_pool, caches_all, l,
            mu_all, vecs_all, rwkv_w_up, rwkv_a_up, qnw_all, knw_all, cos_s, sa_s, sb_s,
            pool_w, psc_all, past_len=past_len)
        mix_s = sample[0].reshape(nb, -1)

        ys = [ya.reshape(b * t, RWKV_W)] + [y.reshape(b * t, 4 * HEAD) for y in att[:3]] + [yc.reshape(b * t, POOL_W)]
        hp, hs = _outproj(hp, ys, hs, mix_s, w_out, l)

        p_out[0].append(p_wkv)
        p_out[1].append(z3[:ATT_BLK0, :, -1, :].transpose(1, 0, 2).reshape(b, RWKV_COLS))
        p_out[2].append(z3[POOL_BLK0:POOL_BLK0 + POOL_W // LANES, :, t - POOL_BUF:, :]
                        .transpose(1, 2, 0, 3).reshape(b, POOL_BUF, POOL_W))
        for gi in range(3):
            keep = att[3 + 2 * gi].shape[1]
            p_out[3 + 2 * gi].append(att[3 + 2 * gi].reshape(b, keep, 4, HEAD))
            p_out[4 + 2 * gi].append(att[4 + 2 * gi].reshape(b, keep, 4, HEAD))
        s_out[0].append(sample[1])
        s_out[1].append(zs[:, :RWKV_COLS])
        s_out[2].append(sample[2])
        for i in range(6):
            s_out[3 + i].append(sample[3 + i])

    return (hp.reshape(b, t, d), hs.reshape(nb, 1, d),
            *[jnp.stack(x) for x in p_out], *[jnp.stack(x) for x in s_out])
```

```python
import functools

import jax
import jax.numpy as jnp
from jax import lax
from jax.experimental import pallas as pl
from jax.experimental.pallas import tpu as pltpu

f32 = jnp.float32
bf16 = jnp.bfloat16

HEAD = 64
LANES = 128
NORM_EPS = 1e-6
GN_EPS = 64e-5
RWKV_HEADS = 12
RWKV_W = RWKV_HEADS * HEAD
LORA = 64
RWKV_COLS = 4 * RWKV_W + 2 * LORA
ATT_GROUPS = ((128, 1), (512, 4), (2048, 16))
ATT_W = 12 * HEAD
ATT_COLS = 4 * ATT_W
ROPE_THETA = 500000.0
ROPE_DIMS = 16
QBLK = 128
POOL_WINDOWS = (2, 4, 8, 16)
POOL_W = 512
POOL_BUF = 15
CHUNK = 64
VMEM_LIMIT = 56 * 1024 * 1024

ATT_BLK0 = RWKV_COLS // LANES
POOL_BLK0 = (RWKV_COLS + ATT_COLS) // LANES


def _dot(a, b):
    return jnp.dot(a, b, preferred_element_type=f32)


def _dot_nt(a, b):
    return lax.dot_general(a, b, (((1,), (1,)), ((), ())), preferred_element_type=f32)


def _dot_tn(a, b):
    return lax.dot_general(a, b, (((0,), (0,)), ((), ())), preferred_element_type=f32)


def _split2(x):
    hi = x.astype(bf16)
    lo = (x - hi.astype(f32)).astype(bf16)
    return hi, lo


def _split3(x):
    hi = x.astype(bf16)
    r1 = x - hi.astype(f32)
    mid = r1.astype(bf16)
    lo = (r1 - mid.astype(f32)).astype(bf16)
    return hi, mid, lo


def _silu(x):
    return x * jax.nn.sigmoid(x)


def _rms(x, w):
    ms = jnp.mean(x * x, axis=-1, keepdims=True)
    return x * lax.rsqrt(ms + NORM_EPS) * w


def _pair_consts(n):
    lane = lax.broadcasted_iota(jnp.int32, (n, LANES), 1)
    return lane < HEAD


def _ones_blockdiag():
    r = lax.broadcasted_iota(jnp.int32, (LANES, LANES), 0)
    c = lax.broadcasted_iota(jnp.int32, (LANES, LANES), 1)
    return jnp.where((r < HEAD) == (c < HEAD), 1.0, 0.0).astype(bf16)


def _segsum(x, ones_bd):
    hi, lo = _split2(x)
    return _dot(jnp.concatenate([hi, lo], axis=1), jnp.concatenate([ones_bd, ones_bd], axis=0))


def _inproj_body(xp_hbm, xs_ref, nw_ref, w_ref, zp_ref, zs_ref, h_scr, x_buf, x_sem, *, tn, n, tm):
    i = pl.program_id(0)
    j = pl.program_id(1)
    ni = pl.num_programs(0)
    nj = pl.num_programs(1)

    def x_copy(blk):
        return pltpu.make_async_copy(xp_hbm.at[pl.ds(pl.multiple_of(blk * tm, tm), tm), :], x_buf, x_sem)

    @pl.when((j == 0) & (i == 0))
    def _():
        x_copy(0).start()

    @pl.when(j == 0)
    def _():
        x_copy(i).wait()
        h_scr[...] = _rms(x_buf[...], nw_ref[...]).astype(bf16)

    @pl.when((j == 1) & (i + 1 < ni))
    def _():
        x_copy(i + 1).start()

    tail = n - (pl.cdiv(n, tn) - 1) * tn

    def project(width):
        w = w_ref[:, 0:width].astype(bf16)
        acc = _dot(h_scr[...], w)
        for c in range(width // LANES):
            zp_ref[c] = acc[:, c * LANES:(c + 1) * LANES]

        @pl.when(i == 0)
        def _():
            hs = _rms(xs_ref[...], nw_ref[...]).astype(bf16)
            col = pl.multiple_of(j * tn, LANES)
            zs_ref[:, pl.ds(col, width)] = _dot(hs, w)

    @pl.when(j < nj - 1)
    def _():
        project(tn)

    @pl.when(j == nj - 1)
    def _():
        project(tail)


def _inproj(xp, xs, nw, w_all, layer, *, tm=2048, tn=512):
    m, d = xp.shape
    n = w_all.shape[2]
    nb = xs.shape[0]
    nj = pl.cdiv(n, tn)
    assert n % LANES == 0 and tn % LANES == 0 and m % tm == 0 and nj >= 2
    return pl.pallas_call(
        functools.partial(_inproj_body, tn=tn, n=n, tm=tm),
        grid=(m // tm, nj),
        in_specs=[
            pl.BlockSpec(memory_space=pl.ANY),
            pl.BlockSpec((nb, d), lambda i, j: (0, 0)),
            pl.BlockSpec((None, 1, d), lambda i, j: (layer, 0, 0)),
            pl.BlockSpec((None, d, tn), lambda i, j: (layer, 0, j)),
        ],
        out_specs=[
            pl.BlockSpec((tn // LANES, tm, LANES), lambda i, j: (j, i, 0)),
            pl.BlockSpec((nb, n), lambda i, j: (0, 0)),
        ],
        out_shape=[jax.ShapeDtypeStruct((n // LANES, m, LANES), f32),
                   jax.ShapeDtypeStruct((nb, n), f32)],
        scratch_shapes=[pltpu.VMEM((tm, d), bf16), pltpu.VMEM((tm, d), f32), pltpu.SemaphoreType.DMA(())],
        compiler_params=pltpu.CompilerParams(
            dimension_semantics=("arbitrary", "arbitrary"), vmem_limit_bytes=VMEM_LIMIT),
        name="inproj",
    )(xp, xs, nw, w_all)


def _outproj_body(*refs, tn, widths):
    n = len(widths)
    xp_ref = refs[0]
    y_refs = refs[1:1 + n]
    xs_ref, ms_ref, w_ref, op_ref, os_ref, w_scr = refs[1 + n:]
    i = pl.program_id(0)
    j = pl.program_id(1)
    col = pl.multiple_of(j * tn, LANES)

    @pl.when(i == 0)
    def _():
        w_scr[:, pl.ds(col, tn)] = w_ref[...].astype(bf16)

    w = w_scr[:, pl.ds(col, tn)]
    acc = None
    row = 0
    for y_ref, width in zip(y_refs, widths):
        part = _dot(y_ref[...], w[row:row + width])
        acc = part if acc is None else acc + part
        row += width
    op_ref[...] = xp_ref[...] + acc

    @pl.when(i == 0)
    def _():
        os_ref[:, pl.ds(col, tn)] = xs_ref[:, pl.ds(col, tn)] + _dot(ms_ref[...].astype(bf16), w)


def _outproj(xp, ys, xs, ms, w_all, layer, *, tm=2048, tn=512):
    m, d = xp.shape
    nb = xs.shape[0]
    dm = w_all.shape[1]
    widths = tuple(y.shape[1] for y in ys)
    assert sum(widths) == dm
    nj = d // tn
    return pl.pallas_call(
        functools.partial(_outproj_body, tn=tn, widths=widths),
        grid=(m // tm, d // tn),
        in_specs=[pl.BlockSpec((tm, tn), lambda i, j: (i, j))]
        + [pl.BlockSpec((tm, width), lambda i, j: (i, 0)) for width in widths]
        + [
            pl.BlockSpec((nb, d), lambda i, j: (0, 0)),
            pl.BlockSpec((nb, dm), lambda i, j: (0, 0)),
            pl.BlockSpec((None, dm, tn), lambda i, j: (layer, 0, jnp.where(i == 0, j, nj - 1))),
        ],
        out_specs=[
            pl.BlockSpec((tm, tn), lambda i, j: (i, j)),
            pl.BlockSpec((nb, d), lambda i, j: (0, 0)),
        ],
        out_shape=[jax.ShapeDtypeStruct((m, d), f32), jax.ShapeDtypeStruct((nb, d), f32)],
        scratch_shapes=[pltpu.VMEM((dm, d), bf16)],
        compiler_params=pltpu.CompilerParams(
            dimension_semantics=("arbitrary", "arbitrary"), vmem_limit_bytes=VMEM_LIMIT),
        name="outproj",
    )(xp, *ys, xs, ms, w_all)


_DECAY_SCALE = 0.6065306597126334

_P_MU, _P_W0, _P_A0, _P_KK, _P_KA, _P_RK, _P_LNW, _P_LNB = 0, 5, 6, 7, 8, 9, 10, 11


def _rwkv_chunk_stages(at, bt, kt, rt, v, wcs, consts):
    m0, gmask, bdmask, eye = consts
    m0w = jnp.concatenate([m0, m0], axis=1)
    eye_pair = jnp.where(lax.broadcasted_iota(jnp.int32, (CHUNK, LANES), 1) % CHUNK
                         == lax.broadcasted_iota(jnp.int32, (CHUNK, LANES), 0), 1.0, 0.0)
    zb64 = jnp.zeros((CHUNK, LANES), bf16)
    n = len(wcs)

    def rows(x, c):
        return x[c * CHUNK:(c + 1) * CHUNK]

    def split_heads(xb, m):
        zero = jnp.zeros_like(xb)
        return jnp.concatenate([jnp.where(m, xb, zero), jnp.where(m, zero, xb)], axis=0)

    g, p, x, mc, nc, rp, ov = {}, {}, {}, {}, {}, {}, {}

    def stage_scores(chs):
        for c in chs:
            lhs = jnp.concatenate([rows(at, c), rows(rt, c)], axis=0).astype(bf16)
            rhs = jnp.concatenate([split_heads(rows(bt, c).astype(bf16), m0),
                                   split_heads(rows(kt, c).astype(bf16), m0)], axis=0)
            g[c] = jnp.where(gmask, _dot_nt(lhs, rhs), 0.0)
            p[c] = g[c][0:CHUNK, 0:LANES]

    av, tinv = {}, {}

    def stage_level(chs, lev):
        for c in chs:
            pb = p[c].astype(bf16)
            if lev == 0:
                vs = split_heads(rows(v, c).astype(bf16), m0)
                av[c] = _dot(g[c][0:CHUNK, LANES:].astype(bf16), vs)
                tinv[c] = eye_pair + p[c]
                p[c] = _dot(pb, split_heads(pb, m0))
            elif lev < 5:
                tt_ = split_heads(tinv[c].astype(bf16), m0)
                out = _dot(pb, jnp.concatenate([split_heads(pb, m0), tt_], axis=1))
                p[c] = out[:, 0:LANES]
                tinv[c] = tinv[c] + out[:, LANES:]
            else:
                tinv[c] = tinv[c] + _dot(pb, split_heads(tinv[c].astype(bf16), m0))

    def stage_apply(chs):
        for c in chs:
            rhs = split_heads(jnp.concatenate([av[c], rows(at, c)], axis=1).astype(bf16), m0w)
            x[c] = _dot(tinv[c].astype(bf16), rhs)

    def stage_fold(chs):
        for c in chs:
            xb = x[c].astype(bf16)
            uv = xb[:, 0:LANES]
            ap = xb[:, LANES:]
            vb = rows(v, c).astype(bf16)
            tnl = jnp.concatenate([rows(bt, c) * wcs[c], rows(kt, c) * wcs[c]], axis=0).astype(bf16)
            tnr = jnp.concatenate(
                [jnp.concatenate([ap, uv], axis=1), jnp.concatenate([zb64, vb], axis=1)], axis=0)
            mn = jnp.where(bdmask, _dot_tn(tnl, tnr), 0.0)
            mc[c] = mn[:, 0:LANES] + jnp.where(eye, jnp.broadcast_to(wcs[c], (LANES, LANES)), 0.0)
            nc[c] = mn[:, LANES:]
            l2 = g[c][CHUNK:].astype(bf16)
            r2 = jnp.concatenate([
                split_heads(jnp.concatenate([ap, uv], axis=1), m0w),
                split_heads(jnp.concatenate([zb64, vb], axis=1), m0w)], axis=0)
            ro = _dot(l2, r2)
            rp[c] = rows(rt, c) + ro[:, 0:LANES]
            ov[c] = ro[:, LANES:]

    def carry(c, st):
        seq = _dot(jnp.concatenate([rp[c], mc[c]], axis=0).astype(bf16), st.astype(bf16))
        return seq[0:CHUNK] + ov[c], seq[CHUNK:] + nc[c]

    chs = list(range(n))
    stages = ([functools.partial(stage_scores, chs)]
              + [functools.partial(stage_level, chs, lev) for lev in range(6)]
              + [functools.partial(stage_apply, chs), functools.partial(stage_fold, chs)])
    return stages, carry


def _rwkv_body(zr_ref, zk_ref, zv_ref, zg_ref, zl_ref, pp_ref, wl_ref, y_ref, s_ref,
               sh_scr, st_scr, *, tt, npr, lock):
    tb = pl.program_id(2)
    nt = pl.num_programs(2)
    nch = tt // CHUNK
    nsh = 4 * npr + 1

    @pl.when(tb == 0)
    def _():
        st_scr[...] = jnp.zeros_like(st_scr)
        for i in range(nsh):
            sh_scr[i, 7:8, :] = jnp.zeros((1, LANES), f32)

    @pl.when(tb > 0)
    def _():
        for i in range(nsh):
            sh_scr[i, 7:8, :] = sh_scr[i, 7 + tt:8 + tt, :]

    def shifted(idx, z, mu):
        sh_scr[idx, 8:8 + tt, :] = z
        zprev = sh_scr[idx, pl.ds(7, tt), :]
        return z + (zprev - z) * mu

    m0t = _pair_consts(tt)
    ones_bd = _ones_blockdiag()
    lo = shifted(4 * npr, zl_ref[0, 0], pp_ref[0, _P_MU + 4:_P_MU + 5])
    xl = jnp.where(m0t, jnp.tanh(lo), lo).astype(bf16)

    m0 = _pair_consts(CHUNK)
    gr = lax.broadcasted_iota(jnp.int32, (2 * CHUNK, 2 * LANES), 0)
    gc = lax.broadcasted_iota(jnp.int32, (2 * CHUNK, 2 * LANES), 1)
    gt = gr % CHUNK
    gs = gc % CHUNK
    gmask = (gt > gs) | ((gr >= CHUNK) & (gt == gs))
    br = lax.broadcasted_iota(jnp.int32, (LANES, 2 * LANES), 0)
    bc = lax.broadcasted_iota(jnp.int32, (LANES, 2 * LANES), 1)
    bdmask = (br < HEAD) == ((bc % LANES) < HEAD)
    er = lax.broadcasted_iota(jnp.int32, (LANES, LANES), 0)
    ec = lax.broadcasted_iota(jnp.int32, (LANES, LANES), 1)
    eye = er == ec
    consts = (m0, gmask, bdmask, eye)
    tr = lax.broadcasted_iota(jnp.int32, (CHUNK, CHUNK), 0)
    tc = lax.broadcasted_iota(jnp.int32, (CHUNK, CHUNK), 1)
    tril = jnp.where(tc <= tr, 1.0, 0.0).astype(bf16)
    tril3 = jnp.concatenate([tril, tril, tril], axis=1)

    def prepare(s, out):
        pp = pp_ref[s]
        r, k, v, g = [shifted(4 * s + i, ref[s, 0], pp[_P_MU + i:_P_MU + i + 1])
                      for i, ref in enumerate((zr_ref, zk_ref, zv_ref, zg_ref))]
        la = _dot(xl, wl_ref[s].astype(bf16))
        kk = k * pp[_P_KK:_P_KK + 1]
        kk_ss = _segsum(kk * kk, ones_bd)
        yield
        ld = -_DECAY_SCALE * jax.nn.sigmoid(pp[_P_W0:_P_W0 + 1] + la[:, 0:LANES])
        a = jax.nn.sigmoid(pp[_P_A0:_P_A0 + 1] + la[:, LANES:])
        kk = kk * jnp.minimum(lax.rsqrt(kk_ss), 1e12)
        k2 = k * (1.0 + (a - 1.0) * pp[_P_KA:_P_KA + 1])
        h3 = _split3(ld)
        cl = jnp.concatenate([
            _dot(tril3, jnp.concatenate([h[c * CHUNK:(c + 1) * CHUNK] for h in h3], axis=0))
            for c in range(nch)], axis=0)
        bonus = _segsum(r * k2 * pp[_P_RK:_P_RK + 1], ones_bd) * v
        yield
        e_in = jnp.exp(cl)
        e_neg = jnp.exp(-cl)
        wcs = [e_in[(c + 1) * CHUNK - 1:(c + 1) * CHUNK] for c in range(nch)]
        out["stages"], out["carry"] = _rwkv_chunk_stages(
            -kk * jnp.exp(cl - ld), kk * a * e_neg, k2 * e_neg, r * e_in, v, wcs, consts)
        out["epi"] = (pp, g, bonus)

    def finish(s, outs, st, epi):
        pp, g, bonus = epi
        st_scr[s] = st
        o = jnp.concatenate(outs, axis=0)
        mean = _segsum(o, ones_bd) * (1.0 / HEAD)
        yield
        dl = o - mean
        var = _segsum(dl * dl, ones_bd) * (1.0 / HEAD)
        yield
        on = dl * lax.rsqrt(var + GN_EPS) * pp[_P_LNW:_P_LNW + 1] + pp[_P_LNB:_P_LNB + 1] + bonus
        y_ref[0, :, s * LANES:(s + 1) * LANES] = (on * _silu(g)).astype(bf16)

    def drain(gen):
        for _ in gen:
            pass

    def lockstep(gens):
        gens = list(gens)
        while gens:
            gens = [g for g in gens if next(g, StopIteration) is not StopIteration]
            yield

    preps = [dict() for _ in range(npr)]
    units = [list(range(i, min(i + lock, npr))) for i in range(0, npr, lock)]
    drain(lockstep(prepare(s, preps[s]) for s in units[0]))
    prev = None
    closing = iter(())
    for ui, unit in enumerate(units):
        nxt = (lockstep(prepare(s, preps[s]) for s in units[ui + 1]) if ui + 1 < len(units) else iter(()))
        for stage_row in zip(*[preps[s]["stages"] for s in unit]):
            for stage in stage_row:
                stage()
            if prev is not None:
                for cur in prev:
                    if cur["todo"]:
                        cur["step"]()
            next(closing, None)
            next(nxt, None)
        drain(closing)
        drain(nxt)
        if prev is not None:
            while any(cur["todo"] for cur in prev):
                for cur in prev:
                    if cur["todo"]:
                        cur["step"]()
            closing = lockstep(finish(cur["s"], cur["outs"], cur["st"][0], cur["epi"]) for cur in prev)
        prev = []
        for s in unit:
            cur = {"s": s, "outs": [], "st": [st_scr[s]], "epi": preps[s]["epi"], "todo": list(range(nch))}

            def step(cur=cur, carry=preps[s]["carry"]):
                o_c, cur["st"][0] = carry(cur["todo"].pop(0), cur["st"][0])
                cur["outs"].append(o_c)

            cur["step"] = step
            prev.append(cur)
    while any(cur["todo"] for cur in prev):
        for cur in prev:
            if cur["todo"]:
                cur["step"]()
        next(closing, None)
    drain(closing)
    drain(lockstep(finish(cur["s"], cur["outs"], cur["st"][0], cur["epi"]) for cur in prev))

    @pl.when(tb == nt - 1)
    def _():
        for s in range(npr):
            stt = st_scr[s].T
            s_ref[0, 2 * s] = stt[0:HEAD, 0:HEAD]
            s_ref[0, 2 * s + 1] = stt[HEAD:, HEAD:]


def _rwkv_prompt(z3, pp, wl, layer, *, tt=512, npr=6, lock=2):
    _, b, t, _ = z3.shape
    npair = RWKV_HEADS // 2
    ngrp = npair // npr
    wide = npr * LANES

    def zspec(part):
        return pl.BlockSpec((npr, 1, tt, LANES), lambda bi, p, tb, part=part: (part * ngrp + p, bi, tb, 0))

    lora_blk = 4 * RWKV_W // LANES
    return pl.pallas_call(
        functools.partial(_rwkv_body, tt=tt, npr=npr, lock=lock),
        grid=(b, ngrp, t // tt),
        in_specs=[
            zspec(0), zspec(1), zspec(2), zspec(3),
            pl.BlockSpec((1, 1, tt, LANES), lambda bi, p, tb: (lora_blk, bi, tb, 0)),
            pl.BlockSpec((None, npr, 12, LANES), lambda bi, p, tb: (layer, p, 0, 0)),
            pl.BlockSpec((None, npr, LANES, 2 * LANES), lambda bi, p, tb: (layer, p, 0, 0)),
        ],
        out_specs=[
            pl.BlockSpec((1, tt, wide), lambda bi, p, tb: (bi, tb, p)),
            pl.BlockSpec((1, 2 * npr, HEAD, HEAD), lambda bi, p, tb: (bi, p, 0, 0)),
        ],
        out_shape=[jax.ShapeDtypeStruct((b, t, RWKV_W), bf16),
                   jax.ShapeDtypeStruct((b, RWKV_HEADS, HEAD, HEAD), f32)],
        scratch_shapes=[pltpu.VMEM((4 * npr + 1, tt + 8, LANES), f32), pltpu.VMEM((npr, LANES, LANES), f32)],
        compiler_params=pltpu.CompilerParams(
            dimension_semantics=("arbitrary", "arbitrary", "arbitrary"), vmem_limit_bytes=VMEM_LIMIT),
        name="rwkv_prompt",
    )(z3, z3, z3, z3, z3, pp, wl)


def _rope_pair(x, cos, sa, sb):
    return x * cos + pltpu.roll(x, LANES - ROPE_DIMS // 2, 1) * sa + pltpu.roll(x, ROPE_DIMS // 2, 1) * sb


def _attn_body(*refs, t):
    qkvg = [[r.at[0] for r in refs[4 * gi:4 * gi + 4]] for gi in range(3)]
    cos_ref, sin_ref, qnw_ref, knw_ref = refs[12:16]
    y_refs = refs[16:19]
    kv_refs = [refs[19 + 2 * gi:21 + 2 * gi] for gi in range(3)]
    qn_scr, kn_scr, o_scr, lse_scr = refs[25:29]

    rb = 256
    ones_bd = _ones_blockdiag()
    m0q = _pair_consts(QBLK)
    scale = HEAD ** -0.5
    half = ROPE_DIMS // 2
    pj = lax.broadcasted_iota(jnp.int32, (LANES, LANES), 0)
    pi = lax.broadcasted_iota(jnp.int32, (LANES, LANES), 1)
    pin = pi % HEAD
    perm = jnp.where(((pin < half) & (pj == pi + half)) | ((pin >= half) & (pin < ROPE_DIMS) & (pj == pi - half)),
                     1.0, 0.0).astype(bf16)
    zpad = jnp.zeros((LANES, LANES), bf16)
    sum_swap = jnp.concatenate([jnp.concatenate([ones_bd, zpad], axis=1),
                                jnp.concatenate([zpad, perm], axis=1)], axis=0)

    for gi, (window, dil) in enumerate(ATT_GROUPS):
        q_ref, k_ref, v_ref, g_ref = qkvg[gi]
        pk_ref, pv_ref = kv_refs[gi]
        keep = min(window, t)

        def norm_rows(i, carry, q_ref=q_ref, k_ref=k_ref):
            jobs = []
            for u in range(2):
                rows = pl.ds(pl.multiple_of((2 * i + u) * rb, rb), rb)
                for src, nw, dst in ((q_ref, qnw_ref, qn_scr), (k_ref, knw_ref, kn_scr)):
                    x = src[0, rows, :]
                    y = x * nw[...]
                    hi, lo = _split2(jnp.concatenate([x * x, y], axis=1))
                    jobs.append((y, _dot(hi, sum_swap) + _dot(lo, sum_swap), dst, rows))
            for y, res, dst, rows in jobs:
                rs = lax.rsqrt(res[:, 0:LANES] * (1.0 / HEAD) + NORM_EPS)
                dst[rows, :] = (y * cos_ref[rows, :] + res[:, LANES:] * sin_ref[rows, :]) * rs
            return carry

        lax.fori_loop(0, t // (2 * rb), norm_rows, 0)
        pk_ref[0] = kn_scr[t - keep:t, :]
        pv_ref[0] = v_ref[0, t - keep:t, :]

        n_sub = t // dil
        n_blk = n_sub // QBLK

        def tiles(starts, nk, first, v_ref=v_ref, dil=dil, gi=gi):
            qi = lax.broadcasted_iota(jnp.int32, (QBLK, nk), 0)
            kj = lax.broadcasted_iota(jnp.int32, (QBLK, nk), 1)
            mask = (kj <= qi) if first else ((kj >= qi) & (kj <= qi + QBLK))
            scores, vts = [], []
            for q0, k0 in starts:
                qt = qn_scr[pl.ds(q0, QBLK, stride=dil), :] * scale
                kt = kn_scr[pl.ds(k0, nk, stride=dil), :].astype(bf16)
                vts.append(v_ref[0, pl.ds(k0, nk, stride=dil), :].astype(bf16))
                for hh in range(2):
                    qh = jnp.where(m0q, qt, 0.0) if hh == 0 else jnp.where(m0q, 0.0, qt)
                    scores.append(_dot_nt(qh.astype(bf16), kt))
            probs, sums, lses = [], [], []
            for s in scores:
                s = jnp.where(mask, s, -jnp.inf)
                m = jnp.max(s, axis=-1, keepdims=True)
                p = jnp.exp(s - m)
                l = jnp.sum(p, axis=-1, keepdims=True)
                probs.append(p.astype(bf16))
                sums.append(l)
                lses.append(m + jnp.log(l))
            for ti, (q0, _) in enumerate(starts):
                oh = [_dot(probs[2 * ti + hh], vts[ti]) / sums[2 * ti + hh] for hh in range(2)]
                o_scr[gi, pl.ds(q0, QBLK, stride=dil), :] = jnp.where(m0q, oh[0], oh[1])
                lse_scr[gi, pl.ds(q0, QBLK, stride=dil), :] = jnp.where(
                    m0q, jnp.broadcast_to(lses[2 * ti], (QBLK, LANES)),
                    jnp.broadcast_to(lses[2 * ti + 1], (QBLK, LANES)))

        per_first = min(8, dil)

        def first_tiles(i, carry, tiles=tiles, per=per_first):
            tiles([(i * per + u, i * per + u) for u in range(per)], QBLK, True)
            return carry

        lax.fori_loop(0, dil // per_first, first_tiles, 0)

        if n_blk > 1:
            n_later = dil * (n_blk - 1)
            per_later = max(p for p in range(1, 7) if n_later % p == 0)

            def later_tiles(i, carry, tiles=tiles, dil=dil, per=per_later):
                starts = []
                for u in range(per):
                    idx = i * per + u
                    r = idx % dil
                    blk = idx // dil + 1
                    starts.append((r + blk * QBLK * dil, r + (blk - 1) * QBLK * dil))
                tiles(starts, 2 * QBLK, False)
                return carry

            lax.fori_loop(0, dil * (n_blk - 1) // per_later, later_tiles, 0)

    def combine(i, carry):
        rows = pl.ds(pl.multiple_of(i * rb, rb), rb)
        ls = [lse_scr[gi, rows, :] for gi in range(3)]
        mx = jnp.maximum(jnp.maximum(ls[0], ls[1]), ls[2])
        es = [jnp.exp(l - mx) for l in ls]
        inv = 1.0 / (es[0] + es[1] + es[2])
        for gi in range(3):
            gate = qkvg[gi][3][0, rows, :]
            y_refs[gi][0, rows, :] = (o_scr[gi, rows, :] * (es[gi] * inv) * _silu(gate)).astype(bf16)
        return carry

    lax.fori_loop(0, t // rb, combine, 0)


def _attn_prompt(z3, cos, sin, qnw, knw, layer):
    _, b, t, _ = z3.shape
    in_specs = []
    for gi in range(3):
        for part in range(4):
            off = ATT_BLK0 + part * 6 + 2 * gi
            in_specs.append(pl.BlockSpec((1, 1, t, LANES), lambda bi, jp, off=off: (off + jp, bi, 0, 0)))
    in_specs += [pl.BlockSpec((t, LANES), lambda bi, jp: (0, 0))] * 2
    in_specs += [pl.BlockSpec((None, 1, LANES), lambda bi, jp: (layer, 0, 0))] * 2
    out_specs = [pl.BlockSpec((1, t, LANES), lambda bi, jp: (bi, 0, jp))] * 3
    out_shape = [jax.ShapeDtypeStruct((b, t, 4 * HEAD), bf16)] * 3
    for window, _ in ATT_GROUPS:
        keep = min(window, t)
        out_specs += [pl.BlockSpec((1, keep, LANES), lambda bi, jp: (bi, 0, jp))] * 2
        out_shape += [jax.ShapeDtypeStruct((b, keep, 4 * HEAD), f32)] * 2
    outs = pl.pallas_call(
        functools.partial(_attn_body, t=t),
        grid=(b, 2),
        in_specs=in_specs,
        out_specs=out_specs,
        out_shape=out_shape,
        scratch_shapes=[pltpu.VMEM((t, LANES), f32), pltpu.VMEM((t, LANES), f32),
                        pltpu.VMEM((3, t, LANES), f32), pltpu.VMEM((3, t, LANES), f32)],
        compiler_params=pltpu.CompilerParams(
            dimension_semantics=("arbitrary", "arbitrary"), vmem_limit_bytes=VMEM_LIMIT),
        name="attn_prompt",
    )(*([z3] * 12), cos, sin, qnw, knw)
    return outs


def _pool_body(*refs, tt):
    ng = len(POOL_WINDOWS)
    u_refs = [r.at[0] for r in refs[0:ng]]
    g_refs = [r.at[0] for r in refs[ng:2 * ng]]
    w_ref, sc_ref, y_ref, scr = refs[2 * ng:]
    tb = pl.program_id(1)
    pad = max(POOL_WINDOWS)

    @pl.when(tb == 0)
    def _():
        for gi in range(ng):
            scr[gi, 0:pad, :] = jnp.zeros((pad, LANES), f32)

    @pl.when(tb > 0)
    def _():
        for gi in range(ng):
            scr[gi, 0:pad, :] = scr[gi, tt:tt + pad, :]

    pos = (tb * tt + lax.broadcasted_iota(jnp.int32, (tt, 1), 0)).astype(f32)
    for gi, w in enumerate(POOL_WINDOWS):
        u = u_refs[gi][0]
        scr[gi, pad:pad + tt, :] = u
        acc = u
        for k in range(1, w):
            acc = acc + scr[gi, pl.ds(pad - k, tt), :]
        d = acc / jnp.minimum(float(w), pos + 1.0) - u
        dd = _dot(d.astype(bf16), w_ref[gi].astype(bf16)) * sc_ref[:, gi * LANES:(gi + 1) * LANES]
        y_ref[0, :, gi * LANES:(gi + 1) * LANES] = (dd * _silu(g_refs[gi][0])).astype(bf16)


def _pool_prompt(z3, pw, psc, layer, *, tt=1024):
    _, b, t, _ = z3.shape
    ng = len(POOL_WINDOWS)
    pad = max(POOL_WINDOWS)
    return pl.pallas_call(
        functools.partial(_pool_body, tt=tt),
        grid=(b, t // tt),
        in_specs=[pl.BlockSpec((1, 1, tt, LANES), lambda bi, tb, off=POOL_BLK0 + i: (off, bi, tb, 0))
                  for i in range(2 * ng)]
        + [
            pl.BlockSpec((None, ng, LANES, LANES), lambda bi, tb: (layer, 0, 0, 0)),
            pl.BlockSpec((None, 1, POOL_W), lambda bi, tb: (layer, 0, 0)),
        ],
        out_specs=pl.BlockSpec((1, tt, POOL_W), lambda bi, tb: (bi, tb, 0)),
        out_shape=jax.ShapeDtypeStruct((b, t, POOL_W), bf16),
        scratch_shapes=[pltpu.VMEM((ng, tt + pad, LANES), f32)],
        compiler_params=pltpu.CompilerParams(
            dimension_semantics=("arbitrary", "arbitrary"), vmem_limit_bytes=VMEM_LIMIT),
        name="pool_prompt",
    )(*([z3] * (2 * ng)), pw, psc)


def _bcast8(x):
    return jnp.broadcast_to(x, (8, x.shape[-1]))


def _sample_body(*refs, past_len, nbs):
    gens = [_sample_one(bi, *refs, past_len=past_len) for bi in range(nbs)]
    while gens:
        gens = [g for g in gens if next(g, StopIteration) is not StopIteration]


def _sample_one(bi, z_ref, sh_ref, wkv_ref, pool_ref, ck0, cv0, ck1, cv1, ck2, cv2,
                mu_ref, rp_ref, wup_ref, aup_ref, qnw_ref, knw_ref, cos_ref, sa_ref, sb_ref,
                pw_ref, psc_ref,
                mix_ref, swkv_ref, spool_ref, sk0, sv0, sk1, sv1, sk2, sv2,
                qk_scr, *, past_len):
    z = z_ref[bi]
    za = z[:, 0:RWKV_COLS]
    zb = z[:, RWKV_COLS:RWKV_COLS + ATT_COLS]
    zc = z[:, RWKV_COLS + ATT_COLS:RWKV_COLS + ATT_COLS + 2 * POOL_W]
    er = lax.broadcasted_iota(jnp.int32, (HEAD, HEAD), 0)
    ec = lax.broadcasted_iota(jnp.int32, (HEAD, HEAD), 1)
    eye = er == ec

    zs = za + (sh_ref[bi] - za) * mu_ref[...]
    w_ = RWKV_W
    r, k, v, g = (zs[:, i * w_:(i + 1) * w_] for i in range(4))
    w_dn = zs[:, 4 * w_:4 * w_ + LORA]
    a_dn = zs[:, 4 * w_ + LORA:]
    rp = rp_ref[...]
    w0, a0, k_k, k_a, r_k, ln_w, ln_b = (rp[i:i + 1] for i in range(7))
    lw = _dot(_bcast8(jnp.tanh(w_dn)).astype(bf16), wup_ref[...].astype(bf16))[0:1]
    la = _dot(_bcast8(a_dn).astype(bf16), aup_ref[...].astype(bf16))[0:1]
    yield
    decay = jnp.exp(-_DECAY_SCALE * jax.nn.sigmoid(w0 + lw))
    a = jax.nn.sigmoid(a0 + la)
    kk = k * k_k
    k2 = k * (1.0 + (a - 1.0) * k_a)
    heads = range(RWKV_HEADS)
    hsl = [slice(h * HEAD, (h + 1) * HEAD) for h in heads]

    def lane_sum(x):
        return jnp.sum(x, axis=-1, keepdims=True)

    kk_ss = [lane_sum(kk[:, hs] * kk[:, hs]) for hs in hsl]
    bonus = [lane_sum(r[:, hs] * k2[:, hs] * r_k[:, hs]) for hs in hsl]
    v_col = [lane_sum(jnp.where(eye, jnp.broadcast_to(v[:, hs], (HEAD, HEAD)), 0.0)) for hs in hsl]
    yield
    kkn = [kk[:, hs] / jnp.maximum(jnp.sqrt(ss), 1e-12) for hs, ss in zip(hsl, kk_ss)]
    sa_col = [lane_sum(wkv_ref[bi, h] * (-kkn[h])) for h in heads]
    yield
    sn = [wkv_ref[bi, h] * decay[:, hsl[h]] + sa_col[h] * (kkn[h] * a[:, hsl[h]]) + v_col[h] * k2[:, hsl[h]]
          for h in heads]
    for h in heads:
        swkv_ref[bi, h] = sn[h]
    o_col = [lane_sum(sn[h] * r[:, hsl[h]]) for h in heads]
    yield
    o = [jnp.sum(jnp.where(eye, jnp.broadcast_to(oc, (HEAD, HEAD)), 0.0), axis=0, keepdims=True) for oc in o_col]
    mean = [jnp.mean(x, axis=-1, keepdims=True) for x in o]
    yield
    dl = [x - mu_ for x, mu_ in zip(o, mean)]
    var = [jnp.mean(x * x, axis=-1, keepdims=True) for x in dl]
    yield
    for h in heads:
        hs = hsl[h]
        on = dl[h] * lax.rsqrt(var[h] + GN_EPS) * ln_w[:, hs] + ln_b[:, hs] + bonus[h] * v[:, hs]
        mix_ref[bi, :, hs] = on * _silu(g[:, hs])

    aw = ATT_W
    q, kx, vx, gx = (zb[:, i * aw:(i + 1) * aw] for i in range(4))
    m0 = _pair_consts(8)
    for hp in range(ATT_W // LANES):
        ls = slice(hp * LANES, (hp + 1) * LANES)
        for idx, (src, nw) in enumerate(((q, qnw_ref), (kx, knw_ref))):
            x = _bcast8(src[:, ls])
            sq = x * x
            s0 = jnp.sum(jnp.where(m0, sq, 0.0), axis=-1, keepdims=True)
            s1 = jnp.sum(jnp.where(m0, 0.0, sq), axis=-1, keepdims=True)
            ms = jnp.where(m0, s0, s1) * (1.0 / HEAD)
            xn = x * lax.rsqrt(ms + NORM_EPS) * nw[...]
            qk_scr[bi, idx, :, ls] = _rope_pair(xn, cos_ref[...], sa_ref[...], sb_ref[...])
    yield
    qn = qk_scr[bi, 0, 0:1, :]
    kn = qk_scr[bi, 1, 0:1, :]
    scale = HEAD ** -0.5

    def heads4(row, gi):
        return jnp.concatenate([row[:, (gi * 4 + j) * HEAD:(gi * 4 + j + 1) * HEAD] for j in range(4)], axis=0)

    def to_col(row):
        return jnp.sum(jnp.where(eye, jnp.broadcast_to(row, (HEAD, HEAD)), 0.0), axis=-1, keepdims=True)

    def to_row(col):
        return jnp.sum(jnp.where(eye, jnp.broadcast_to(col, (HEAD, HEAD)), 0.0), axis=0, keepdims=True)

    caches = ((ck0, cv0, sk0, sv0), (ck1, cv1, sk1, sv1), (ck2, cv2, sk2, sv2))
    windows = []
    for gi, (ck, cv, sk, sv) in enumerate(caches):
        sk[bi, 0] = heads4(kn, gi)
        sv[bi, 0] = heads4(vx, gi)
        row_id = lax.broadcasted_iota(jnp.int32, (1, ck.shape[-1]), 1)
        windows.append((row_id % ATT_GROUPS[gi][1]) == 0)
    combos = [(gi, j) for gi in range(3) for j in range(4)]
    asl = [slice((gi * 4 + j) * HEAD, (gi * 4 + j + 1) * HEAD) for gi, j in combos]
    qh = [qn[:, hs] * scale for hs in asl]
    q_col = [to_col(x) for x in qh]
    s_new = [jnp.sum(kn[:, hs] * x, axis=-1, keepdims=True) for hs, x in zip(asl, qh)]
    yield
    s_all = [jnp.where(windows[gi], jnp.sum(caches[gi][0][bi, j] * qc, axis=0, keepdims=True), -jnp.inf)
             for (gi, j), qc in zip(combos, q_col)]
    m_all = [jnp.maximum(jnp.max(s, axis=-1, keepdims=True), sn_) for s, sn_ in zip(s_all, s_new)]
    yield
    p_all = [jnp.exp(s - m) for s, m in zip(s_all, m_all)]
    p_new = [jnp.exp(sn_ - m) for sn_, m in zip(s_new, m_all)]
    l_all = [jnp.sum(p, axis=-1, keepdims=True) + pn for p, pn in zip(p_all, p_new)]
    o_colv = [jnp.sum(caches[gi][1][bi, j] * p, axis=-1, keepdims=True) for (gi, j), p in zip(combos, p_all)]
    yield
    o_rows = [(to_row(oc) + pn * vx[:, hs]) / l for oc, pn, hs, l in zip(o_colv, p_new, asl, l_all)]
    lse = [m + jnp.log(l) for m, l in zip(m_all, l_all)]
    for j in range(4):
        mx = jnp.maximum(jnp.maximum(lse[j], lse[4 + j]), lse[8 + j])
        es = [jnp.exp(lse[gi * 4 + j] - mx) for gi in range(3)]
        inv = 1.0 / (es[0] + es[1] + es[2])
        for gi in range(3):
            ci = gi * 4 + j
            mix_ref[bi, :, RWKV_W + ci * HEAD:RWKV_W + (ci + 1) * HEAD] = (
                o_rows[ci] * (es[gi] * inv) * _silu(gx[:, asl[ci]]))

    u = zc[:, 0:POOL_W]
    gate = zc[:, POOL_W:]
    prev = pool_ref[bi]
    for gi, w in enumerate(POOL_WINDOWS):
        cs = slice(gi * LANES, (gi + 1) * LANES)
        ug = u[:, cs]
        sw = jnp.sum(prev[POOL_BUF - (w - 1):POOL_BUF, cs], axis=0, keepdims=True) + ug
        cnt = min(float(w), float(past_len) + 1.0)
        d = sw / cnt - ug
        dd = _dot(_bcast8(d).astype(bf16), pw_ref[gi].astype(bf16))[0:1] * psc_ref[:, cs]
        mix_ref[bi, :, RWKV_W + ATT_W + gi * LANES:RWKV_W + ATT_W + (gi + 1) * LANES] = dd * _silu(gate[:, cs])
    spool_ref[bi, 0:POOL_BUF - 1, :] = prev[1:POOL_BUF]
    spool_ref[bi, POOL_BUF - 1:POOL_BUF, :] = u


def _sample_step(zs, sh_all, wkv_all, pool_all, caches_all, layer, mu, rp, wup, aup, qnw, knw, cos, sa, sb,
                 pw, psc, *, past_len, nbs=2):
    nb = zs.shape[0]
    d_in = zs.shape[-1]
    d_mix = RWKV_W + ATT_W + POOL_W

    def full(shape):
        nd = len(shape)
        return pl.BlockSpec(shape, lambda b, nd=nd: (0,) * nd)

    assert nb % nbs == 0
    in_specs = [
        pl.BlockSpec((nbs, 1, d_in), lambda b: (b, 0, 0)),
        pl.BlockSpec((None, nbs, 1, RWKV_COLS), lambda b: (layer, b, 0, 0)),
        pl.BlockSpec((None, nbs, RWKV_HEADS, HEAD, HEAD), lambda b: (layer, b, 0, 0, 0)),
        pl.BlockSpec((None, nbs, POOL_BUF, POOL_W), lambda b: (layer, b, 0, 0)),
    ]
    in_specs += [pl.BlockSpec((None, nbs, 4, HEAD, c.shape[-1]), lambda b: (layer, b, 0, 0, 0))
                 for c in caches_all]

    def of_layer(x):
        nd = x.ndim - 1
        return pl.BlockSpec((None,) + x.shape[1:], lambda b, nd=nd: (layer,) + (0,) * nd)

    in_specs += [of_layer(mu), of_layer(rp), of_layer(wup), of_layer(aup), of_layer(qnw), of_layer(knw),
                 full(cos.shape), full(sa.shape), full(sb.shape), of_layer(pw), of_layer(psc)]
    out_specs = [
        pl.BlockSpec((nbs, 1, d_mix), lambda b: (b, 0, 0)),
        pl.BlockSpec((nbs, RWKV_HEADS, HEAD, HEAD), lambda b: (b, 0, 0, 0)),
        pl.BlockSpec((nbs, POOL_BUF, POOL_W), lambda b: (b, 0, 0)),
    ] + [pl.BlockSpec((nbs, 1, 4, HEAD), lambda b: (b, 0, 0, 0))] * 6
    out_shape = [
        jax.ShapeDtypeStruct((nb, 1, d_mix), f32),
        jax.ShapeDtypeStruct((nb, RWKV_HEADS, HEAD, HEAD), f32),
        jax.ShapeDtypeStruct((nb, POOL_BUF, POOL_W), f32),
    ] + [jax.ShapeDtypeStruct((nb, 1, 4, HEAD), f32)] * 6
    return pl.pallas_call(
        functools.partial(_sample_body, past_len=past_len, nbs=nbs),
        grid=(nb // nbs,),
        in_specs=in_specs,
        out_specs=out_specs,
        out_shape=out_shape,
        scratch_shapes=[pltpu.VMEM((nbs, 2, 8, ATT_W), f32)],
        compiler_params=pltpu.CompilerParams(
            dimension_semantics=("arbitrary",), vmem_limit_bytes=VMEM_LIMIT),
        name="sample_step",
    )(zs, sh_all, wkv_all, pool_all, *caches_all, mu, rp, wup, aup, qnw, knw, cos, sa, sb, pw, psc)


def _rope_tables(pos):
    half = ROPE_DIMS // 2
    inv = jnp.power(jnp.float32(ROPE_THETA), -jnp.arange(half, dtype=f32) * 2.0 / ROPE_DIMS)
    ang = pos[:, None] * inv[None, :]
    cos, sin = jnp.cos(ang), jnp.sin(ang)
    n = pos.shape[0]
    pad = jnp.zeros((n, HEAD - ROPE_DIMS), f32)
    zero = jnp.zeros((n, half), f32)
    c_head = jnp.concatenate([cos, cos, pad + 1.0], axis=1)
    a_head = jnp.concatenate([-sin, zero, pad], axis=1)
    b_head = jnp.concatenate([zero, sin, pad], axis=1)
    return tuple(jnp.concatenate([x, x], axis=1) for x in (c_head, a_head, b_head))


def kernel(x_prompt, x_sample, state_wkv, state_shift, state_pool,
           cache_k_w128, cache_v_w128, cache_k_w512, cache_v_w512, cache_k_w2048, cache_v_w2048,
           norm_w, w_in, w_out, rwkv_mu, rwkv_w0, rwkv_w_up, rwkv_a0, rwkv_a_up,
           rwkv_k_k, rwkv_k_a, rwkv_r_k, rwkv_ln_w, rwkv_ln_b, q_norm_w, k_norm_w, pool_w, pool_scale):
    b, t, d = x_prompt.shape
    nb, ts, _ = x_sample.shape
    depth = w_in.shape[0]
    d_in = w_in.shape[2]
    assert ts == 1 and t % 512 == 0
    past_len = 16384
    caches_in = ((cache_k_w128, cache_v_w128), (cache_k_w512, cache_v_w512), (cache_k_w2048, cache_v_w2048))
    for (window, _), (ck, _) in zip(ATT_GROUPS, caches_in):
        assert ck.shape[2] == window, "window buffers are expected to be full"

    cos_p, sa_p, sb_p = _rope_tables(jnp.arange(t, dtype=f32))
    cos_s, sa_s, sb_s = _rope_tables(past_len + jnp.arange(1, dtype=f32))

    hp = x_prompt.reshape(b * t, d)
    hs = x_sample.reshape(nb, d)
    npair = RWKV_HEADS // 2
    p_out = [[] for _ in range(9)]
    s_out = [[] for _ in range(9)]
    sh_all = state_shift.reshape(depth, nb, 1, RWKV_COLS)
    caches_all = [jnp.transpose(c, (0, 1, 3, 4, 2)) for pair in caches_in for c in pair]

    nw_all = norm_w.reshape(depth, 1, d)
    mu_all = rwkv_mu.reshape(depth, 1, RWKV_COLS)
    mu4 = rwkv_mu[:, :4 * RWKV_W].reshape(depth, 4, npair, LANES).transpose(0, 2, 1, 3)
    mul = jnp.broadcast_to(rwkv_mu[:, 4 * RWKV_W:].reshape(depth, 1, 1, LANES), (depth, npair, 1, LANES))
    vecs_all = jnp.stack([rwkv_w0, rwkv_a0, rwkv_k_k, rwkv_k_a, rwkv_r_k.reshape(depth, -1),
                          rwkv_ln_w, rwkv_ln_b], axis=1)
    pp_all = jnp.concatenate(
        [mu4, mul, vecs_all.reshape(depth, 7, npair, LANES).transpose(0, 2, 1, 3)], axis=2)
    wup = rwkv_w_up.reshape(depth, LORA, npair, LANES).transpose(0, 2, 1, 3)
    aup = rwkv_a_up.reshape(depth, LORA, npair, LANES).transpose(0, 2, 1, 3)
    zl = jnp.zeros_like(wup)
    wl_all = jnp.concatenate(
        [jnp.concatenate([wup, zl], axis=3), jnp.concatenate([zl, aup], axis=3)], axis=2)
    qnw_all = jnp.tile(q_norm_w, (1, 2)).reshape(depth, 1, LANES)
    knw_all = jnp.tile(k_norm_w, (1, 2)).reshape(depth, 1, LANES)
    psc_all = pool_scale.reshape(depth, 1, POOL_W)
    sin_p = sa_p + sb_p

    for l in range(depth):
        zp, zs = _inproj(hp, hs, nw_all, w_in, l)
        z3 = zp.reshape(d_in // LANES, b, t, LANES)

        ya, p_wkv = _rwkv_prompt(z3, pp_all, wl_all, l)
        att = _attn_prompt(z3, cos_p, sin_p, qnw_all, knw_all, l)
        yc = _pool_prompt(z3, pool_w, psc_all, l)

        sample = _sample_step(
            zs.reshape(nb, 1, -1), sh_all, state_wkv, state_pool, caches_all, l,
            mu_all, vecs_all, rwkv_w_up, rwkv_a_up, qnw_all, knw_all, cos_s, sa_s, sb_s,
            pool_w, psc_all, past_len=past_len)
        mix_s = sample[0].reshape(nb, -1)

        ys = [ya.reshape(b * t, RWKV_W)] + [y.reshape(b * t, 4 * HEAD) for y in att[:3]] + [yc.reshape(b * t, POOL_W)]
        hp, hs = _outproj(hp, ys, hs, mix_s, w_out, l)

        p_out[0].append(p_wkv)
        p_out[1].append(z3[:ATT_BLK0, :, -1, :].transpose(1, 0, 2).reshape(b, RWKV_COLS))
        p_out[2].append(z3[POOL_BLK0:POOL_BLK0 + POOL_W // LANES, :, t - POOL_BUF:, :]
                        .transpose(1, 2, 0, 3).reshape(b, POOL_BUF, POOL_W))
        for gi in range(3):
            keep = att[3 + 2 * gi].shape[1]
            p_out[3 + 2 * gi].append(att[3 + 2 * gi].reshape(b, keep, 4, HEAD))
            p_out[4 + 2 * gi].append(att[4 + 2 * gi].reshape(b, keep, 4, HEAD))
        s_out[0].append(sample[1])
        s_out[1].append(zs[:, :RWKV_COLS])
        s_out[2].append(sample[2])
        for i in range(6):
            s_out[3 + i].append(sample[3 + i])

    return (hp.reshape(b, t, d), hs.reshape(nb, 1, d),
            *[jnp.stack(x) for x in p_out], *[jnp.stack(x) for x in s_out])
```

```python
import functools

import jax
import jax.numpy as jnp
from jax import lax
from jax.experimental import pallas as pl
from jax.experimental.pallas import tpu as pltpu

f32 = jnp.float32
bf16 = jnp.bfloat16

HEAD = 64
LANES = 128
NORM_EPS = 1e-6
GN_EPS = 64e-5
RWKV_HEADS = 12
RWKV_W = RWKV_HEADS * HEAD
LORA = 64
RWKV_COLS = 4 * RWKV_W + 2 * LORA
ATT_GROUPS = ((128, 1), (512, 4), (2048, 16))
ATT_W = 12 * HEAD
ATT_COLS = 4 * ATT_W
ROPE_THETA = 500000.0
ROPE_DIMS = 16
QBLK = 128
POOL_WINDOWS = (2, 4, 8, 16)
POOL_W = 512
POOL_BUF = 15
CHUNK = 64
VMEM_LIMIT = 56 * 1024 * 1024

ATT_BLK0 = RWKV_COLS // LANES
POOL_BLK0 = (RWKV_COLS + ATT_COLS) // LANES


def _dot(a, b):
    return jnp.dot(a, b, preferred_element_type=f32)


def _dot_nt(a, b):
    return lax.dot_general(a, b, (((1,), (1,)), ((), ())), preferred_element_type=f32)


def _dot_tn(a, b):
    return lax.dot_general(a, b, (((0,), (0,)), ((), ())), preferred_element_type=f32)


def _split2(x):
    hi = x.astype(bf16)
    lo = (x - hi.astype(f32)).astype(bf16)
    return hi, lo


def _split3(x):
    hi = x.astype(bf16)
    r1 = x - hi.astype(f32)
    mid = r1.astype(bf16)
    lo = (r1 - mid.astype(f32)).astype(bf16)
    return hi, mid, lo


def _silu(x):
    return x * jax.nn.sigmoid(x)


def _rms(x, w):
    ms = jnp.mean(x * x, axis=-1, keepdims=True)
    return x * lax.rsqrt(ms + NORM_EPS) * w


def _pair_consts(n):
    lane = lax.broadcasted_iota(jnp.int32, (n, LANES), 1)
    return lane < HEAD


def _ones_blockdiag():
    r = lax.broadcasted_iota(jnp.int32, (LANES, LANES), 0)
    c = lax.broadcasted_iota(jnp.int32, (LANES, LANES), 1)
    return jnp.where((r < HEAD) == (c < HEAD), 1.0, 0.0).astype(bf16)


def _segsum(x, ones_bd):
    hi, lo = _split2(x)
    return _dot(jnp.concatenate([hi, lo], axis=1), jnp.concatenate([ones_bd, ones_bd], axis=0))


def _inproj_body(xp_hbm, xs_ref, nw_ref, w_ref, zp_ref, zs_ref, h_scr, x_buf, x_sem, *, tn, n, tm):
    i = pl.program_id(0)
    j = pl.program_id(1)
    ni = pl.num_programs(0)
    nj = pl.num_programs(1)

    def x_copy(blk):
        return pltpu.make_async_copy(xp_hbm.at[pl.ds(pl.multiple_of(blk * tm, tm), tm), :], x_buf, x_sem)

    @pl.when((j == 0) & (i == 0))
    def _():
        x_copy(0).start()

    @pl.when(j == 0)
    def _():
        x_copy(i).wait()
        h_scr[...] = _rms(x_buf[...], nw_ref[...]).astype(bf16)

    @pl.when((j == 1) & (i + 1 < ni))
    def _():
        x_copy(i + 1).start()

    tail = n - (pl.cdiv(n, tn) - 1) * tn

    def project(width):
        w = w_ref[:, 0:width].astype(bf16)
        acc = _dot(h_scr[...], w)
        for c in range(width // LANES):
            zp_ref[c] = acc[:, c * LANES:(c + 1) * LANES]

        @pl.when(i == 0)
        def _():
            hs = _rms(xs_ref[...], nw_ref[...]).astype(bf16)
            col = pl.multiple_of(j * tn, LANES)
            zs_ref[:, pl.ds(col, width)] = _dot(hs, w)

    @pl.when(j < nj - 1)
    def _():
        project(tn)

    @pl.when(j == nj - 1)
    def _():
        project(tail)


def _inproj(xp, xs, nw, w_all, layer, *, tm=2048, tn=512):
    m, d = xp.shape
    n = w_all.shape[2]
    nb = xs.shape[0]
    nj = pl.cdiv(n, tn)
    assert n % LANES == 0 and tn % LANES == 0 and m % tm == 0 and nj >= 2
    return pl.pallas_call(
        functools.partial(_inproj_body, tn=tn, n=n, tm=tm),
        grid=(m // tm, nj),
        in_specs=[
            pl.BlockSpec(memory_space=pl.ANY),
            pl.BlockSpec((nb, d), lambda i, j: (0, 0)),
            pl.BlockSpec((None, 1, d), lambda i, j: (layer, 0, 0)),
            pl.BlockSpec((None, d, tn), lambda i, j: (layer, 0, j)),
        ],
        out_specs=[
            pl.BlockSpec((tn // LANES, tm, LANES), lambda i, j: (j, i, 0)),
            pl.BlockSpec((nb, n), lambda i, j: (0, 0)),
        ],
        out_shape=[jax.ShapeDtypeStruct((n // LANES, m, LANES), f32),
                   jax.ShapeDtypeStruct((nb, n), f32)],
        scratch_shapes=[pltpu.VMEM((tm, d), bf16), pltpu.VMEM((tm, d), f32), pltpu.SemaphoreType.DMA(())],
        compiler_params=pltpu.CompilerParams(
            dimension_semantics=("arbitrary", "arbitrary"), vmem_limit_bytes=VMEM_LIMIT),
        name="inproj",
    )(xp, xs, nw, w_all)


def _outproj_body(*refs, tn, widths):
    n = len(widths)
    xp_ref = refs[0]
    y_refs = refs[1:1 + n]
    xs_ref, ms_ref, w_ref, op_ref, os_ref, w_scr = refs[1 + n:]
    i = pl.program_id(0)
    j = pl.program_id(1)
    col = pl.multiple_of(j * tn, LANES)

    @pl.when(i == 0)
    def _():
        w_scr[:, pl.ds(col, tn)] = w_ref[...].astype(bf16)

    w = w_scr[:, pl.ds(col, tn)]
    acc = None
    row = 0
    for y_ref, width in zip(y_refs, widths):
        part = _dot(y_ref[...], w[row:row + width])
        acc = part if acc is None else acc + part
        row += width
    op_ref[...] = xp_ref[...] + acc

    @pl.when(i == 0)
    def _():
        os_ref[:, pl.ds(col, tn)] = xs_ref[:, pl.ds(col, tn)] + _dot(ms_ref[...].astype(bf16), w)


def _outproj(xp, ys, xs, ms, w_all, layer, *, tm=2048, tn=512):
    m, d = xp.shape
    nb = xs.shape[0]
    dm = w_all.shape[1]
    widths = tuple(y.shape[1] for y in ys)
    assert sum(widths) == dm
    nj = d // tn
    return pl.pallas_call(
        functools.partial(_outproj_body, tn=tn, widths=widths),
        grid=(m // tm, d // tn),
        in_specs=[pl.BlockSpec((tm, tn), lambda i, j: (i, j))]
        + [pl.BlockSpec((tm, width), lambda i, j: (i, 0)) for width in widths]
        + [
            pl.BlockSpec((nb, d), lambda i, j: (0, 0)),
            pl.BlockSpec((nb, dm), lambda i, j: (0, 0)),
            pl.BlockSpec((None, dm, tn), lambda i, j: (layer, 0, jnp.where(i == 0, j, nj - 1))),
        ],
        out_specs=[
            pl.BlockSpec((tm, tn), lambda i, j: (i, j)),
            pl.BlockSpec((nb, d), lambda i, j: (0, 0)),
        ],
        out_shape=[jax.ShapeDtypeStruct((m, d), f32), jax.ShapeDtypeStruct((nb, d), f32)],
        scratch_shapes=[pltpu.VMEM((dm, d), bf16)],
        compiler_params=pltpu.CompilerParams(
            dimension_semantics=("arbitrary", "arbitrary"), vmem_limit_bytes=VMEM_LIMIT),
        name="outproj",
    )(xp, *ys, xs, ms, w_all)


_DECAY_SCALE = 0.6065306597126334

_P_MU, _P_W0, _P_A0, _P_KK, _P_KA, _P_RK, _P_LNW, _P_LNB = 0, 5, 6, 7, 8, 9, 10, 11


def _rwkv_chunk_stages(at, bt, kt, rt, v, wcs, consts):
    m0, gmask, bdmask, eye = consts
    m0w = jnp.concatenate([m0, m0], axis=1)
    eye_pair = jnp.where(lax.broadcasted_iota(jnp.int32, (CHUNK, LANES), 1) % CHUNK
                         == lax.broadcasted_iota(jnp.int32, (CHUNK, LANES), 0), 1.0, 0.0)
    zb64 = jnp.zeros((CHUNK, LANES), bf16)
    n = len(wcs)

    def rows(x, c):
        return x[c * CHUNK:(c + 1) * CHUNK]

    def split_heads(xb, m):
        zero = jnp.zeros_like(xb)
        return jnp.concatenate([jnp.where(m, xb, zero), jnp.where(m, zero, xb)], axis=0)

    g, p, x, mc, nc, rp, ov = {}, {}, {}, {}, {}, {}, {}

    def stage_scores(chs):
        for c in chs:
            lhs = jnp.concatenate([rows(at, c), rows(rt, c)], axis=0).astype(bf16)
            rhs = jnp.concatenate([split_heads(rows(bt, c).astype(bf16), m0),
                                   split_heads(rows(kt, c).astype(bf16), m0)], axis=0)
            g[c] = jnp.where(gmask, _dot_nt(lhs, rhs), 0.0)
            p[c] = g[c][0:CHUNK, 0:LANES]

    av, tinv = {}, {}

    def stage_level(chs, lev):
        for c in chs:
            pb = p[c].astype(bf16)
            if lev == 0:
                vs = split_heads(rows(v, c).astype(bf16), m0)
                av[c] = _dot(g[c][0:CHUNK, LANES:].astype(bf16), vs)
                tinv[c] = eye_pair + p[c]
                p[c] = _dot(pb, split_heads(pb, m0))
            elif lev < 5:
                tt_ = split_heads(tinv[c].astype(bf16), m0)
                out = _dot(pb, jnp.concatenate([split_heads(pb, m0), tt_], axis=1))
                p[c] = out[:, 0:LANES]
                tinv[c] = tinv[c] + out[:, LANES:]
            else:
                tinv[c] = tinv[c] + _dot(pb, split_heads(tinv[c].astype(bf16), m0))

    def stage_apply(chs):
        for c in chs:
            rhs = split_heads(jnp.concatenate([av[c], rows(at, c)], axis=1).astype(bf16), m0w)
            x[c] = _dot(tinv[c].astype(bf16), rhs)

    def stage_fold(chs):
        for c in chs:
            xb = x[c].astype(bf16)
            uv = xb[:, 0:LANES]
            ap = xb[:, LANES:]
            vb = rows(v, c).astype(bf16)
            tnl = jnp.concatenate([rows(bt, c) * wcs[c], rows(kt, c) * wcs[c]], axis=0).astype(bf16)
            tnr = jnp.concatenate(
                [jnp.concatenate([ap, uv], axis=1), jnp.concatenate([zb64, vb], axis=1)], axis=0)
            mn = jnp.where(bdmask, _dot_tn(tnl, tnr), 0.0)
            mc[c] = mn[:, 0:LANES] + jnp.where(eye, jnp.broadcast_to(wcs[c], (LANES, LANES)), 0.0)
            nc[c] = mn[:, LANES:]
            l2 = g[c][CHUNK:].astype(bf16)
            r2 = jnp.concatenate([
                split_heads(jnp.concatenate([ap, uv], axis=1), m0w),
                split_heads(jnp.concatenate([zb64, vb], axis=1), m0w)], axis=0)
            ro = _dot(l2, r2)
            rp[c] = rows(rt, c) + ro[:, 0:LANES]
            ov[c] = ro[:, LANES:]

    def carry(c, st):
        seq = _dot(jnp.concatenate([rp[c], mc[c]], axis=0).astype(bf16), st.astype(bf16))
        return seq[0:CHUNK] + ov[c], seq[CHUNK:] + nc[c]

    chs = list(range(n))
    stages = ([functools.partial(stage_scores, chs)]
              + [functools.partial(stage_level, chs, lev) for lev in range(6)]
              + [functools.partial(stage_apply, chs), functools.partial(stage_fold, chs)])
    return stages, carry


def _rwkv_body(*refs, tt, npr, lock):
    ng = len(POOL_WINDOWS)
    zr_ref, zk_ref, zv_ref, zg_ref, zl_ref, pp_ref, wl_ref = refs[0:7]
    pu_refs = [r.at[0] for r in refs[7:7 + ng]]
    pg_refs = [r.at[0] for r in refs[7 + ng:7 + 2 * ng]]
    pw_ref, psc_ref, y_ref, s_ref, yc_ref, sh_scr, st_scr, pool_scr = refs[7 + 2 * ng:]
    tb = pl.program_id(2)
    nt = pl.num_programs(2)
    nch = tt // CHUNK
    nsh = 4 * npr + 1
    pad = max(POOL_WINDOWS)

    @pl.when(tb == 0)
    def _():
        st_scr[...] = jnp.zeros_like(st_scr)
        for i in range(nsh):
            sh_scr[i, 7:8, :] = jnp.zeros((1, LANES), f32)
        for gi in range(ng):
            pool_scr[gi, 0:pad, :] = jnp.zeros((pad, LANES), f32)

    @pl.when(tb > 0)
    def _():
        for i in range(nsh):
            sh_scr[i, 7:8, :] = sh_scr[i, 7 + tt:8 + tt, :]
        for gi in range(ng):
            pool_scr[gi, 0:pad, :] = pool_scr[gi, tt:tt + pad, :]

    def pooling():
        pos = (tb * tt + lax.broadcasted_iota(jnp.int32, (tt, 1), 0)).astype(f32)
        for gi, w in enumerate(POOL_WINDOWS):
            u = pu_refs[gi][0]
            pool_scr[gi, pad:pad + tt, :] = u
            acc = u
            for k in range(1, w):
                acc = acc + pool_scr[gi, pl.ds(pad - k, tt), :]
            d = acc / jnp.minimum(float(w), pos + 1.0) - u
            dd = _dot(d.astype(bf16), pw_ref[gi].astype(bf16)) * psc_ref[:, gi * LANES:(gi + 1) * LANES]
            yc_ref[0, :, gi * LANES:(gi + 1) * LANES] = (dd * _silu(pg_refs[gi][0])).astype(bf16)
            yield

    def shifted(idx, z, mu):
        sh_scr[idx, 8:8 + tt, :] = z
        zprev = sh_scr[idx, pl.ds(7, tt), :]
        return z + (zprev - z) * mu

    m0t = _pair_consts(tt)
    ones_bd = _ones_blockdiag()
    lo = shifted(4 * npr, zl_ref[0, 0], pp_ref[0, _P_MU + 4:_P_MU + 5])
    xl = jnp.where(m0t, jnp.tanh(lo), lo).astype(bf16)

    m0 = _pair_consts(CHUNK)
    gr = lax.broadcasted_iota(jnp.int32, (2 * CHUNK, 2 * LANES), 0)
    gc = lax.broadcasted_iota(jnp.int32, (2 * CHUNK, 2 * LANES), 1)
    gt = gr % CHUNK
    gs = gc % CHUNK
    gmask = (gt > gs) | ((gr >= CHUNK) & (gt == gs))
    br = lax.broadcasted_iota(jnp.int32, (LANES, 2 * LANES), 0)
    bc = lax.broadcasted_iota(jnp.int32, (LANES, 2 * LANES), 1)
    bdmask = (br < HEAD) == ((bc % LANES) < HEAD)
    er = lax.broadcasted_iota(jnp.int32, (LANES, LANES), 0)
    ec = lax.broadcasted_iota(jnp.int32, (LANES, LANES), 1)
    eye = er == ec
    consts = (m0, gmask, bdmask, eye)
    tr = lax.broadcasted_iota(jnp.int32, (CHUNK, CHUNK), 0)
    tc = lax.broadcasted_iota(jnp.int32, (CHUNK, CHUNK), 1)
    tril = jnp.where(tc <= tr, 1.0, 0.0).astype(bf16)
    tril3 = jnp.concatenate([tril, tril, tril], axis=1)

    def prepare(s, out):
        pp = pp_ref[s]
        r, k, v, g = [shifted(4 * s + i, ref[s, 0], pp[_P_MU + i:_P_MU + i + 1])
                      for i, ref in enumerate((zr_ref, zk_ref, zv_ref, zg_ref))]
        la = _dot(xl, wl_ref[s].astype(bf16))
        kk = k * pp[_P_KK:_P_KK + 1]
        kk_ss = _segsum(kk * kk, ones_bd)
        yield
        ld = -_DECAY_SCALE * jax.nn.sigmoid(pp[_P_W0:_P_W0 + 1] + la[:, 0:LANES])
        a = jax.nn.sigmoid(pp[_P_A0:_P_A0 + 1] + la[:, LANES:])
        kk = kk * jnp.minimum(lax.rsqrt(kk_ss), 1e12)
        k2 = k * (1.0 + (a - 1.0) * pp[_P_KA:_P_KA + 1])
        h3 = _split3(ld)
        cl = jnp.concatenate([
            _dot(tril3, jnp.concatenate([h[c * CHUNK:(c + 1) * CHUNK] for h in h3], axis=0))
            for c in range(nch)], axis=0)
        bonus = _segsum(r * k2 * pp[_P_RK:_P_RK + 1], ones_bd) * v
        yield
        e_in = jnp.exp(cl)
        e_neg = jnp.exp(-cl)
        wcs = [e_in[(c + 1) * CHUNK - 1:(c + 1) * CHUNK] for c in range(nch)]
        out["stages"], out["carry"] = _rwkv_chunk_stages(
            -kk * jnp.exp(cl - ld), kk * a * e_neg, k2 * e_neg, r * e_in, v, wcs, consts)
        out["epi"] = (pp, g, bonus)

    def finish(s, outs, st, epi):
        pp, g, bonus = epi
        st_scr[s] = st
        o = jnp.concatenate(outs, axis=0)
        mean = _segsum(o, ones_bd) * (1.0 / HEAD)
        yield
        dl = o - mean
        var = _segsum(dl * dl, ones_bd) * (1.0 / HEAD)
        yield
        on = dl * lax.rsqrt(var + GN_EPS) * pp[_P_LNW:_P_LNW + 1] + pp[_P_LNB:_P_LNB + 1] + bonus
        y_ref[0, :, s * LANES:(s + 1) * LANES] = (on * _silu(g)).astype(bf16)

    def drain(gen):
        for _ in gen:
            pass

    def lockstep(gens):
        gens = list(gens)
        while gens:
            gens = [g for g in gens if next(g, StopIteration) is not StopIteration]
            yield

    preps = [dict() for _ in range(npr)]
    units = [list(range(i, min(i + lock, npr))) for i in range(0, npr, lock)]
    drain(lockstep(prepare(s, preps[s]) for s in units[0]))
    prev = None
    closing = iter(())
    side = pooling()
    for ui, unit in enumerate(units):
        nxt = (lockstep(prepare(s, preps[s]) for s in units[ui + 1]) if ui + 1 < len(units) else iter(()))
        for stage_row in zip(*[preps[s]["stages"] for s in unit]):
            for stage in stage_row:
                stage()
            if prev is not None:
                for cur in prev:
                    if cur["todo"]:
                        cur["step"]()
            next(closing, None)
            next(nxt, None)
            if ui > 0:
                next(side, None)
        drain(closing)
        drain(nxt)
        if prev is not None:
            while any(cur["todo"] for cur in prev):
                for cur in prev:
                    if cur["todo"]:
                        cur["step"]()
            closing = lockstep(finish(cur["s"], cur["outs"], cur["st"][0], cur["epi"]) for cur in prev)
        prev = []
        for s in unit:
            cur = {"s": s, "outs": [], "st": [st_scr[s]], "epi": preps[s]["epi"], "todo": list(range(nch))}

            def step(cur=cur, carry=preps[s]["carry"]):
                o_c, cur["st"][0] = carry(cur["todo"].pop(0), cur["st"][0])
                cur["outs"].append(o_c)

            cur["step"] = step
            prev.append(cur)
    while any(cur["todo"] for cur in prev):
        for cur in prev:
            if cur["todo"]:
                cur["step"]()
        next(closing, None)
    drain(closing)
    drain(side)
    drain(lockstep(finish(cur["s"], cur["outs"], cur["st"][0], cur["epi"]) for cur in prev))

    @pl.when(tb == nt - 1)
    def _():
        for s in range(npr):
            stt = st_scr[s].T
            s_ref[0, 2 * s] = stt[0:HEAD, 0:HEAD]
            s_ref[0, 2 * s + 1] = stt[HEAD:, HEAD:]


def _rwkv_pool_prompt(z3, pp, wl, pw, psc, layer, *, tt=512, npr=6, lock=2):
    _, b, t, _ = z3.shape
    npair = RWKV_HEADS // 2
    ngrp = npair // npr
    assert ngrp == 1, "the pooling columns are produced once per (batch row, position tile)"
    wide = npr * LANES
    ng = len(POOL_WINDOWS)

    def zspec(part):
        return pl.BlockSpec((npr, 1, tt, LANES), lambda bi, p, tb, part=part: (part * ngrp + p, bi, tb, 0))

    lora_blk = 4 * RWKV_W // LANES
    return pl.pallas_call(
        functools.partial(_rwkv_body, tt=tt, npr=npr, lock=lock),
        grid=(b, ngrp, t // tt),
        in_specs=[
            zspec(0), zspec(1), zspec(2), zspec(3),
            pl.BlockSpec((1, 1, tt, LANES), lambda bi, p, tb: (lora_blk, bi, tb, 0)),
            pl.BlockSpec((None, npr, 12, LANES), lambda bi, p, tb: (layer, p, 0, 0)),
            pl.BlockSpec((None, npr, LANES, 2 * LANES), lambda bi, p, tb: (layer, p, 0, 0)),
        ]
        + [pl.BlockSpec((1, 1, tt, LANES), lambda bi, p, tb, off=POOL_BLK0 + i: (off, bi, tb, 0))
           for i in range(2 * ng)]
        + [
            pl.BlockSpec((None, ng, LANES, LANES), lambda bi, p, tb: (layer, 0, 0, 0)),
            pl.BlockSpec((None, 1, POOL_W), lambda bi, p, tb: (layer, 0, 0)),
        ],
        out_specs=[
            pl.BlockSpec((1, tt, wide), lambda bi, p, tb: (bi, tb, p)),
            pl.BlockSpec((1, 2 * npr, HEAD, HEAD), lambda bi, p, tb: (bi, p, 0, 0)),
            pl.BlockSpec((1, tt, POOL_W), lambda bi, p, tb: (bi, tb, 0)),
        ],
        out_shape=[jax.ShapeDtypeStruct((b, t, RWKV_W), bf16),
                   jax.ShapeDtypeStruct((b, RWKV_HEADS, HEAD, HEAD), f32),
                   jax.ShapeDtypeStruct((b, t, POOL_W), bf16)],
        scratch_shapes=[pltpu.VMEM((4 * npr + 1, tt + 8, LANES), f32), pltpu.VMEM((npr, LANES, LANES), f32),
                        pltpu.VMEM((ng, tt + max(POOL_WINDOWS), LANES), f32)],
        compiler_params=pltpu.CompilerParams(
            dimension_semantics=("arbitrary", "arbitrary", "arbitrary"), vmem_limit_bytes=VMEM_LIMIT),
        name="rwkv_pool_prompt",
    )(*([z3] * 5), pp, wl, *([z3] * (2 * ng)), pw, psc)


def _rope_pair(x, cos, sa, sb):
    return x * cos + pltpu.roll(x, LANES - ROPE_DIMS // 2, 1) * sa + pltpu.roll(x, ROPE_DIMS // 2, 1) * sb


def _attn_body(*refs, t):
    qkvg = [[r.at[0] for r in refs[4 * gi:4 * gi + 4]] for gi in range(3)]
    cos_ref, sin_ref, qnw_ref, knw_ref = refs[12:16]
    y_refs = refs[16:19]
    kv_refs = [refs[19 + 2 * gi:21 + 2 * gi] for gi in range(3)]
    qn_scr, kn_scr, o_scr, lse_scr = refs[25:29]

    rb = 256
    ones_bd = _ones_blockdiag()
    m0q = _pair_consts(QBLK)
    scale = HEAD ** -0.5
    half = ROPE_DIMS // 2
    pj = lax.broadcasted_iota(jnp.int32, (LANES, LANES), 0)
    pi = lax.broadcasted_iota(jnp.int32, (LANES, LANES), 1)
    pin = pi % HEAD
    perm = jnp.where(((pin < half) & (pj == pi + half)) | ((pin >= half) & (pin < ROPE_DIMS) & (pj == pi - half)),
                     1.0, 0.0).astype(bf16)
    zpad = jnp.zeros((LANES, LANES), bf16)
    sum_swap = jnp.concatenate([jnp.concatenate([ones_bd, zpad], axis=1),
                                jnp.concatenate([zpad, perm], axis=1)], axis=0)

    for gi, (window, dil) in enumerate(ATT_GROUPS):
        q_ref, k_ref, v_ref, g_ref = qkvg[gi]
        pk_ref, pv_ref = kv_refs[gi]
        keep = min(window, t)

        def norm_rows(i, carry, q_ref=q_ref, k_ref=k_ref):
            jobs = []
            for u in range(2):
                rows = pl.ds(pl.multiple_of((2 * i + u) * rb, rb), rb)
                for src, nw, dst in ((q_ref, qnw_ref, qn_scr), (k_ref, knw_ref, kn_scr)):
                    x = src[0, rows, :]
                    y = x * nw[...]
                    hi, lo = _split2(jnp.concatenate([x * x, y], axis=1))
                    jobs.append((y, _dot(hi, sum_swap) + _dot(lo, sum_swap), dst, rows))
            for y, res, dst, rows in jobs:
                rs = lax.rsqrt(res[:, 0:LANES] * (1.0 / HEAD) + NORM_EPS)
                dst[rows, :] = (y * cos_ref[rows, :] + res[:, LANES:] * sin_ref[rows, :]) * rs
            return carry

        lax.fori_loop(0, t // (2 * rb), norm_rows, 0)
        pk_ref[0] = kn_scr[t - keep:t, :]
        pv_ref[0] = v_ref[0, t - keep:t, :]

        n_sub = t // dil
        n_blk = n_sub // QBLK

        def tiles(starts, nk, first, v_ref=v_ref, dil=dil, gi=gi):
            qi = lax.broadcasted_iota(jnp.int32, (QBLK, nk), 0)
            kj = lax.broadcasted_iota(jnp.int32, (QBLK, nk), 1)
            mask = (kj <= qi) if first else ((kj >= qi) & (kj <= qi + QBLK))
            scores, vts = [], []
            for q0, k0 in starts:
                qt = qn_scr[pl.ds(q0, QBLK, stride=dil), :] * scale
                kt = kn_scr[pl.ds(k0, nk, stride=dil), :].astype(bf16)
                vts.append(v_ref[0, pl.ds(k0, nk, stride=dil), :].astype(bf16))
                for hh in range(2):
                    qh = jnp.where(m0q, qt, 0.0) if hh == 0 else jnp.where(m0q, 0.0, qt)
                    scores.append(_dot_nt(qh.astype(bf16), kt))
            probs, sums, lses = [], [], []
            for s in scores:
                s = jnp.where(mask, s, -jnp.inf)
                m = jnp.max(s, axis=-1, keepdims=True)
                p = jnp.exp(s - m)
                l = jnp.sum(p, axis=-1, keepdims=True)
                probs.append(p.astype(bf16))
                sums.append(l)
                lses.append(m + jnp.log(l))
            for ti, (q0, _) in enumerate(starts):
                oh = [_dot(probs[2 * ti + hh], vts[ti]) / sums[2 * ti + hh] for hh in range(2)]
                o_scr[gi, pl.ds(q0, QBLK, stride=dil), :] = jnp.where(m0q, oh[0], oh[1])
                lse_scr[gi, pl.ds(q0, QBLK, stride=dil), :] = jnp.where(
                    m0q, jnp.broadcast_to(lses[2 * ti], (QBLK, LANES)),
                    jnp.broadcast_to(lses[2 * ti + 1], (QBLK, LANES)))

        per_first = min(8, dil)

        def first_tiles(i, carry, tiles=tiles, per=per_first):
            tiles([(i * per + u, i * per + u) for u in range(per)], QBLK, True)
            return carry

        lax.fori_loop(0, dil // per_first, first_tiles, 0)

        if n_blk > 1:
            n_later = dil * (n_blk - 1)
            per_later = max(p for p in range(1, 7) if n_later % p == 0)

            def later_tiles(i, carry, tiles=tiles, dil=dil, per=per_later):
                starts = []
                for u in range(per):
                    idx = i * per + u
                    r = idx % dil
                    blk = idx // dil + 1
                    starts.append((r + blk * QBLK * dil, r + (blk - 1) * QBLK * dil))
                tiles(starts, 2 * QBLK, False)
                return carry

            lax.fori_loop(0, dil * (n_blk - 1) // per_later, later_tiles, 0)

    def combine(i, carry):
        rows = pl.ds(pl.multiple_of(i * rb, rb), rb)
        ls = [lse_scr[gi, rows, :] for gi in range(3)]
        mx = jnp.maximum(jnp.maximum(ls[0], ls[1]), ls[2])
        es = [jnp.exp(l - mx) for l in ls]
        inv = 1.0 / (es[0] + es[1] + es[2])
        for gi in range(3):
            gate = qkvg[gi][3][0, rows, :]
            y_refs[gi][0, rows, :] = (o_scr[gi, rows, :] * (es[gi] * inv) * _silu(gate)).astype(bf16)
        return carry

    lax.fori_loop(0, t // rb, combine, 0)


def _attn_prompt(z3, cos, sin, qnw, knw, layer):
    _, b, t, _ = z3.shape
    in_specs = []
    for gi in range(3):
        for part in range(4):
            off = ATT_BLK0 + part * 6 + 2 * gi
            in_specs.append(pl.BlockSpec((1, 1, t, LANES), lambda bi, jp, off=off: (off + jp, bi, 0, 0)))
    in_specs += [pl.BlockSpec((t, LANES), lambda bi, jp: (0, 0))] * 2
    in_specs += [pl.BlockSpec((None, 1, LANES), lambda bi, jp: (layer, 0, 0))] * 2
    out_specs = [pl.BlockSpec((1, t, LANES), lambda bi, jp: (bi, 0, jp))] * 3
    out_shape = [jax.ShapeDtypeStruct((b, t, 4 * HEAD), bf16)] * 3
    for window, _ in ATT_GROUPS:
        keep = min(window, t)
        out_specs += [pl.BlockSpec((1, keep, LANES), lambda bi, jp: (bi, 0, jp))] * 2
        out_shape += [jax.ShapeDtypeStruct((b, keep, 4 * HEAD), f32)] * 2
    outs = pl.pallas_call(
        functools.partial(_attn_body, t=t),
        grid=(b, 2),
        in_specs=in_specs,
        out_specs=out_specs,
        out_shape=out_shape,
        scratch_shapes=[pltpu.VMEM((t, LANES), f32), pltpu.VMEM((t, LANES), f32),
                        pltpu.VMEM((3, t, LANES), f32), pltpu.VMEM((3, t, LANES), f32)],
        compiler_params=pltpu.CompilerParams(
            dimension_semantics=("arbitrary", "arbitrary"), vmem_limit_bytes=VMEM_LIMIT),
        name="attn_prompt",
    )(*([z3] * 12), cos, sin, qnw, knw)
    return outs


def _bcast8(x):
    return jnp.broadcast_to(x, (8, x.shape[-1]))


def _sample_body(*refs, past_len, nbs):
    gens = [_sample_one(bi, *refs, past_len=past_len) for bi in range(nbs)]
    while gens:
        gens = [g for g in gens if next(g, StopIteration) is not StopIteration]


def _sample_one(bi, z_ref, sh_ref, wkv_ref, pool_ref, ck0, cv0, ck1, cv1, ck2, cv2,
                mu_ref, rp_ref, wup_ref, aup_ref, qnw_ref, knw_ref, cos_ref, sa_ref, sb_ref,
                pw_ref, psc_ref,
                mix_ref, swkv_ref, spool_ref, sk0, sv0, sk1, sv1, sk2, sv2,
                qk_scr, *, past_len):
    z = z_ref[bi]
    za = z[:, 0:RWKV_COLS]
    zb = z[:, RWKV_COLS:RWKV_COLS + ATT_COLS]
    zc = z[:, RWKV_COLS + ATT_COLS:RWKV_COLS + ATT_COLS + 2 * POOL_W]
    er = lax.broadcasted_iota(jnp.int32, (HEAD, HEAD), 0)
    ec = lax.broadcasted_iota(jnp.int32, (HEAD, HEAD), 1)
    eye = er == ec

    zs = za + (sh_ref[bi] - za) * mu_ref[...]
    w_ = RWKV_W
    r, k, v, g = (zs[:, i * w_:(i + 1) * w_] for i in range(4))
    w_dn = zs[:, 4 * w_:4 * w_ + LORA]
    a_dn = zs[:, 4 * w_ + LORA:]
    rp = rp_ref[...]
    w0, a0, k_k, k_a, r_k, ln_w, ln_b = (rp[i:i + 1] for i in range(7))
    lw = _dot(_bcast8(jnp.tanh(w_dn)).astype(bf16), wup_ref[...].astype(bf16))[0:1]
    la = _dot(_bcast8(a_dn).astype(bf16), aup_ref[...].astype(bf16))[0:1]
    yield
    decay = jnp.exp(-_DECAY_SCALE * jax.nn.sigmoid(w0 + lw))
    a = jax.nn.sigmoid(a0 + la)
    kk = k * k_k
    k2 = k * (1.0 + (a - 1.0) * k_a)
    heads = range(RWKV_HEADS)
    hsl = [slice(h * HEAD, (h + 1) * HEAD) for h in heads]

    def lane_sum(x):
        return jnp.sum(x, axis=-1, keepdims=True)

    kk_ss = [lane_sum(kk[:, hs] * kk[:, hs]) for hs in hsl]
    bonus = [lane_sum(r[:, hs] * k2[:, hs] * r_k[:, hs]) for hs in hsl]
    v_col = [lane_sum(jnp.where(eye, jnp.broadcast_to(v[:, hs], (HEAD, HEAD)), 0.0)) for hs in hsl]
    yield
    kkn = [kk[:, hs] / jnp.maximum(jnp.sqrt(ss), 1e-12) for hs, ss in zip(hsl, kk_ss)]
    sa_col = [lane_sum(wkv_ref[bi, h] * (-kkn[h])) for h in heads]
    yield
    sn = [wkv_ref[bi, h] * decay[:, hsl[h]] + sa_col[h] * (kkn[h] * a[:, hsl[h]]) + v_col[h] * k2[:, hsl[h]]
          for h in heads]
    for h in heads:
        swkv_ref[bi, h] = sn[h]
    o_col = [lane_sum(sn[h] * r[:, hsl[h]]) for h in heads]
    yield
    o = [jnp.sum(jnp.where(eye, jnp.broadcast_to(oc, (HEAD, HEAD)), 0.0), axis=0, keepdims=True) for oc in o_col]
    mean = [jnp.mean(x, axis=-1, keepdims=True) for x in o]
    yield
    dl = [x - mu_ for x, mu_ in zip(o, mean)]
    var = [jnp.mean(x * x, axis=-1, keepdims=True) for x in dl]
    yield
    for h in heads:
        hs = hsl[h]
        on = dl[h] * lax.rsqrt(var[h] + GN_EPS) * ln_w[:, hs] + ln_b[:, hs] + bonus[h] * v[:, hs]
        mix_ref[bi, :, hs] = on * _silu(g[:, hs])

    aw = ATT_W
    q, kx, vx, gx = (zb[:, i * aw:(i + 1) * aw] for i in range(4))
    m0 = _pair_consts(8)
    for hp in range(ATT_W // LANES):
        ls = slice(hp * LANES, (hp + 1) * LANES)
        for idx, (src, nw) in enumerate(((q, qnw_ref), (kx, knw_ref))):
            x = _bcast8(src[:, ls])
            sq = x * x
            s0 = jnp.sum(jnp.where(m0, sq, 0.0), axis=-1, keepdims=True)
            s1 = jnp.sum(jnp.where(m0, 0.0, sq), axis=-1, keepdims=True)
            ms = jnp.where(m0, s0, s1) * (1.0 / HEAD)
            xn = x * lax.rsqrt(ms + NORM_EPS) * nw[...]
            qk_scr[bi, idx, :, ls] = _rope_pair(xn, cos_ref[...], sa_ref[...], sb_ref[...])
    yield
    qn = qk_scr[bi, 0, 0:1, :]
    kn = qk_scr[bi, 1, 0:1, :]
    scale = HEAD ** -0.5

    def heads4(row, gi):
        return jnp.concatenate([row[:, (gi * 4 + j) * HEAD:(gi * 4 + j + 1) * HEAD] for j in range(4)], axis=0)

    def to_col(row):
        return jnp.sum(jnp.where(eye, jnp.broadcast_to(row, (HEAD, HEAD)), 0.0), axis=-1, keepdims=True)

    def to_row(col):
        return jnp.sum(jnp.where(eye, jnp.broadcast_to(col, (HEAD, HEAD)), 0.0), axis=0, keepdims=True)

    caches = ((ck0, cv0, sk0, sv0), (ck1, cv1, sk1, sv1), (ck2, cv2, sk2, sv2))
    windows = []
    for gi, (ck, cv, sk, sv) in enumerate(caches):
        sk[bi, 0] = heads4(kn, gi)
        sv[bi, 0] = heads4(vx, gi)
        row_id = lax.broadcasted_iota(jnp.int32, (1, ck.shape[-1]), 1)
        windows.append((row_id % ATT_GROUPS[gi][1]) == 0)
    combos = [(gi, j) for gi in range(3) for j in range(4)]
    asl = [slice((gi * 4 + j) * HEAD, (gi * 4 + j + 1) * HEAD) for gi, j in combos]
    qh = [qn[:, hs] * scale for hs in asl]
    q_col = [to_col(x) for x in qh]
    s_new = [jnp.sum(kn[:, hs] * x, axis=-1, keepdims=True) for hs, x in zip(asl, qh)]
    yield
    s_all = [jnp.where(windows[gi], jnp.sum(caches[gi][0][bi, j] * qc, axis=0, keepdims=True), -jnp.inf)
             for (gi, j), qc in zip(combos, q_col)]
    m_all = [jnp.maximum(jnp.max(s, axis=-1, keepdims=True), sn_) for s, sn_ in zip(s_all, s_new)]
    yield
    p_all = [jnp.exp(s - m) for s, m in zip(s_all, m_all)]
    p_new = [jnp.exp(sn_ - m) for sn_, m in zip(s_new, m_all)]
    l_all = [jnp.sum(p, axis=-1, keepdims=True) + pn for p, pn in zip(p_all, p_new)]
    o_colv = [jnp.sum(caches[gi][1][bi, j] * p, axis=-1, keepdims=True) for (gi, j), p in zip(combos, p_all)]
    yield
    o_rows = [(to_row(oc) + pn * vx[:, hs]) / l for oc, pn, hs, l in zip(o_colv, p_new, asl, l_all)]
    lse = [m + jnp.log(l) for m, l in zip(m_all, l_all)]
    for j in range(4):
        mx = jnp.maximum(jnp.maximum(lse[j], lse[4 + j]), lse[8 + j])
        es = [jnp.exp(lse[gi * 4 + j] - mx) for gi in range(3)]
        inv = 1.0 / (es[0] + es[1] + es[2])
        for gi in range(3):
            ci = gi * 4 + j
            mix_ref[bi, :, RWKV_W + ci * HEAD:RWKV_W + (ci + 1) * HEAD] = (
                o_rows[ci] * (es[gi] * inv) * _silu(gx[:, asl[ci]]))

    u = zc[:, 0:POOL_W]
    gate = zc[:, POOL_W:]
    prev = pool_ref[bi]
    for gi, w in enumerate(POOL_WINDOWS):
        cs = slice(gi * LANES, (gi + 1) * LANES)
        ug = u[:, cs]
        sw = jnp.sum(prev[POOL_BUF - (w - 1):POOL_BUF, cs], axis=0, keepdims=True) + ug
        cnt = min(float(w), float(past_len) + 1.0)
        d = sw / cnt - ug
        dd = _dot(_bcast8(d).astype(bf16), pw_ref[gi].astype(bf16))[0:1] * psc_ref[:, cs]
        mix_ref[bi, :, RWKV_W + ATT_W + gi * LANES:RWKV_W + ATT_W + (gi + 1) * LANES] = dd * _silu(gate[:, cs])
    spool_ref[bi, 0:POOL_BUF - 1, :] = prev[1:POOL_BUF]
    spool_ref[bi, POOL_BUF - 1:POOL_BUF, :] = u


def _sample_step(zs, sh_all, wkv_all, pool_all, caches_all, layer, mu, rp, wup, aup, qnw, knw, cos, sa, sb,
                 pw, psc, *, past_len, nbs=2):
    nb = zs.shape[0]
    d_in = zs.shape[-1]
    d_mix = RWKV_W + ATT_W + POOL_W

    def full(shape):
        nd = len(shape)
        return pl.BlockSpec(shape, lambda b, nd=nd: (0,) * nd)

    assert nb % nbs == 0
    in_specs = [
        pl.BlockSpec((nbs, 1, d_in), lambda b: (b, 0, 0)),
        pl.BlockSpec((None, nbs, 1, RWKV_COLS), lambda b: (layer, b, 0, 0)),
        pl.BlockSpec((None, nbs, RWKV_HEADS, HEAD, HEAD), lambda b: (layer, b, 0, 0, 0)),
        pl.BlockSpec((None, nbs, POOL_BUF, POOL_W), lambda b: (layer, b, 0, 0)),
    ]
    in_specs += [pl.BlockSpec((None, nbs, 4, HEAD, c.shape[-1]), lambda b: (layer, b, 0, 0, 0))
                 for c in caches_all]

    def of_layer(x):
        nd = x.ndim - 1
        return pl.BlockSpec((None,) + x.shape[1:], lambda b, nd=nd: (layer,) + (0,) * nd)

    in_specs += [of_layer(mu), of_layer(rp), of_layer(wup), of_layer(aup), of_layer(qnw), of_layer(knw),
                 full(cos.shape), full(sa.shape), full(sb.shape), of_layer(pw), of_layer(psc)]
    out_specs = [
        pl.BlockSpec((nbs, 1, d_mix), lambda b: (b, 0, 0)),
        pl.BlockSpec((nbs, RWKV_HEADS, HEAD, HEAD), lambda b: (b, 0, 0, 0)),
        pl.BlockSpec((nbs, POOL_BUF, POOL_W), lambda b: (b, 0, 0)),
    ] + [pl.BlockSpec((nbs, 1, 4, HEAD), lambda b: (b, 0, 0, 0))] * 6
    out_shape = [
        jax.ShapeDtypeStruct((nb, 1, d_mix), f32),
        jax.ShapeDtypeStruct((nb, RWKV_HEADS, HEAD, HEAD), f32),
        jax.ShapeDtypeStruct((nb, POOL_BUF, POOL_W), f32),
    ] + [jax.ShapeDtypeStruct((nb, 1, 4, HEAD), f32)] * 6
    return pl.pallas_call(
        functools.partial(_sample_body, past_len=past_len, nbs=nbs),
        grid=(nb // nbs,),
        in_specs=in_specs,
        out_specs=out_specs,
        out_shape=out_shape,
        scratch_shapes=[pltpu.VMEM((nbs, 2, 8, ATT_W), f32)],
        compiler_params=pltpu.CompilerParams(
            dimension_semantics=("arbitrary",), vmem_limit_bytes=VMEM_LIMIT),
        name="sample_step",
    )(zs, sh_all, wkv_all, pool_all, *caches_all, mu, rp, wup, aup, qnw, knw, cos, sa, sb, pw, psc)


def _rope_tables(pos):
    half = ROPE_DIMS // 2
    inv = jnp.power(jnp.float32(ROPE_THETA), -jnp.arange(half, dtype=f32) * 2.0 / ROPE_DIMS)
    ang = pos[:, None] * inv[None, :]
    cos, sin = jnp.cos(ang), jnp.sin(ang)
    n = pos.shape[0]
    pad = jnp.zeros((n, HEAD - ROPE_DIMS), f32)
    zero = jnp.zeros((n, half), f32)
    c_head = jnp.concatenate([cos, cos, pad + 1.0], axis=1)
    a_head = jnp.concatenate([-sin, zero, pad], axis=1)
    b_head = jnp.concatenate([zero, sin, pad], axis=1)
    return tuple(jnp.concatenate([x, x], axis=1) for x in (c_head, a_head, b_head))


def kernel(x_prompt, x_sample, state_wkv, state_shift, state_pool,
           cache_k_w128, cache_v_w128, cache_k_w512, cache_v_w512, cache_k_w2048, cache_v_w2048,
           norm_w, w_in, w_out, rwkv_mu, rwkv_w0, rwkv_w_up, rwkv_a0, rwkv_a_up,
           rwkv_k_k, rwkv_k_a, rwkv_r_k, rwkv_ln_w, rwkv_ln_b, q_norm_w, k_norm_w, pool_w, pool_scale):
    b, t, d = x_prompt.shape
    nb, ts, _ = x_sample.shape
    depth = w_in.shape[0]
    d_in = w_in.shape[2]
    assert ts == 1 and t % 512 == 0
    past_len = 16384
    caches_in = ((cache_k_w128, cache_v_w128), (cache_k_w512, cache_v_w512), (cache_k_w2048, cache_v_w2048))
    for (window, _), (ck, _) in zip(ATT_GROUPS, caches_in):
        assert ck.shape[2] == window, "window buffers are expected to be full"

    cos_p, sa_p, sb_p = _rope_tables(jnp.arange(t, dtype=f32))
    cos_s, sa_s, sb_s = _rope_tables(past_len + jnp.arange(1, dtype=f32))

    hp = x_prompt.reshape(b * t, d)
    hs = x_sample.reshape(nb, d)
    npair = RWKV_HEADS // 2
    p_out = [[] for _ in range(9)]
    s_out = [[] for _ in range(9)]
    sh_all = state_shift.reshape(depth, nb, 1, RWKV_COLS)
    caches_all = [jnp.transpose(c, (0, 1, 3, 4, 2)) for pair in caches_in for c in pair]

    nw_all = norm_w.reshape(depth, 1, d)
    mu_all = rwkv_mu.reshape(depth, 1, RWKV_COLS)
    mu4 = rwkv_mu[:, :4 * RWKV_W].reshape(depth, 4, npair, LANES).transpose(0, 2, 1, 3)
    mul = jnp.broadcast_to(rwkv_mu[:, 4 * RWKV_W:].reshape(depth, 1, 1, LANES), (depth, npair, 1, LANES))
    vecs_all = jnp.stack([rwkv_w0, rwkv_a0, rwkv_k_k, rwkv_k_a, rwkv_r_k.reshape(depth, -1),
                          rwkv_ln_w, rwkv_ln_b], axis=1)
    pp_all = jnp.concatenate(
        [mu4, mul, vecs_all.reshape(depth, 7, npair, LANES).transpose(0, 2, 1, 3)], axis=2)
    wup = rwkv_w_up.reshape(depth, LORA, npair, LANES).transpose(0, 2, 1, 3)
    aup = rwkv_a_up.reshape(depth, LORA, npair, LANES).transpose(0, 2, 1, 3)
    zl = jnp.zeros_like(wup)
    wl_all = jnp.concatenate(
        [jnp.concatenate([wup, zl], axis=3), jnp.concatenate([zl, aup], axis=3)], axis=2)
    qnw_all = jnp.tile(q_norm_w, (1, 2)).reshape(depth, 1, LANES)
    knw_all = jnp.tile(k_norm_w, (1, 2)).reshape(depth, 1, LANES)
    psc_all = pool_scale.reshape(depth, 1, POOL_W)
    sin_p = sa_p + sb_p

    for l in range(depth):
        zp, zs = _inproj(hp, hs, nw_all, w_in, l)
        z3 = zp.reshape(d_in // LANES, b, t, LANES)

        ya, p_wkv, yc = _rwkv_pool_prompt(z3, pp_all, wl_all, pool_w, psc_all, l)
        att = _attn_prompt(z3, cos_p, sin_p, qnw_all, knw_all, l)

        sample = _sample_step(
            zs.reshape(nb, 1, -1), sh_all, state_wkv, state_pool, caches_all, l,
            mu_all, vecs_all, rwkv_w_up, rwkv_a_up, qnw_all, knw_all, cos_s, sa_s, sb_s,
            pool_w, psc_all, past_len=past_len)
        mix_s = sample[0].reshape(nb, -1)

        ys = [ya.reshape(b * t, RWKV_W)] + [y.reshape(b * t, 4 * HEAD) for y in att[:3]] + [yc.reshape(b * t, POOL_W)]
        hp, hs = _outproj(hp, ys, hs, mix_s, w_out, l)

        p_out[0].append(p_wkv)
        p_out[1].append(z3[:ATT_BLK0, :, -1, :].transpose(1, 0, 2).reshape(b, RWKV_COLS))
        p_out[2].append(z3[POOL_BLK0:POOL_BLK0 + POOL_W // LANES, :, t - POOL_BUF:, :]
                        .transpose(1, 2, 0, 3).reshape(b, POOL_BUF, POOL_W))
        for gi in range(3):
            keep = att[3 + 2 * gi].shape[1]
            p_out[3 + 2 * gi].append(att[3 + 2 * gi].reshape(b, keep, 4, HEAD))
            p_out[4 + 2 * gi].append(att[4 + 2 * gi].reshape(b, keep, 4, HEAD))
        s_out[0].append(sample[1])
        s_out[1].append(zs[:, :RWKV_COLS])
        s_out[2].append(sample[2])
        for i in range(6):
            s_out[3 + i].append(sample[3 + i])

    return (hp.reshape(b, t, d), hs.reshape(nb, 1, d),
            *[jnp.stack(x) for x in p_out], *[jnp.stack(x) for x in s_out])
```

```python
import functools

import jax
import jax.numpy as jnp
from jax import lax
from jax.experimental import pallas as pl
from jax.experimental.pallas import tpu as pltpu

f32 = jnp.float32
bf16 = jnp.bfloat16

HEAD = 64
LANES = 128
NORM_EPS = 1e-6
GN_EPS = 64e-5
RWKV_HEADS = 12
RWKV_W = RWKV_HEADS * HEAD
LORA = 64
RWKV_COLS = 4 * RWKV_W + 2 * LORA
ATT_GROUPS = ((128, 1), (512, 4), (2048, 16))
ATT_W = 12 * HEAD
ATT_COLS = 4 * ATT_W
ROPE_THETA = 500000.0
ROPE_DIMS = 16
QBLK = 128
POOL_WINDOWS = (2, 4, 8, 16)
POOL_W = 512
POOL_BUF = 15
PAST_LEN = 16384
CHUNK = 64
VMEM_LIMIT = 56 * 1024 * 1024

ATT_BLK0 = RWKV_COLS // LANES
POOL_BLK0 = (RWKV_COLS + ATT_COLS) // LANES


def _dot(a, b):
    return jnp.dot(a, b, preferred_element_type=f32)


def _dot_nt(a, b):
    return lax.dot_general(a, b, (((1,), (1,)), ((), ())), preferred_element_type=f32)


def _dot_tn(a, b):
    return lax.dot_general(a, b, (((0,), (0,)), ((), ())), preferred_element_type=f32)


def _split2(x):
    hi = x.astype(bf16)
    lo = (x - hi.astype(f32)).astype(bf16)
    return hi, lo


def _split3(x):
    hi = x.astype(bf16)
    r1 = x - hi.astype(f32)
    mid = r1.astype(bf16)
    lo = (r1 - mid.astype(f32)).astype(bf16)
    return hi, mid, lo


def _silu(x):
    return x * jax.nn.sigmoid(x)


def _rms(x, w):
    ms = jnp.mean(x * x, axis=-1, keepdims=True)
    return x * lax.rsqrt(ms + NORM_EPS) * w


def _pair_consts(n):
    lane = lax.broadcasted_iota(jnp.int32, (n, LANES), 1)
    return lane < HEAD


def _ones_blockdiag():
    r = lax.broadcasted_iota(jnp.int32, (LANES, LANES), 0)
    c = lax.broadcasted_iota(jnp.int32, (LANES, LANES), 1)
    return jnp.where((r < HEAD) == (c < HEAD), 1.0, 0.0).astype(bf16)


def _segsum(x, ones_bd):
    hi, lo = _split2(x)
    return _dot(jnp.concatenate([hi, lo], axis=1), jnp.concatenate([ones_bd, ones_bd], axis=0))


def _inproj_body(xp_hbm, xs_ref, nw_ref, w_ref, zp_ref, zs_ref, h_scr, x_buf, x_sem, *, tn, n, tm):
    i = pl.program_id(0)
    j = pl.program_id(1)
    ni = pl.num_programs(0)
    nj = pl.num_programs(1)

    def x_copy(blk):
        return pltpu.make_async_copy(xp_hbm.at[pl.ds(pl.multiple_of(blk * tm, tm), tm), :], x_buf, x_sem)

    @pl.when((j == 0) & (i == 0))
    def _():
        x_copy(0).start()

    @pl.when(j == 0)
    def _():
        x_copy(i).wait()
        h_scr[...] = _rms(x_buf[...], nw_ref[...]).astype(bf16)

    @pl.when((j == 1) & (i + 1 < ni))
    def _():
        x_copy(i + 1).start()

    tail = n - (pl.cdiv(n, tn) - 1) * tn

    def project(width):
        w = w_ref[:, 0:width].astype(bf16)
        acc = _dot(h_scr[...], w)
        for c in range(width // LANES):
            zp_ref[c] = acc[:, c * LANES:(c + 1) * LANES]

        @pl.when(i == 0)
        def _():
            hs = _rms(xs_ref[...], nw_ref[...]).astype(bf16)
            col = pl.multiple_of(j * tn, LANES)
            zs_ref[:, pl.ds(col, width)] = _dot(hs, w)

    @pl.when(j < nj - 1)
    def _():
        project(tn)

    @pl.when(j == nj - 1)
    def _():
        project(tail)


def _inproj(xp, xs, nw, w_all, layer, *, tm=2048, tn=512):
    m, d = xp.shape
    n = w_all.shape[2]
    nb = xs.shape[0]
    nj = pl.cdiv(n, tn)
    assert n % LANES == 0 and tn % LANES == 0 and m % tm == 0 and nj >= 2
    return pl.pallas_call(
        functools.partial(_inproj_body, tn=tn, n=n, tm=tm),
        grid=(m // tm, nj),
        in_specs=[
            pl.BlockSpec(memory_space=pl.ANY),
            pl.BlockSpec((nb, d), lambda i, j: (0, 0)),
            pl.BlockSpec((None, 1, d), lambda i, j: (layer, 0, 0)),
            pl.BlockSpec((None, d, tn), lambda i, j: (layer, 0, j)),
        ],
        out_specs=[
            pl.BlockSpec((tn // LANES, tm, LANES), lambda i, j: (j, i, 0)),
            pl.BlockSpec((nb, n), lambda i, j: (0, 0)),
        ],
        out_shape=[jax.ShapeDtypeStruct((n // LANES, m, LANES), f32),
                   jax.ShapeDtypeStruct((nb, n), f32)],
        scratch_shapes=[pltpu.VMEM((tm, d), bf16), pltpu.VMEM((tm, d), f32), pltpu.SemaphoreType.DMA(())],
        compiler_params=pltpu.CompilerParams(
            dimension_semantics=("arbitrary", "arbitrary"), vmem_limit_bytes=VMEM_LIMIT),
        name="inproj",
    )(xp, xs, nw, w_all)


def _outproj_body(*refs, tn, widths):
    n = len(widths)
    xp_ref = refs[0]
    y_refs = refs[1:1 + n]
    xs_ref, ms_ref, w_ref, op_ref, os_ref, w_scr = refs[1 + n:]
    i = pl.program_id(0)
    j = pl.program_id(1)
    col = pl.multiple_of(j * tn, LANES)

    @pl.when(i == 0)
    def _():
        w_scr[:, pl.ds(col, tn)] = w_ref[...].astype(bf16)

    w = w_scr[:, pl.ds(col, tn)]
    acc = None
    row = 0
    for y_ref, width in zip(y_refs, widths):
        part = _dot(y_ref[...], w[row:row + width])
        acc = part if acc is None else acc + part
        row += width
    op_ref[...] = xp_ref[...] + acc

    @pl.when(i == 0)
    def _():
        os_ref[:, pl.ds(col, tn)] = xs_ref[:, pl.ds(col, tn)] + _dot(ms_ref[...].astype(bf16), w)


def _outproj(xp, ys, xs, ms, w_all, layer, *, tm=2048, tn=512):
    m, d = xp.shape
    nb = xs.shape[0]
    dm = w_all.shape[1]
    widths = tuple(y.shape[1] for y in ys)
    assert sum(widths) == dm
    nj = d // tn
    return pl.pallas_call(
        functools.partial(_outproj_body, tn=tn, widths=widths),
        grid=(m // tm, d // tn),
        in_specs=[pl.BlockSpec((tm, tn), lambda i, j: (i, j))]
        + [pl.BlockSpec((tm, width), lambda i, j: (i, 0)) for width in widths]
        + [
            pl.BlockSpec((nb, d), lambda i, j: (0, 0)),
            pl.BlockSpec((nb, dm), lambda i, j: (0, 0)),
            pl.BlockSpec((None, dm, tn), lambda i, j: (layer, 0, jnp.where(i == 0, j, nj - 1))),
        ],
        out_specs=[
            pl.BlockSpec((tm, tn), lambda i, j: (i, j)),
            pl.BlockSpec((nb, d), lambda i, j: (0, 0)),
        ],
        out_shape=[jax.ShapeDtypeStruct((m, d), f32), jax.ShapeDtypeStruct((nb, d), f32)],
        scratch_shapes=[pltpu.VMEM((dm, d), bf16)],
        compiler_params=pltpu.CompilerParams(
            dimension_semantics=("arbitrary", "arbitrary"), vmem_limit_bytes=VMEM_LIMIT),
        name="outproj",
    )(xp, *ys, xs, ms, w_all)


_DECAY_SCALE = 0.6065306597126334

_P_MU, _P_W0, _P_A0, _P_KK, _P_KA, _P_RK, _P_LNW, _P_LNB = 0, 5, 6, 7, 8, 9, 10, 11


def _rwkv_chunk_stages(at, bt, kt, rt, v, wcs, consts):
    m0, gmask, bdmask, eye = consts
    m0w = jnp.concatenate([m0, m0], axis=1)
    eye_pair = jnp.where(lax.broadcasted_iota(jnp.int32, (CHUNK, LANES), 1) % CHUNK
                         == lax.broadcasted_iota(jnp.int32, (CHUNK, LANES), 0), 1.0, 0.0)
    zb64 = jnp.zeros((CHUNK, LANES), bf16)
    n = len(wcs)

    def rows(x, c):
        return x[c * CHUNK:(c + 1) * CHUNK]

    def split_heads(xb, m):
        zero = jnp.zeros_like(xb)
        return jnp.concatenate([jnp.where(m, xb, zero), jnp.where(m, zero, xb)], axis=0)

    g, p, x, mc, nc, rp, ov = {}, {}, {}, {}, {}, {}, {}

    def stage_scores(chs):
        for c in chs:
            lhs = jnp.concatenate([rows(at, c), rows(rt, c)], axis=0).astype(bf16)
            rhs = jnp.concatenate([split_heads(rows(bt, c).astype(bf16), m0),
                                   split_heads(rows(kt, c).astype(bf16), m0)], axis=0)
            g[c] = jnp.where(gmask, _dot_nt(lhs, rhs), 0.0)
            p[c] = g[c][0:CHUNK, 0:LANES]

    av, tinv = {}, {}

    def stage_level(chs, lev):
        for c in chs:
            pb = p[c].astype(bf16)
            if lev == 0:
                vs = split_heads(rows(v, c).astype(bf16), m0)
                av[c] = _dot(g[c][0:CHUNK, LANES:].astype(bf16), vs)
                tinv[c] = eye_pair + p[c]
                p[c] = _dot(pb, split_heads(pb, m0))
            elif lev < 5:
                tt_ = split_heads(tinv[c].astype(bf16), m0)
                out = _dot(pb, jnp.concatenate([split_heads(pb, m0), tt_], axis=1))
                p[c] = out[:, 0:LANES]
                tinv[c] = tinv[c] + out[:, LANES:]
            else:
                tinv[c] = tinv[c] + _dot(pb, split_heads(tinv[c].astype(bf16), m0))

    def stage_apply(chs):
        for c in chs:
            rhs = split_heads(jnp.concatenate([av[c], rows(at, c)], axis=1).astype(bf16), m0w)
            x[c] = _dot(tinv[c].astype(bf16), rhs)

    def stage_fold(chs):
        for c in chs:
            xb = x[c].astype(bf16)
            uv = xb[:, 0:LANES]
            ap = xb[:, LANES:]
            vb = rows(v, c).astype(bf16)
            tnl = jnp.concatenate([rows(bt, c) * wcs[c], rows(kt, c) * wcs[c]], axis=0).astype(bf16)
            tnr = jnp.concatenate(
                [jnp.concatenate([ap, uv], axis=1), jnp.concatenate([zb64, vb], axis=1)], axis=0)
            mn = jnp.where(bdmask, _dot_tn(tnl, tnr), 0.0)
            mc[c] = mn[:, 0:LANES] + jnp.where(eye, jnp.broadcast_to(wcs[c], (LANES, LANES)), 0.0)
            nc[c] = mn[:, LANES:]
            l2 = g[c][CHUNK:].astype(bf16)
            r2 = jnp.concatenate([
                split_heads(jnp.concatenate([ap, uv], axis=1), m0w),
                split_heads(jnp.concatenate([zb64, vb], axis=1), m0w)], axis=0)
            ro = _dot(l2, r2)
            rp[c] = rows(rt, c) + ro[:, 0:LANES]
            ov[c] = ro[:, LANES:]

    def carry(c, st):
        seq = _dot(jnp.concatenate([rp[c], mc[c]], axis=0).astype(bf16), st.astype(bf16))
        return seq[0:CHUNK] + ov[c], seq[CHUNK:] + nc[c]

    chs = list(range(n))
    stages = ([functools.partial(stage_scores, chs)]
              + [functools.partial(stage_level, chs, lev) for lev in range(6)]
              + [functools.partial(stage_apply, chs), functools.partial(stage_fold, chs)])
    return stages, carry


def _rwkv_body(*refs, tt, npr, lock):
    ng = len(POOL_WINDOWS)
    zr_ref, zk_ref, zv_ref, zg_ref, zl_ref, pp_ref, wl_ref = refs[0:7]
    pu_refs = [r.at[0] for r in refs[7:7 + ng]]
    pg_refs = [r.at[0] for r in refs[7 + ng:7 + 2 * ng]]
    pw_ref, psc_ref, y_ref, s_ref, yc_ref, sh_scr, st_scr, pool_scr = refs[7 + 2 * ng:]
    tb = pl.program_id(2)
    nt = pl.num_programs(2)
    nch = tt // CHUNK
    nsh = 4 * npr + 1
    pad = max(POOL_WINDOWS)

    @pl.when(tb == 0)
    def _():
        st_scr[...] = jnp.zeros_like(st_scr)
        for i in range(nsh):
            sh_scr[i, 7:8, :] = jnp.zeros((1, LANES), f32)
        for gi in range(ng):
            pool_scr[gi, 0:pad, :] = jnp.zeros((pad, LANES), f32)

    @pl.when(tb > 0)
    def _():
        for i in range(nsh):
            sh_scr[i, 7:8, :] = sh_scr[i, 7 + tt:8 + tt, :]
        for gi in range(ng):
            pool_scr[gi, 0:pad, :] = pool_scr[gi, tt:tt + pad, :]

    def pooling():
        pos = (tb * tt + lax.broadcasted_iota(jnp.int32, (tt, 1), 0)).astype(f32)
        for gi, w in enumerate(POOL_WINDOWS):
            u = pu_refs[gi][0]
            pool_scr[gi, pad:pad + tt, :] = u
            acc = u
            for k in range(1, w):
                acc = acc + pool_scr[gi, pl.ds(pad - k, tt), :]
            d = acc / jnp.minimum(float(w), pos + 1.0) - u
            dd = _dot(d.astype(bf16), pw_ref[gi].astype(bf16)) * psc_ref[:, gi * LANES:(gi + 1) * LANES]
            yc_ref[0, :, gi * LANES:(gi + 1) * LANES] = (dd * _silu(pg_refs[gi][0])).astype(bf16)
            yield

    def shifted(idx, z, mu):
        sh_scr[idx, 8:8 + tt, :] = z
        zprev = sh_scr[idx, pl.ds(7, tt), :]
        return z + (zprev - z) * mu

    m0t = _pair_consts(tt)
    ones_bd = _ones_blockdiag()
    lo = shifted(4 * npr, zl_ref[0, 0], pp_ref[0, _P_MU + 4:_P_MU + 5])
    xl = jnp.where(m0t, jnp.tanh(lo), lo).astype(bf16)

    m0 = _pair_consts(CHUNK)
    gr = lax.broadcasted_iota(jnp.int32, (2 * CHUNK, 2 * LANES), 0)
    gc = lax.broadcasted_iota(jnp.int32, (2 * CHUNK, 2 * LANES), 1)
    gt = gr % CHUNK
    gs = gc % CHUNK
    gmask = (gt > gs) | ((gr >= CHUNK) & (gt == gs))
    br = lax.broadcasted_iota(jnp.int32, (LANES, 2 * LANES), 0)
    bc = lax.broadcasted_iota(jnp.int32, (LANES, 2 * LANES), 1)
    bdmask = (br < HEAD) == ((bc % LANES) < HEAD)
    er = lax.broadcasted_iota(jnp.int32, (LANES, LANES), 0)
    ec = lax.broadcasted_iota(jnp.int32, (LANES, LANES), 1)
    eye = er == ec
    consts = (m0, gmask, bdmask, eye)
    tr = lax.broadcasted_iota(jnp.int32, (CHUNK, CHUNK), 0)
    tc = lax.broadcasted_iota(jnp.int32, (CHUNK, CHUNK), 1)
    tril = jnp.where(tc <= tr, 1.0, 0.0).astype(bf16)
    tril3 = jnp.concatenate([tril, tril, tril], axis=1)

    def prepare(s, out):
        pp = pp_ref[s]
        r, k, v, g = [shifted(4 * s + i, ref[s, 0], pp[_P_MU + i:_P_MU + i + 1])
                      for i, ref in enumerate((zr_ref, zk_ref, zv_ref, zg_ref))]
        la = _dot(xl, wl_ref[s].astype(bf16))
        kk = k * pp[_P_KK:_P_KK + 1]
        kk_ss = _segsum(kk * kk, ones_bd)
        yield
        ld = -_DECAY_SCALE * jax.nn.sigmoid(pp[_P_W0:_P_W0 + 1] + la[:, 0:LANES])
        a = jax.nn.sigmoid(pp[_P_A0:_P_A0 + 1] + la[:, LANES:])
        kk = kk * jnp.minimum(lax.rsqrt(kk_ss), 1e12)
        k2 = k * (1.0 + (a - 1.0) * pp[_P_KA:_P_KA + 1])
        h3 = _split3(ld)
        cl = jnp.concatenate([
            _dot(tril3, jnp.concatenate([h[c * CHUNK:(c + 1) * CHUNK] for h in h3], axis=0))
            for c in range(nch)], axis=0)
        bonus = _segsum(r * k2 * pp[_P_RK:_P_RK + 1], ones_bd) * v
        yield
        e_in = jnp.exp(cl)
        e_neg = jnp.exp(-cl)
        wcs = [e_in[(c + 1) * CHUNK - 1:(c + 1) * CHUNK] for c in range(nch)]
        out["stages"], out["carry"] = _rwkv_chunk_stages(
            -kk * jnp.exp(cl - ld), kk * a * e_neg, k2 * e_neg, r * e_in, v, wcs, consts)
        out["epi"] = (pp, g, bonus)

    def finish(s, outs, st, epi):
        pp, g, bonus = epi
        st_scr[s] = st
        o = jnp.concatenate(outs, axis=0)
        mean = _segsum(o, ones_bd) * (1.0 / HEAD)
        yield
        dl = o - mean
        var = _segsum(dl * dl, ones_bd) * (1.0 / HEAD)
        yield
        on = dl * lax.rsqrt(var + GN_EPS) * pp[_P_LNW:_P_LNW + 1] + pp[_P_LNB:_P_LNB + 1] + bonus
        y_ref[0, :, s * LANES:(s + 1) * LANES] = (on * _silu(g)).astype(bf16)

    def drain(gen):
        for _ in gen:
            pass

    def lockstep(gens):
        gens = list(gens)
        while gens:
            gens = [g for g in gens if next(g, StopIteration) is not StopIteration]
            yield

    preps = [dict() for _ in range(npr)]
    units = [list(range(i, min(i + lock, npr))) for i in range(0, npr, lock)]
    drain(lockstep(prepare(s, preps[s]) for s in units[0]))
    prev = None
    closing = iter(())
    side = pooling()
    for ui, unit in enumerate(units):
        nxt = (lockstep(prepare(s, preps[s]) for s in units[ui + 1]) if ui + 1 < len(units) else iter(()))
        for stage_row in zip(*[preps[s]["stages"] for s in unit]):
            for stage in stage_row:
                stage()
            if prev is not None:
                for cur in prev:
                    if cur["todo"]:
                        cur["step"]()
            next(closing, None)
            next(nxt, None)
            if ui > 0:
                next(side, None)
        drain(closing)
        drain(nxt)
        if prev is not None:
            while any(cur["todo"] for cur in prev):
                for cur in prev:
                    if cur["todo"]:
                        cur["step"]()
            closing = lockstep(finish(cur["s"], cur["outs"], cur["st"][0], cur["epi"]) for cur in prev)
        prev = []
        for s in unit:
            cur = {"s": s, "outs": [], "st": [st_scr[s]], "epi": preps[s]["epi"], "todo": list(range(nch))}

            def step(cur=cur, carry=preps[s]["carry"]):
                o_c, cur["st"][0] = carry(cur["todo"].pop(0), cur["st"][0])
                cur["outs"].append(o_c)

            cur["step"] = step
            prev.append(cur)
    while any(cur["todo"] for cur in prev):
        for cur in prev:
            if cur["todo"]:
                cur["step"]()
        next(closing, None)
    drain(closing)
    drain(side)
    drain(lockstep(finish(cur["s"], cur["outs"], cur["st"][0], cur["epi"]) for cur in prev))

    @pl.when(tb == nt - 1)
    def _():
        for s in range(npr):
            stt = st_scr[s].T
            s_ref[0, 2 * s] = stt[0:HEAD, 0:HEAD]
            s_ref[0, 2 * s + 1] = stt[HEAD:, HEAD:]


def _rwkv_pool_prompt(z3, pp, wl, pw, psc, layer, *, tt=512, npr=6, lock=2):
    _, b, t, _ = z3.shape
    npair = RWKV_HEADS // 2
    ngrp = npair // npr
    assert ngrp == 1, "the pooling columns are produced once per (batch row, position tile)"
    wide = npr * LANES
    ng = len(POOL_WINDOWS)

    def zspec(part):
        return pl.BlockSpec((npr, 1, tt, LANES), lambda bi, p, tb, part=part: (part * ngrp + p, bi, tb, 0))

    lora_blk = 4 * RWKV_W // LANES
    return pl.pallas_call(
        functools.partial(_rwkv_body, tt=tt, npr=npr, lock=lock),
        grid=(b, ngrp, t // tt),
        in_specs=[
            zspec(0), zspec(1), zspec(2), zspec(3),
            pl.BlockSpec((1, 1, tt, LANES), lambda bi, p, tb: (lora_blk, bi, tb, 0)),
            pl.BlockSpec((None, npr, 12, LANES), lambda bi, p, tb: (layer, p, 0, 0)),
            pl.BlockSpec((None, npr, LANES, 2 * LANES), lambda bi, p, tb: (layer, p, 0, 0)),
        ]
        + [pl.BlockSpec((1, 1, tt, LANES), lambda bi, p, tb, off=POOL_BLK0 + i: (off, bi, tb, 0))
           for i in range(2 * ng)]
        + [
            pl.BlockSpec((None, ng, LANES, LANES), lambda bi, p, tb: (layer, 0, 0, 0)),
            pl.BlockSpec((None, 1, POOL_W), lambda bi, p, tb: (layer, 0, 0)),
        ],
        out_specs=[
            pl.BlockSpec((1, tt, wide), lambda bi, p, tb: (bi, tb, p)),
            pl.BlockSpec((1, 2 * npr, HEAD, HEAD), lambda bi, p, tb: (bi, p, 0, 0)),
            pl.BlockSpec((1, tt, POOL_W), lambda bi, p, tb: (bi, tb, 0)),
        ],
        out_shape=[jax.ShapeDtypeStruct((b, t, RWKV_W), bf16),
                   jax.ShapeDtypeStruct((b, RWKV_HEADS, HEAD, HEAD), f32),
                   jax.ShapeDtypeStruct((b, t, POOL_W), bf16)],
        scratch_shapes=[pltpu.VMEM((4 * npr + 1, tt + 8, LANES), f32), pltpu.VMEM((npr, LANES, LANES), f32),
                        pltpu.VMEM((ng, tt + max(POOL_WINDOWS), LANES), f32)],
        compiler_params=pltpu.CompilerParams(
            dimension_semantics=("arbitrary", "arbitrary", "arbitrary"), vmem_limit_bytes=VMEM_LIMIT),
        name="rwkv_pool_prompt",
    )(*([z3] * 5), pp, wl, *([z3] * (2 * ng)), pw, psc)


def _rope_pair(x, cos, sa, sb):
    return x * cos + pltpu.roll(x, LANES - ROPE_DIMS // 2, 1) * sa + pltpu.roll(x, ROPE_DIMS // 2, 1) * sb


def _attn_body(*refs, t):
    qkvg = [[r.at[0] for r in refs[4 * gi:4 * gi + 4]] for gi in range(3)]
    cos_ref, sin_ref, qnw_ref, knw_ref = refs[12:16]
    y_refs = refs[16:19]
    kv_refs = [refs[19 + 2 * gi:21 + 2 * gi] for gi in range(3)]
    qn_scr, kn_scr, o_scr, lse_scr = refs[25:29]

    rb = 256
    ones_bd = _ones_blockdiag()
    m0q = _pair_consts(QBLK)
    scale = HEAD ** -0.5
    half = ROPE_DIMS // 2
    pj = lax.broadcasted_iota(jnp.int32, (LANES, LANES), 0)
    pi = lax.broadcasted_iota(jnp.int32, (LANES, LANES), 1)
    pin = pi % HEAD
    perm = jnp.where(((pin < half) & (pj == pi + half)) | ((pin >= half) & (pin < ROPE_DIMS) & (pj == pi - half)),
                     1.0, 0.0).astype(bf16)
    zpad = jnp.zeros((LANES, LANES), bf16)
    sum_swap = jnp.concatenate([jnp.concatenate([ones_bd, zpad], axis=1),
                                jnp.concatenate([zpad, perm], axis=1)], axis=0)

    for gi, (window, dil) in enumerate(ATT_GROUPS):
        q_ref, k_ref, v_ref, g_ref = qkvg[gi]
        pk_ref, pv_ref = kv_refs[gi]
        keep = min(window, t)

        def norm_rows(i, carry, q_ref=q_ref, k_ref=k_ref):
            jobs = []
            for u in range(2):
                rows = pl.ds(pl.multiple_of((2 * i + u) * rb, rb), rb)
                for src, nw, dst in ((q_ref, qnw_ref, qn_scr), (k_ref, knw_ref, kn_scr)):
                    x = src[0, rows, :]
                    y = x * nw[...]
                    hi, lo = _split2(jnp.concatenate([x * x, y], axis=1))
                    jobs.append((y, _dot(hi, sum_swap) + _dot(lo, sum_swap), dst, rows))
            for y, res, dst, rows in jobs:
                rs = lax.rsqrt(res[:, 0:LANES] * (1.0 / HEAD) + NORM_EPS)
                dst[rows, :] = (y * cos_ref[rows, :] + res[:, LANES:] * sin_ref[rows, :]) * rs
            return carry

        lax.fori_loop(0, t // (2 * rb), norm_rows, 0)
        pk_ref[0] = kn_scr[t - keep:t, :]
        pv_ref[0] = v_ref[0, t - keep:t, :]

        n_sub = t // dil
        n_blk = n_sub // QBLK

        def tiles(starts, nk, first, v_ref=v_ref, dil=dil, gi=gi):
            qi = lax.broadcasted_iota(jnp.int32, (QBLK, nk), 0)
            kj = lax.broadcasted_iota(jnp.int32, (QBLK, nk), 1)
            mask = (kj <= qi) if first else ((kj >= qi) & (kj <= qi + QBLK))
            scores, vts = [], []
            for q0, k0 in starts:
                qt = qn_scr[pl.ds(q0, QBLK, stride=dil), :] * scale
                kt = kn_scr[pl.ds(k0, nk, stride=dil), :].astype(bf16)
                vts.append(v_ref[0, pl.ds(k0, nk, stride=dil), :].astype(bf16))
                for hh in range(2):
                    qh = jnp.where(m0q, qt, 0.0) if hh == 0 else jnp.where(m0q, 0.0, qt)
                    scores.append(_dot_nt(qh.astype(bf16), kt))
            probs, sums, lses = [], [], []
            for s in scores:
                s = jnp.where(mask, s, -jnp.inf)
                m = jnp.max(s, axis=-1, keepdims=True)
                p = jnp.exp(s - m)
                l = jnp.sum(p, axis=-1, keepdims=True)
                probs.append(p.astype(bf16))
                sums.append(l)
                lses.append(m + jnp.log(l))
            for ti, (q0, _) in enumerate(starts):
                oh = [_dot(probs[2 * ti + hh], vts[ti]) / sums[2 * ti + hh] for hh in range(2)]
                o_scr[gi, pl.ds(q0, QBLK, stride=dil), :] = jnp.where(m0q, oh[0], oh[1])
                lse_scr[gi, pl.ds(q0, QBLK, stride=dil), :] = jnp.where(
                    m0q, jnp.broadcast_to(lses[2 * ti], (QBLK, LANES)),
                    jnp.broadcast_to(lses[2 * ti + 1], (QBLK, LANES)))

        per_first = min(8, dil)

        def first_tiles(i, carry, tiles=tiles, per=per_first):
            tiles([(i * per + u, i * per + u) for u in range(per)], QBLK, True)
            return carry

        lax.fori_loop(0, dil // per_first, first_tiles, 0)

        if n_blk > 1:
            n_later = dil * (n_blk - 1)
            per_later = max(p for p in range(1, 7) if n_later % p == 0)

            def later_tiles(i, carry, tiles=tiles, dil=dil, per=per_later):
                starts = []
                for u in range(per):
                    idx = i * per + u
                    r = idx % dil
                    blk = idx // dil + 1
                    starts.append((r + blk * QBLK * dil, r + (blk - 1) * QBLK * dil))
                tiles(starts, 2 * QBLK, False)
                return carry

            lax.fori_loop(0, dil * (n_blk - 1) // per_later, later_tiles, 0)

    def combine(i, carry):
        rows = pl.ds(pl.multiple_of(i * rb, rb), rb)
        ls = [lse_scr[gi, rows, :] for gi in range(3)]
        mx = jnp.maximum(jnp.maximum(ls[0], ls[1]), ls[2])
        es = [jnp.exp(l - mx) for l in ls]
        inv = 1.0 / (es[0] + es[1] + es[2])
        for gi in range(3):
            gate = qkvg[gi][3][0, rows, :]
            y_refs[gi][0, rows, :] = (o_scr[gi, rows, :] * (es[gi] * inv) * _silu(gate)).astype(bf16)
        return carry

    lax.fori_loop(0, t // rb, combine, 0)


def _attn_prompt(z3, cos, sin, qnw, knw, layer):
    _, b, t, _ = z3.shape
    in_specs = []
    for gi in range(3):
        for part in range(4):
            off = ATT_BLK0 + part * 6 + 2 * gi
            in_specs.append(pl.BlockSpec((1, 1, t, LANES), lambda bi, jp, off=off: (off + jp, bi, 0, 0)))
    in_specs += [pl.BlockSpec((t, LANES), lambda bi, jp: (0, 0))] * 2
    in_specs += [pl.BlockSpec((None, 1, LANES), lambda bi, jp: (layer, 0, 0))] * 2
    out_specs = [pl.BlockSpec((1, t, LANES), lambda bi, jp: (bi, 0, jp))] * 3
    out_shape = [jax.ShapeDtypeStruct((b, t, 4 * HEAD), bf16)] * 3
    for window, _ in ATT_GROUPS:
        keep = min(window, t)
        out_specs += [pl.BlockSpec((1, keep, LANES), lambda bi, jp: (bi, 0, jp))] * 2
        out_shape += [jax.ShapeDtypeStruct((b, keep, 4 * HEAD), f32)] * 2
    outs = pl.pallas_call(
        functools.partial(_attn_body, t=t),
        grid=(b, 2),
        in_specs=in_specs,
        out_specs=out_specs,
        out_shape=out_shape,
        scratch_shapes=[pltpu.VMEM((t, LANES), f32), pltpu.VMEM((t, LANES), f32),
                        pltpu.VMEM((3, t, LANES), f32), pltpu.VMEM((3, t, LANES), f32)],
        compiler_params=pltpu.CompilerParams(
            dimension_semantics=("arbitrary", "arbitrary"), vmem_limit_bytes=VMEM_LIMIT),
        name="attn_prompt",
    )(*([z3] * 12), cos, sin, qnw, knw)
    return outs


def _bcast8(x):
    return jnp.broadcast_to(x, (8, x.shape[-1]))


def _sample_body(*refs, past_len, nbs):
    gens = [_sample_one(bi, *refs, past_len=past_len) for bi in range(nbs)]
    while gens:
        gens = [g for g in gens if next(g, StopIteration) is not StopIteration]


def _sample_one(bi, z_ref, sh_ref, wkv_ref, pool_ref, ck0, cv0, ck1, cv1, ck2, cv2,
                mu_ref, rp_ref, wup_ref, aup_ref, qnw_ref, knw_ref, cos_ref, sa_ref, sb_ref,
                pw_ref, psc_ref,
                mix_ref, swkv_ref, spool_ref, sk0, sv0, sk1, sv1, sk2, sv2,
                qk_scr, *, past_len):
    z = z_ref[bi]
    za = z[:, 0:RWKV_COLS]
    zb = z[:, RWKV_COLS:RWKV_COLS + ATT_COLS]
    zc = z[:, RWKV_COLS + ATT_COLS:RWKV_COLS + ATT_COLS + 2 * POOL_W]
    er = lax.broadcasted_iota(jnp.int32, (HEAD, HEAD), 0)
    ec = lax.broadcasted_iota(jnp.int32, (HEAD, HEAD), 1)
    eye = er == ec

    zs = za + (sh_ref[bi] - za) * mu_ref[...]
    w_ = RWKV_W
    r, k, v, g = (zs[:, i * w_:(i + 1) * w_] for i in range(4))
    w_dn = zs[:, 4 * w_:4 * w_ + LORA]
    a_dn = zs[:, 4 * w_ + LORA:]
    rp = rp_ref[...]
    w0, a0, k_k, k_a, r_k, ln_w, ln_b = (rp[i:i + 1] for i in range(7))
    lw = _dot(_bcast8(jnp.tanh(w_dn)).astype(bf16), wup_ref[...].astype(bf16))[0:1]
    la = _dot(_bcast8(a_dn).astype(bf16), aup_ref[...].astype(bf16))[0:1]
    yield
    decay = jnp.exp(-_DECAY_SCALE * jax.nn.sigmoid(w0 + lw))
    a = jax.nn.sigmoid(a0 + la)
    kk = k * k_k
    k2 = k * (1.0 + (a - 1.0) * k_a)
    heads = range(RWKV_HEADS)
    hsl = [slice(h * HEAD, (h + 1) * HEAD) for h in heads]

    def lane_sum(x):
        return jnp.sum(x, axis=-1, keepdims=True)

    kk_ss = [lane_sum(kk[:, hs] * kk[:, hs]) for hs in hsl]
    bonus = [lane_sum(r[:, hs] * k2[:, hs] * r_k[:, hs]) for hs in hsl]
    v_col = [lane_sum(jnp.where(eye, jnp.broadcast_to(v[:, hs], (HEAD, HEAD)), 0.0)) for hs in hsl]
    yield
    kkn = [kk[:, hs] / jnp.maximum(jnp.sqrt(ss), 1e-12) for hs, ss in zip(hsl, kk_ss)]
    sa_col = [lane_sum(wkv_ref[bi, h] * (-kkn[h])) for h in heads]
    yield
    sn = [wkv_ref[bi, h] * decay[:, hsl[h]] + sa_col[h] * (kkn[h] * a[:, hsl[h]]) + v_col[h] * k2[:, hsl[h]]
          for h in heads]
    for h in heads:
        swkv_ref[bi, h] = sn[h]
    o_col = [lane_sum(sn[h] * r[:, hsl[h]]) for h in heads]
    yield
    o = [jnp.sum(jnp.where(eye, jnp.broadcast_to(oc, (HEAD, HEAD)), 0.0), axis=0, keepdims=True) for oc in o_col]
    mean = [jnp.mean(x, axis=-1, keepdims=True) for x in o]
    yield
    dl = [x - mu_ for x, mu_ in zip(o, mean)]
    var = [jnp.mean(x * x, axis=-1, keepdims=True) for x in dl]
    yield
    for h in heads:
        hs = hsl[h]
        on = dl[h] * lax.rsqrt(var[h] + GN_EPS) * ln_w[:, hs] + ln_b[:, hs] + bonus[h] * v[:, hs]
        mix_ref[bi, :, hs] = on * _silu(g[:, hs])

    aw = ATT_W
    q, kx, vx, gx = (zb[:, i * aw:(i + 1) * aw] for i in range(4))
    m0 = _pair_consts(8)
    for hp in range(ATT_W // LANES):
        ls = slice(hp * LANES, (hp + 1) * LANES)
        for idx, (src, nw) in enumerate(((q, qnw_ref), (kx, knw_ref))):
            x = _bcast8(src[:, ls])
            sq = x * x
            s0 = jnp.sum(jnp.where(m0, sq, 0.0), axis=-1, keepdims=True)
            s1 = jnp.sum(jnp.where(m0, 0.0, sq), axis=-1, keepdims=True)
            ms = jnp.where(m0, s0, s1) * (1.0 / HEAD)
            xn = x * lax.rsqrt(ms + NORM_EPS) * nw[...]
            qk_scr[bi, idx, :, ls] = _rope_pair(xn, cos_ref[...], sa_ref[...], sb_ref[...])
    yield
    qn = qk_scr[bi, 0, 0:1, :]
    kn = qk_scr[bi, 1, 0:1, :]
    scale = HEAD ** -0.5

    def heads4(row, gi):
        return jnp.concatenate([row[:, (gi * 4 + j) * HEAD:(gi * 4 + j + 1) * HEAD] for j in range(4)], axis=0)

    def to_col(row):
        return jnp.sum(jnp.where(eye, jnp.broadcast_to(row, (HEAD, HEAD)), 0.0), axis=-1, keepdims=True)

    def to_row(col):
        return jnp.sum(jnp.where(eye, jnp.broadcast_to(col, (HEAD, HEAD)), 0.0), axis=0, keepdims=True)

    caches = ((ck0, cv0, sk0, sv0), (ck1, cv1, sk1, sv1), (ck2, cv2, sk2, sv2))
    windows = []
    for gi, (ck, cv, sk, sv) in enumerate(caches):
        sk[bi, 0] = heads4(kn, gi)
        sv[bi, 0] = heads4(vx, gi)
        row_id = lax.broadcasted_iota(jnp.int32, (1, ck.shape[-1]), 1)
        windows.append((row_id % ATT_GROUPS[gi][1]) == 0)
    combos = [(gi, j) for gi in range(3) for j in range(4)]
    asl = [slice((gi * 4 + j) * HEAD, (gi * 4 + j + 1) * HEAD) for gi, j in combos]
    qh = [qn[:, hs] * scale for hs in asl]
    q_col = [to_col(x) for x in qh]
    s_new = [jnp.sum(kn[:, hs] * x, axis=-1, keepdims=True) for hs, x in zip(asl, qh)]
    yield
    s_all = [jnp.where(windows[gi], jnp.sum(caches[gi][0][bi, j] * qc, axis=0, keepdims=True), -jnp.inf)
             for (gi, j), qc in zip(combos, q_col)]
    m_all = [jnp.maximum(jnp.max(s, axis=-1, keepdims=True), sn_) for s, sn_ in zip(s_all, s_new)]
    yield
    p_all = [jnp.exp(s - m) for s, m in zip(s_all, m_all)]
    p_new = [jnp.exp(sn_ - m) for sn_, m in zip(s_new, m_all)]
    l_all = [jnp.sum(p, axis=-1, keepdims=True) + pn for p, pn in zip(p_all, p_new)]
    o_colv = [jnp.sum(caches[gi][1][bi, j] * p, axis=-1, keepdims=True) for (gi, j), p in zip(combos, p_all)]
    yield
    o_rows = [(to_row(oc) + pn * vx[:, hs]) / l for oc, pn, hs, l in zip(o_colv, p_new, asl, l_all)]
    lse = [m + jnp.log(l) for m, l in zip(m_all, l_all)]
    for j in range(4):
        mx = jnp.maximum(jnp.maximum(lse[j], lse[4 + j]), lse[8 + j])
        es = [jnp.exp(lse[gi * 4 + j] - mx) for gi in range(3)]
        inv = 1.0 / (es[0] + es[1] + es[2])
        for gi in range(3):
            ci = gi * 4 + j
            mix_ref[bi, :, RWKV_W + ci * HEAD:RWKV_W + (ci + 1) * HEAD] = (
                o_rows[ci] * (es[gi] * inv) * _silu(gx[:, asl[ci]]))

    u = zc[:, 0:POOL_W]
    gate = zc[:, POOL_W:]
    prev = pool_ref[bi]
    for gi, w in enumerate(POOL_WINDOWS):
        cs = slice(gi * LANES, (gi + 1) * LANES)
        ug = u[:, cs]
        sw = jnp.sum(prev[POOL_BUF - (w - 1):POOL_BUF, cs], axis=0, keepdims=True) + ug
        cnt = min(float(w), float(past_len) + 1.0)
        d = sw / cnt - ug
        dd = _dot(_bcast8(d).astype(bf16), pw_ref[gi].astype(bf16))[0:1] * psc_ref[:, cs]
        mix_ref[bi, :, RWKV_W + ATT_W + gi * LANES:RWKV_W + ATT_W + (gi + 1) * LANES] = dd * _silu(gate[:, cs])
    spool_ref[bi, 0:POOL_BUF - 1, :] = prev[1:POOL_BUF]
    spool_ref[bi, POOL_BUF - 1:POOL_BUF, :] = u


def _sample_step(zs, sh_all, wkv_all, pool_all, caches_all, layer, mu, rp, wup, aup, qnw, knw, cos, sa, sb,
                 pw, psc, *, past_len, nbs=2):
    nb = zs.shape[0]
    d_in = zs.shape[-1]
    d_mix = RWKV_W + ATT_W + POOL_W

    def full(shape):
        nd = len(shape)
        return pl.BlockSpec(shape, lambda b, nd=nd: (0,) * nd)

    assert nb % nbs == 0
    in_specs = [
        pl.BlockSpec((nbs, 1, d_in), lambda b: (b, 0, 0)),
        pl.BlockSpec((None, nbs, 1, RWKV_COLS), lambda b: (layer, b, 0, 0)),
        pl.BlockSpec((None, nbs, RWKV_HEADS, HEAD, HEAD), lambda b: (layer, b, 0, 0, 0)),
        pl.BlockSpec((None, nbs, POOL_BUF, POOL_W), lambda b: (layer, b, 0, 0)),
    ]
    in_specs += [pl.BlockSpec((None, nbs, 4, HEAD, c.shape[-1]), lambda b: (layer, b, 0, 0, 0))
                 for c in caches_all]

    def of_layer(x):
        nd = x.ndim - 1
        return pl.BlockSpec((None,) + x.shape[1:], lambda b, nd=nd: (layer,) + (0,) * nd)

    in_specs += [of_layer(mu), of_layer(rp), of_layer(wup), of_layer(aup), of_layer(qnw), of_layer(knw),
                 full(cos.shape), full(sa.shape), full(sb.shape), of_layer(pw), of_layer(psc)]
    out_specs = [
        pl.BlockSpec((nbs, 1, d_mix), lambda b: (b, 0, 0)),
        pl.BlockSpec((nbs, RWKV_HEADS, HEAD, HEAD), lambda b: (b, 0, 0, 0)),
        pl.BlockSpec((nbs, POOL_BUF, POOL_W), lambda b: (b, 0, 0)),
    ] + [pl.BlockSpec((nbs, 1, 4, HEAD), lambda b: (b, 0, 0, 0))] * 6
    out_shape = [
        jax.ShapeDtypeStruct((nb, 1, d_mix), f32),
        jax.ShapeDtypeStruct((nb, RWKV_HEADS, HEAD, HEAD), f32),
        jax.ShapeDtypeStruct((nb, POOL_BUF, POOL_W), f32),
    ] + [jax.ShapeDtypeStruct((nb, 1, 4, HEAD), f32)] * 6
    return pl.pallas_call(
        functools.partial(_sample_body, past_len=past_len, nbs=nbs),
        grid=(nb // nbs,),
        in_specs=in_specs,
        out_specs=out_specs,
        out_shape=out_shape,
        scratch_shapes=[pltpu.VMEM((nbs, 2, 8, ATT_W), f32)],
        compiler_params=pltpu.CompilerParams(
            dimension_semantics=("arbitrary",), vmem_limit_bytes=VMEM_LIMIT),
        name="sample_step",
    )(zs, sh_all, wkv_all, pool_all, *caches_all, mu, rp, wup, aup, qnw, knw, cos, sa, sb, pw, psc)


def _rope_tables(pos):
    half = ROPE_DIMS // 2
    inv = jnp.power(jnp.float32(ROPE_THETA), -jnp.arange(half, dtype=f32) * 2.0 / ROPE_DIMS)
    ang = pos[:, None] * inv[None, :]
    cos, sin = jnp.cos(ang), jnp.sin(ang)
    n = pos.shape[0]
    pad = jnp.zeros((n, HEAD - ROPE_DIMS), f32)
    zero = jnp.zeros((n, half), f32)
    c_head = jnp.concatenate([cos, cos, pad + 1.0], axis=1)
    a_head = jnp.concatenate([-sin, zero, pad], axis=1)
    b_head = jnp.concatenate([zero, sin, pad], axis=1)
    return tuple(jnp.concatenate([x, x], axis=1) for x in (c_head, a_head, b_head))


def kernel(x_prompt, x_sample, state_wkv, state_shift, state_pool,
           cache_k_w128, cache_v_w128, cache_k_w512, cache_v_w512, cache_k_w2048, cache_v_w2048,
           norm_w, w_in, w_out, rwkv_mu, rwkv_w0, rwkv_w_up, rwkv_a0, rwkv_a_up,
           rwkv_k_k, rwkv_k_a, rwkv_r_k, rwkv_ln_w, rwkv_ln_b, q_norm_w, k_norm_w, pool_w, pool_scale):
    b, t, d = x_prompt.shape
    nb, ts, _ = x_sample.shape
    depth = w_in.shape[0]
    d_in = w_in.shape[2]
    assert ts == 1 and t % 512 == 0
    past_len = PAST_LEN
    caches_in = ((cache_k_w128, cache_v_w128), (cache_k_w512, cache_v_w512), (cache_k_w2048, cache_v_w2048))
    for (window, _), (ck, _) in zip(ATT_GROUPS, caches_in):
        assert ck.shape[2] == window, "window buffers are expected to be full"

    cos_p, sa_p, sb_p = _rope_tables(jnp.arange(t, dtype=f32))
    cos_s, sa_s, sb_s = _rope_tables(past_len + jnp.arange(1, dtype=f32))

    hp = x_prompt.reshape(b * t, d)
    hs = x_sample.reshape(nb, d)
    npair = RWKV_HEADS // 2
    p_out = [[] for _ in range(9)]
    s_out = [[] for _ in range(9)]
    sh_all = state_shift.reshape(depth, nb, 1, RWKV_COLS)
    caches_all = [jnp.transpose(c, (0, 1, 3, 4, 2)) for pair in caches_in for c in pair]

    nw_all = norm_w.reshape(depth, 1, d)
    mu_all = rwkv_mu.reshape(depth, 1, RWKV_COLS)
    mu4 = rwkv_mu[:, :4 * RWKV_W].reshape(depth, 4, npair, LANES).transpose(0, 2, 1, 3)
    mul = jnp.broadcast_to(rwkv_mu[:, 4 * RWKV_W:].reshape(depth, 1, 1, LANES), (depth, npair, 1, LANES))
    vecs_all = jnp.stack([rwkv_w0, rwkv_a0, rwkv_k_k, rwkv_k_a, rwkv_r_k.reshape(depth, -1),
                          rwkv_ln_w, rwkv_ln_b], axis=1)
    pp_all = jnp.concatenate(
        [mu4, mul, vecs_all.reshape(depth, 7, npair, LANES).transpose(0, 2, 1, 3)], axis=2)
    wup = rwkv_w_up.reshape(depth, LORA, npair, LANES).transpose(0, 2, 1, 3)
    aup = rwkv_a_up.reshape(depth, LORA, npair, LANES).transpose(0, 2, 1, 3)
    zl = jnp.zeros_like(wup)
    wl_all = jnp.concatenate(
        [jnp.concatenate([wup, zl], axis=3), jnp.concatenate([zl, aup], axis=3)], axis=2)
    qnw_all = jnp.tile(q_norm_w, (1, 2)).reshape(depth, 1, LANES)
    knw_all = jnp.tile(k_norm_w, (1, 2)).reshape(depth, 1, LANES)
    psc_all = pool_scale.reshape(depth, 1, POOL_W)
    sin_p = sa_p + sb_p

    for l in range(depth):
        zp, zs = _inproj(hp, hs, nw_all, w_in, l)
        z3 = zp.reshape(d_in // LANES, b, t, LANES)

        ya, p_wkv, yc = _rwkv_pool_prompt(z3, pp_all, wl_all, pool_w, psc_all, l)
        att = _attn_prompt(z3, cos_p, sin_p, qnw_all, knw_all, l)

        sample = _sample_step(
            zs.reshape(nb, 1, -1), sh_all, state_wkv, state_pool, caches_all, l,
            mu_all, vecs_all, rwkv_w_up, rwkv_a_up, qnw_all, knw_all, cos_s, sa_s, sb_s,
            pool_w, psc_all, past_len=past_len)
        mix_s = sample[0].reshape(nb, -1)

        ys = [ya.reshape(b * t, RWKV_W)] + [y.reshape(b * t, 4 * HEAD) for y in att[:3]] + [yc.reshape(b * t, POOL_W)]
        hp, hs = _outproj(hp, ys, hs, mix_s, w_out, l)

        p_out[0].append(p_wkv)
        p_out[1].append(z3[:ATT_BLK0, :, -1, :].transpose(1, 0, 2).reshape(b, RWKV_COLS))
        p_out[2].append(z3[POOL_BLK0:POOL_BLK0 + POOL_W // LANES, :, t - POOL_BUF:, :]
                        .transpose(1, 2, 0, 3).reshape(b, POOL_BUF, POOL_W))
        for gi in range(3):
            keep = att[3 + 2 * gi].shape[1]
            p_out[3 + 2 * gi].append(att[3 + 2 * gi].reshape(b, keep, 4, HEAD))
            p_out[4 + 2 * gi].append(att[4 + 2 * gi].reshape(b, keep, 4, HEAD))
        s_out[0].append(sample[1])
        s_out[1].append(zs[:, :RWKV_COLS])
        s_out[2].append(sample[2])
        for i in range(6):
            s_out[3 + i].append(sample[3 + i])

    return (hp.reshape(b, t, d), hs.reshape(nb, 1, d),
            *[jnp.stack(x) for x in p_out], *[jnp.stack(x) for x in s_out])
```

```python
import functools

import jax
import jax.numpy as jnp
from jax import lax
from jax.experimental import pallas as pl
from jax.experimental.pallas import tpu as pltpu

f32 = jnp.float32
bf16 = jnp.bfloat16

HEAD = 64
LANES = 128
NORM_EPS = 1e-6
GN_EPS = 64e-5
RWKV_HEADS = 12
RWKV_W = RWKV_HEADS * HEAD
LORA = 64
RWKV_COLS = 4 * RWKV_W + 2 * LORA
ATT_GROUPS = ((128, 1), (512, 4), (2048, 16))
ATT_W = 12 * HEAD
ATT_COLS = 4 * ATT_W
ROPE_THETA = 500000.0
ROPE_DIMS = 16
QBLK = 128
POOL_WINDOWS = (2, 4, 8, 16)
POOL_W = 512
POOL_BUF = 15
PAST_LEN = 16384
CHUNK = 64
VMEM_LIMIT = 56 * 1024 * 1024

ATT_BLK0 = RWKV_COLS // LANES
POOL_BLK0 = (RWKV_COLS + ATT_COLS) // LANES


def _dot(a, b):
    return jnp.dot(a, b, preferred_element_type=f32)


def _dot_nt(a, b):
    return lax.dot_general(a, b, (((1,), (1,)), ((), ())), preferred_element_type=f32)


def _dot_tn(a, b):
    return lax.dot_general(a, b, (((0,), (0,)), ((), ())), preferred_element_type=f32)


def _split2(x):
    hi = x.astype(bf16)
    lo = (x - hi.astype(f32)).astype(bf16)
    return hi, lo


def _split3(x):
    hi = x.astype(bf16)
    r1 = x - hi.astype(f32)
    mid = r1.astype(bf16)
    lo = (r1 - mid.astype(f32)).astype(bf16)
    return hi, mid, lo


def _silu(x):
    return x * jax.nn.sigmoid(x)


def _rms(x, w):
    ms = jnp.mean(x * x, axis=-1, keepdims=True)
    return x * lax.rsqrt(ms + NORM_EPS) * w


def _pair_consts(n):
    lane = lax.broadcasted_iota(jnp.int32, (n, LANES), 1)
    return lane < HEAD


def _ones_blockdiag():
    r = lax.broadcasted_iota(jnp.int32, (LANES, LANES), 0)
    c = lax.broadcasted_iota(jnp.int32, (LANES, LANES), 1)
    return jnp.where((r < HEAD) == (c < HEAD), 1.0, 0.0).astype(bf16)


def _segsum(x, ones_bd):
    hi, lo = _split2(x)
    return _dot(jnp.concatenate([hi, lo], axis=1), jnp.concatenate([ones_bd, ones_bd], axis=0))


def _inproj_body(xp_hbm, xs_ref, nw_ref, w_ref, zp_ref, zs_ref, h_scr, x_buf, x_sem, *, tn, n, tm):
    i = pl.program_id(0)
    j = pl.program_id(1)
    ni = pl.num_programs(0)
    nj = pl.num_programs(1)

    def x_copy(blk):
        return pltpu.make_async_copy(xp_hbm.at[pl.ds(pl.multiple_of(blk * tm, tm), tm), :], x_buf, x_sem)

    @pl.when((j == 0) & (i == 0))
    def _():
        x_copy(0).start()

    @pl.when(j == 0)
    def _():
        x_copy(i).wait()
        h_scr[...] = _rms(x_buf[...], nw_ref[...]).astype(bf16)

    @pl.when((j == 1) & (i + 1 < ni))
    def _():
        x_copy(i + 1).start()

    tail = n - (pl.cdiv(n, tn) - 1) * tn

    def project(width):
        w = w_ref[:, 0:width].astype(bf16)
        acc = _dot(h_scr[...], w)
        for c in range(width // LANES):
            zp_ref[c] = acc[:, c * LANES:(c + 1) * LANES]

        @pl.when(i == 0)
        def _():
            hs = _rms(xs_ref[...], nw_ref[...]).astype(bf16)
            col = pl.multiple_of(j * tn, LANES)
            zs_ref[:, pl.ds(col, width)] = _dot(hs, w)

    @pl.when(j < nj - 1)
    def _():
        project(tn)

    @pl.when(j == nj - 1)
    def _():
        project(tail)


def _inproj(xp, xs, nw, w_all, layer, *, tm=2048, tn=512):
    m, d = xp.shape
    n = w_all.shape[2]
    nb = xs.shape[0]
    nj = pl.cdiv(n, tn)
    assert n % LANES == 0 and tn % LANES == 0 and m % tm == 0 and nj >= 2
    return pl.pallas_call(
        functools.partial(_inproj_body, tn=tn, n=n, tm=tm),
        grid=(m // tm, nj),
        in_specs=[
            pl.BlockSpec(memory_space=pl.ANY),
            pl.BlockSpec((nb, d), lambda i, j: (0, 0)),
            pl.BlockSpec((None, 1, d), lambda i, j: (layer, 0, 0)),
            pl.BlockSpec((None, d, tn), lambda i, j: (layer, 0, j)),
        ],
        out_specs=[
            pl.BlockSpec((tn // LANES, tm, LANES), lambda i, j: (j, i, 0)),
            pl.BlockSpec((nb, n), lambda i, j: (0, 0)),
        ],
        out_shape=[jax.ShapeDtypeStruct((n // LANES, m, LANES), f32),
                   jax.ShapeDtypeStruct((nb, n), f32)],
        scratch_shapes=[pltpu.VMEM((tm, d), bf16), pltpu.VMEM((tm, d), f32), pltpu.SemaphoreType.DMA(())],
        compiler_params=pltpu.CompilerParams(
            dimension_semantics=("arbitrary", "arbitrary"), vmem_limit_bytes=VMEM_LIMIT),
        name="inproj",
    )(xp, xs, nw, w_all)


def _outproj_body(*refs, tn, widths):
    n = len(widths)
    xp_ref = refs[0]
    y_refs = refs[1:1 + n]
    xs_ref, ms_ref, w_ref, op_ref, os_ref, w_scr = refs[1 + n:]
    i = pl.program_id(0)
    j = pl.program_id(1)
    col = pl.multiple_of(j * tn, LANES)

    @pl.when(i == 0)
    def _():
        w_scr[:, pl.ds(col, tn)] = w_ref[...].astype(bf16)

    w = w_scr[:, pl.ds(col, tn)]
    acc = None
    row = 0
    for y_ref, width in zip(y_refs, widths):
        part = _dot(y_ref[...], w[row:row + width])
        acc = part if acc is None else acc + part
        row += width
    op_ref[...] = xp_ref[...] + acc

    @pl.when(i == 0)
    def _():
        os_ref[:, pl.ds(col, tn)] = xs_ref[:, pl.ds(col, tn)] + _dot(ms_ref[...].astype(bf16), w)


def _outproj(xp, ys, xs, ms, w_all, layer, *, tm=2048, tn=512):
    m, d = xp.shape
    nb = xs.shape[0]
    dm = w_all.shape[1]
    widths = tuple(y.shape[1] for y in ys)
    assert sum(widths) == dm
    nj = d // tn
    return pl.pallas_call(
        functools.partial(_outproj_body, tn=tn, widths=widths),
        grid=(m // tm, d // tn),
        in_specs=[pl.BlockSpec((tm, tn), lambda i, j: (i, j))]
        + [pl.BlockSpec((tm, width), lambda i, j: (i, 0)) for width in widths]
        + [
            pl.BlockSpec((nb, d), lambda i, j: (0, 0)),
            pl.BlockSpec((nb, dm), lambda i, j: (0, 0)),
            pl.BlockSpec((None, dm, tn), lambda i, j: (layer, 0, jnp.where(i == 0, j, nj - 1))),
        ],
        out_specs=[
            pl.BlockSpec((tm, tn), lambda i, j: (i, j)),
            pl.BlockSpec((nb, d), lambda i, j: (0, 0)),
        ],
        out_shape=[jax.ShapeDtypeStruct((m, d), f32), jax.ShapeDtypeStruct((nb, d), f32)],
        scratch_shapes=[pltpu.VMEM((dm, d), bf16)],
        compiler_params=pltpu.CompilerParams(
            dimension_semantics=("arbitrary", "arbitrary"), vmem_limit_bytes=VMEM_LIMIT),
        name="outproj",
    )(xp, *ys, xs, ms, w_all)


_DECAY_SCALE = 0.6065306597126334

_P_MU, _P_W0, _P_A0, _P_KK, _P_KA, _P_RK, _P_LNW, _P_LNB = 0, 5, 6, 7, 8, 9, 10, 11


def _rwkv_chunk_stages(at, bt, kt, rt, v, wcs, consts):
    m0, gmask, bdmask, eye = consts
    m0w = jnp.concatenate([m0, m0], axis=1)
    eye_pair = jnp.where(lax.broadcasted_iota(jnp.int32, (CHUNK, LANES), 1) % CHUNK
                         == lax.broadcasted_iota(jnp.int32, (CHUNK, LANES), 0), 1.0, 0.0)
    zb64 = jnp.zeros((CHUNK, LANES), bf16)
    n = len(wcs)

    def rows(x, c):
        return x[c * CHUNK:(c + 1) * CHUNK]

    def split_heads(xb, m):
        zero = jnp.zeros_like(xb)
        return jnp.concatenate([jnp.where(m, xb, zero), jnp.where(m, zero, xb)], axis=0)

    g, p, x, mc, nc, rp, ov = {}, {}, {}, {}, {}, {}, {}

    def stage_scores(chs):
        for c in chs:
            lhs = jnp.concatenate([rows(at, c), rows(rt, c)], axis=0).astype(bf16)
            rhs = jnp.concatenate([split_heads(rows(bt, c).astype(bf16), m0),
                                   split_heads(rows(kt, c).astype(bf16), m0)], axis=0)
            g[c] = jnp.where(gmask, _dot_nt(lhs, rhs), 0.0)
            p[c] = g[c][0:CHUNK, 0:LANES]

    av, tinv = {}, {}

    def stage_level(chs, lev):
        for c in chs:
            pb = p[c].astype(bf16)
            if lev == 0:
                vs = split_heads(rows(v, c).astype(bf16), m0)
                av[c] = _dot(g[c][0:CHUNK, LANES:].astype(bf16), vs)
                tinv[c] = eye_pair + p[c]
                p[c] = _dot(pb, split_heads(pb, m0))
            elif lev < 5:
                tt_ = split_heads(tinv[c].astype(bf16), m0)
                out = _dot(pb, jnp.concatenate([split_heads(pb, m0), tt_], axis=1))
                p[c] = out[:, 0:LANES]
                tinv[c] = tinv[c] + out[:, LANES:]
            else:
                tinv[c] = tinv[c] + _dot(pb, split_heads(tinv[c].astype(bf16), m0))

    def stage_apply(chs):
        for c in chs:
            rhs = split_heads(jnp.concatenate([av[c], rows(at, c)], axis=1).astype(bf16), m0w)
            x[c] = _dot(tinv[c].astype(bf16), rhs)

    def stage_fold(chs):
        for c in chs:
            xb = x[c].astype(bf16)
            uv = xb[:, 0:LANES]
            ap = xb[:, LANES:]
            vb = rows(v, c).astype(bf16)
            tnl = jnp.concatenate([rows(bt, c) * wcs[c], rows(kt, c) * wcs[c]], axis=0).astype(bf16)
            tnr = jnp.concatenate(
                [jnp.concatenate([ap, uv], axis=1), jnp.concatenate([zb64, vb], axis=1)], axis=0)
            mn = jnp.where(bdmask, _dot_tn(tnl, tnr), 0.0)
            mc[c] = mn[:, 0:LANES] + jnp.where(eye, jnp.broadcast_to(wcs[c], (LANES, LANES)), 0.0)
            nc[c] = mn[:, LANES:]
            l2 = g[c][CHUNK:].astype(bf16)
            r2 = jnp.concatenate([
                split_heads(jnp.concatenate([ap, uv], axis=1), m0w),
                split_heads(jnp.concatenate([zb64, vb], axis=1), m0w)], axis=0)
            ro = _dot(l2, r2)
            rp[c] = rows(rt, c) + ro[:, 0:LANES]
            ov[c] = ro[:, LANES:]

    def carry(c, st):
        seq = _dot(jnp.concatenate([rp[c], mc[c]], axis=0).astype(bf16), st.astype(bf16))
        return seq[0:CHUNK] + ov[c], seq[CHUNK:] + nc[c]

    chs = list(range(n))
    stages = ([functools.partial(stage_scores, chs)]
              + [functools.partial(stage_level, chs, lev) for lev in range(6)]
              + [functools.partial(stage_apply, chs), functools.partial(stage_fold, chs)])
    return stages, carry


def _rwkv_body(*refs, tt, npr, lock):
    ng = len(POOL_WINDOWS)
    zr_ref, zk_ref, zv_ref, zg_ref, zl_ref, pp_ref, wl_ref = refs[0:7]
    pu_refs = [r.at[0] for r in refs[7:7 + ng]]
    pg_refs = [r.at[0] for r in refs[7 + ng:7 + 2 * ng]]
    pw_ref, psc_ref, y_ref, s_ref, yc_ref, sh_scr, st_scr, pool_scr = refs[7 + 2 * ng:]
    tb = pl.program_id(2)
    nt = pl.num_programs(2)
    nch = tt // CHUNK
    nsh = 4 * npr + 1
    pad = max(POOL_WINDOWS)

    @pl.when(tb == 0)
    def _():
        st_scr[...] = jnp.zeros_like(st_scr)
        for i in range(nsh):
            sh_scr[i, 7:8, :] = jnp.zeros((1, LANES), f32)
        for gi in range(ng):
            pool_scr[gi, 0:pad, :] = jnp.zeros((pad, LANES), f32)

    @pl.when(tb > 0)
    def _():
        for i in range(nsh):
            sh_scr[i, 7:8, :] = sh_scr[i, 7 + tt:8 + tt, :]
        for gi in range(ng):
            pool_scr[gi, 0:pad, :] = pool_scr[gi, tt:tt + pad, :]

    def pooling():
        pos = (tb * tt + lax.broadcasted_iota(jnp.int32, (tt, 1), 0)).astype(f32)
        for gi, w in enumerate(POOL_WINDOWS):
            u = pu_refs[gi][0]
            pool_scr[gi, pad:pad + tt, :] = u
            acc = u
            for k in range(1, w):
                acc = acc + pool_scr[gi, pl.ds(pad - k, tt), :]
            d = acc / jnp.minimum(float(w), pos + 1.0) - u
            dd = _dot(d.astype(bf16), pw_ref[gi].astype(bf16)) * psc_ref[:, gi * LANES:(gi + 1) * LANES]
            yc_ref[0, :, gi * LANES:(gi + 1) * LANES] = (dd * _silu(pg_refs[gi][0])).astype(bf16)
            yield

    def shifted(idx, z, mu):
        sh_scr[idx, 8:8 + tt, :] = z
        zprev = sh_scr[idx, pl.ds(7, tt), :]
        return z + (zprev - z) * mu

    m0t = _pair_consts(tt)
    ones_bd = _ones_blockdiag()
    lo = shifted(4 * npr, zl_ref[0, 0], pp_ref[0, _P_MU + 4:_P_MU + 5])
    xl = jnp.where(m0t, jnp.tanh(lo), lo).astype(bf16)

    m0 = _pair_consts(CHUNK)
    gr = lax.broadcasted_iota(jnp.int32, (2 * CHUNK, 2 * LANES), 0)
    gc = lax.broadcasted_iota(jnp.int32, (2 * CHUNK, 2 * LANES), 1)
    gt = gr % CHUNK
    gs = gc % CHUNK
    gmask = (gt > gs) | ((gr >= CHUNK) & (gt == gs))
    br = lax.broadcasted_iota(jnp.int32, (LANES, 2 * LANES), 0)
    bc = lax.broadcasted_iota(jnp.int32, (LANES, 2 * LANES), 1)
    bdmask = (br < HEAD) == ((bc % LANES) < HEAD)
    er = lax.broadcasted_iota(jnp.int32, (LANES, LANES), 0)
    ec = lax.broadcasted_iota(jnp.int32, (LANES, LANES), 1)
    eye = er == ec
    consts = (m0, gmask, bdmask, eye)
    tr = lax.broadcasted_iota(jnp.int32, (CHUNK, CHUNK), 0)
    tc = lax.broadcasted_iota(jnp.int32, (CHUNK, CHUNK), 1)
    tril = jnp.where(tc <= tr, 1.0, 0.0).astype(bf16)
    tril3 = jnp.concatenate([tril, tril, tril], axis=1)

    def prepare(s, out):
        pp = pp_ref[s]
        r, k, v, g = [shifted(4 * s + i, ref[s, 0], pp[_P_MU + i:_P_MU + i + 1])
                      for i, ref in enumerate((zr_ref, zk_ref, zv_ref, zg_ref))]
        la = _dot(xl, wl_ref[s].astype(bf16))
        kk = k * pp[_P_KK:_P_KK + 1]
        kk_ss = _segsum(kk * kk, ones_bd)
        yield
        ld = -_DECAY_SCALE * jax.nn.sigmoid(pp[_P_W0:_P_W0 + 1] + la[:, 0:LANES])
        a = jax.nn.sigmoid(pp[_P_A0:_P_A0 + 1] + la[:, LANES:])
        kk = kk * jnp.minimum(lax.rsqrt(kk_ss), 1e12)
        k2 = k * (1.0 + (a - 1.0) * pp[_P_KA:_P_KA + 1])
        h3 = _split3(ld)
        cl = jnp.concatenate([
            _dot(tril3, jnp.concatenate([h[c * CHUNK:(c + 1) * CHUNK] for h in h3], axis=0))
            for c in range(nch)], axis=0)
        bonus = _segsum(r * k2 * pp[_P_RK:_P_RK + 1], ones_bd) * v
        yield
        e_in = jnp.exp(cl)
        e_neg = jnp.exp(-cl)
        wcs = [e_in[(c + 1) * CHUNK - 1:(c + 1) * CHUNK] for c in range(nch)]
        out["stages"], out["carry"] = _rwkv_chunk_stages(
            -kk * jnp.exp(cl - ld), kk * a * e_neg, k2 * e_neg, r * e_in, v, wcs, consts)
        out["epi"] = (pp, g, bonus)

    def finish(s, outs, st, epi):
        pp, g, bonus = epi
        st_scr[s] = st
        o = jnp.concatenate(outs, axis=0)
        mean = _segsum(o, ones_bd) * (1.0 / HEAD)
        yield
        dl = o - mean
        var = _segsum(dl * dl, ones_bd) * (1.0 / HEAD)
        yield
        on = dl * lax.rsqrt(var + GN_EPS) * pp[_P_LNW:_P_LNW + 1] + pp[_P_LNB:_P_LNB + 1] + bonus
        y_ref[0, :, s * LANES:(s + 1) * LANES] = (on * _silu(g)).astype(bf16)

    def drain(gen):
        for _ in gen:
            pass

    def lockstep(gens):
        gens = list(gens)
        while gens:
            gens = [g for g in gens if next(g, StopIteration) is not StopIteration]
            yield

    preps = [dict() for _ in range(npr)]
    units = [list(range(i, min(i + lock, npr))) for i in range(0, npr, lock)]
    drain(lockstep(prepare(s, preps[s]) for s in units[0]))
    prev = None
    closing = iter(())
    side = pooling()
    for ui, unit in enumerate(units):
        nxt = (lockstep(prepare(s, preps[s]) for s in units[ui + 1]) if ui + 1 < len(units) else iter(()))
        for stage_row in zip(*[preps[s]["stages"] for s in unit]):
            for stage in stage_row:
                stage()
            if prev is not None:
                for cur in prev:
                    if cur["todo"]:
                        cur["step"]()
            next(closing, None)
            next(nxt, None)
            if ui > 0:
                next(side, None)
        drain(closing)
        drain(nxt)
        if prev is not None:
            while any(cur["todo"] for cur in prev):
                for cur in prev:
                    if cur["todo"]:
                        cur["step"]()
            closing = lockstep(finish(cur["s"], cur["outs"], cur["st"][0], cur["epi"]) for cur in prev)
        prev = []
        for s in unit:
            cur = {"s": s, "outs": [], "st": [st_scr[s]], "epi": preps[s]["epi"], "todo": list(range(nch))}

            def step(cur=cur, carry=preps[s]["carry"]):
                o_c, cur["st"][0] = carry(cur["todo"].pop(0), cur["st"][0])
                cur["outs"].append(o_c)

            cur["step"] = step
            prev.append(cur)
    while any(cur["todo"] for cur in prev):
        for cur in prev:
            if cur["todo"]:
                cur["step"]()
        next(closing, None)
    drain(closing)
    drain(side)
    drain(lockstep(finish(cur["s"], cur["outs"], cur["st"][0], cur["epi"]) for cur in prev))

    @pl.when(tb == nt - 1)
    def _():
        for s in range(npr):
            stt = st_scr[s].T
            s_ref[0, 2 * s] = stt[0:HEAD, 0:HEAD]
            s_ref[0, 2 * s + 1] = stt[HEAD:, HEAD:]


def _rwkv_pool_prompt(z3, pp, wl, pw, psc, layer, *, tt=512, npr=6, lock=2):
    _, b, t, _ = z3.shape
    npair = RWKV_HEADS // 2
    ngrp = npair // npr
    assert ngrp == 1, "the pooling columns are produced once per (batch row, position tile)"
    wide = npr * LANES
    ng = len(POOL_WINDOWS)

    def zspec(part):
        return pl.BlockSpec((npr, 1, tt, LANES), lambda bi, p, tb, part=part: (part * ngrp + p, bi, tb, 0))

    lora_blk = 4 * RWKV_W // LANES
    return pl.pallas_call(
        functools.partial(_rwkv_body, tt=tt, npr=npr, lock=lock),
        grid=(b, ngrp, t // tt),
        in_specs=[
            zspec(0), zspec(1), zspec(2), zspec(3),
            pl.BlockSpec((1, 1, tt, LANES), lambda bi, p, tb: (lora_blk, bi, tb, 0)),
            pl.BlockSpec((None, npr, 12, LANES), lambda bi, p, tb: (layer, p, 0, 0)),
            pl.BlockSpec((None, npr, LANES, 2 * LANES), lambda bi, p, tb: (layer, p, 0, 0)),
        ]
        + [pl.BlockSpec((1, 1, tt, LANES), lambda bi, p, tb, off=POOL_BLK0 + i: (off, bi, tb, 0))
           for i in range(2 * ng)]
        + [
            pl.BlockSpec((None, ng, LANES, LANES), lambda bi, p, tb: (layer, 0, 0, 0)),
            pl.BlockSpec((None, 1, POOL_W), lambda bi, p, tb: (layer, 0, 0)),
        ],
        out_specs=[
            pl.BlockSpec((1, tt, wide), lambda bi, p, tb: (bi, tb, p)),
            pl.BlockSpec((1, 2 * npr, HEAD, HEAD), lambda bi, p, tb: (bi, p, 0, 0)),
            pl.BlockSpec((1, tt, POOL_W), lambda bi, p, tb: (bi, tb, 0)),
        ],
        out_shape=[jax.ShapeDtypeStruct((b, t, RWKV_W), bf16),
                   jax.ShapeDtypeStruct((b, RWKV_HEADS, HEAD, HEAD), f32),
                   jax.ShapeDtypeStruct((b, t, POOL_W), bf16)],
        scratch_shapes=[pltpu.VMEM((4 * npr + 1, tt + 8, LANES), f32), pltpu.VMEM((npr, LANES, LANES), f32),
                        pltpu.VMEM((ng, tt + max(POOL_WINDOWS), LANES), f32)],
        compiler_params=pltpu.CompilerParams(
            dimension_semantics=("arbitrary", "arbitrary", "arbitrary"), vmem_limit_bytes=VMEM_LIMIT),
        name="rwkv_pool_prompt",
    )(*([z3] * 5), pp, wl, *([z3] * (2 * ng)), pw, psc)


def _rope_pair(x, cos, sa, sb):
    return x * cos + pltpu.roll(x, LANES - ROPE_DIMS // 2, 1) * sa + pltpu.roll(x, ROPE_DIMS // 2, 1) * sb


def _attn_body(*refs, t):
    qkvg = [[r.at[0] for r in refs[4 * gi:4 * gi + 4]] for gi in range(3)]
    cos_ref, sin_ref, qnw_ref, knw_ref = refs[12:16]
    y_refs = refs[16:19]
    kv_refs = [refs[19 + 2 * gi:21 + 2 * gi] for gi in range(3)]
    qn_scr, kn_scr, o_scr, lse_scr = refs[25:29]

    rb = 256
    ones_bd = _ones_blockdiag()
    m0q = _pair_consts(QBLK)
    scale = HEAD ** -0.5
    half = ROPE_DIMS // 2
    pj = lax.broadcasted_iota(jnp.int32, (LANES, LANES), 0)
    pi = lax.broadcasted_iota(jnp.int32, (LANES, LANES), 1)
    pin = pi % HEAD
    perm = jnp.where(((pin < half) & (pj == pi + half)) | ((pin >= half) & (pin < ROPE_DIMS) & (pj == pi - half)),
                     1.0, 0.0).astype(bf16)
    zpad = jnp.zeros((LANES, LANES), bf16)
    sum_swap = jnp.concatenate([jnp.concatenate([ones_bd, zpad], axis=1),
                                jnp.concatenate([zpad, perm], axis=1)], axis=0)

    for gi, (window, dil) in enumerate(ATT_GROUPS):
        q_ref, k_ref, v_ref, g_ref = qkvg[gi]
        pk_ref, pv_ref = kv_refs[gi]
        keep = min(window, t)

        def norm_rows(i, carry, q_ref=q_ref, k_ref=k_ref):
            jobs = []
            for u in range(4):
                rows = pl.ds(pl.multiple_of((4 * i + u) * rb, rb), rb)
                for src, nw, dst in ((q_ref, qnw_ref, qn_scr), (k_ref, knw_ref, kn_scr)):
                    x = src[0, rows, :]
                    y = x * nw[...]
                    hi, lo = _split2(jnp.concatenate([x * x, y], axis=1))
                    jobs.append((y, _dot(hi, sum_swap) + _dot(lo, sum_swap), dst, rows))
            for y, res, dst, rows in jobs:
                rs = lax.rsqrt(res[:, 0:LANES] * (1.0 / HEAD) + NORM_EPS)
                dst[rows, :] = (y * cos_ref[rows, :] + res[:, LANES:] * sin_ref[rows, :]) * rs
            return carry

        lax.fori_loop(0, t // (4 * rb), norm_rows, 0)
        pk_ref[0] = kn_scr[t - keep:t, :]
        pv_ref[0] = v_ref[0, t - keep:t, :]

        n_sub = t // dil
        n_blk = n_sub // QBLK

        def tiles(starts, nk, first, v_ref=v_ref, dil=dil, gi=gi):
            qi = lax.broadcasted_iota(jnp.int32, (QBLK, nk), 0)
            kj = lax.broadcasted_iota(jnp.int32, (QBLK, nk), 1)
            mask = (kj <= qi) if first else ((kj >= qi) & (kj <= qi + QBLK))
            scores, vts = [], []
            for q0, k0 in starts:
                qt = qn_scr[pl.ds(q0, QBLK, stride=dil), :] * scale
                kt = kn_scr[pl.ds(k0, nk, stride=dil), :].astype(bf16)
                vts.append(v_ref[0, pl.ds(k0, nk, stride=dil), :].astype(bf16))
                for hh in range(2):
                    qh = jnp.where(m0q, qt, 0.0) if hh == 0 else jnp.where(m0q, 0.0, qt)
                    scores.append(_dot_nt(qh.astype(bf16), kt))
            probs, sums, lses = [], [], []
            for s in scores:
                s = jnp.where(mask, s, -jnp.inf)
                m = jnp.max(s, axis=-1, keepdims=True)
                p = jnp.exp(s - m)
                l = jnp.sum(p, axis=-1, keepdims=True)
                probs.append(p.astype(bf16))
                sums.append(l)
                lses.append(m + jnp.log(l))
            for ti, (q0, _) in enumerate(starts):
                oh = [_dot(probs[2 * ti + hh], vts[ti]) / sums[2 * ti + hh] for hh in range(2)]
                o_scr[gi, pl.ds(q0, QBLK, stride=dil), :] = jnp.where(m0q, oh[0], oh[1])
                lse_scr[gi, pl.ds(q0, QBLK, stride=dil), :] = jnp.where(
                    m0q, jnp.broadcast_to(lses[2 * ti], (QBLK, LANES)),
                    jnp.broadcast_to(lses[2 * ti + 1], (QBLK, LANES)))

        per_first = min(8, dil)

        def first_tiles(i, carry, tiles=tiles, per=per_first):
            tiles([(i * per + u, i * per + u) for u in range(per)], QBLK, True)
            return carry

        lax.fori_loop(0, dil // per_first, first_tiles, 0)

        if n_blk > 1:
            n_later = dil * (n_blk - 1)
            per_later = max(p for p in range(1, 7) if n_later % p == 0)

            def later_tiles(i, carry, tiles=tiles, dil=dil, per=per_later):
                starts = []
                for u in range(per):
                    idx = i * per + u
                    r = idx % dil
                    blk = idx // dil + 1
                    starts.append((r + blk * QBLK * dil, r + (blk - 1) * QBLK * dil))
                tiles(starts, 2 * QBLK, False)
                return carry

            lax.fori_loop(0, dil * (n_blk - 1) // per_later, later_tiles, 0)

    def combine(i, carry):
        rows = pl.ds(pl.multiple_of(i * rb, rb), rb)
        ls = [lse_scr[gi, rows, :] for gi in range(3)]
        mx = jnp.maximum(jnp.maximum(ls[0], ls[1]), ls[2])
        es = [jnp.exp(l - mx) for l in ls]
        inv = 1.0 / (es[0] + es[1] + es[2])
        for gi in range(3):
            gate = qkvg[gi][3][0, rows, :]
            y_refs[gi][0, rows, :] = (o_scr[gi, rows, :] * (es[gi] * inv) * _silu(gate)).astype(bf16)
        return carry

    lax.fori_loop(0, t // rb, combine, 0)


def _attn_prompt(z3, cos, sin, qnw, knw, layer):
    _, b, t, _ = z3.shape
    in_specs = []
    for gi in range(3):
        for part in range(4):
            off = ATT_BLK0 + part * 6 + 2 * gi
            in_specs.append(pl.BlockSpec((1, 1, t, LANES), lambda bi, jp, off=off: (off + jp, bi, 0, 0)))
    in_specs += [pl.BlockSpec((t, LANES), lambda bi, jp: (0, 0))] * 2
    in_specs += [pl.BlockSpec((None, 1, LANES), lambda bi, jp: (layer, 0, 0))] * 2
    out_specs = [pl.BlockSpec((1, t, LANES), lambda bi, jp: (bi, 0, jp))] * 3
    out_shape = [jax.ShapeDtypeStruct((b, t, 4 * HEAD), bf16)] * 3
    for window, _ in ATT_GROUPS:
        keep = min(window, t)
        out_specs += [pl.BlockSpec((1, keep, LANES), lambda bi, jp: (bi, 0, jp))] * 2
        out_shape += [jax.ShapeDtypeStruct((b, keep, 4 * HEAD), f32)] * 2
    outs = pl.pallas_call(
        functools.partial(_attn_body, t=t),
        grid=(b, 2),
        in_specs=in_specs,
        out_specs=out_specs,
        out_shape=out_shape,
        scratch_shapes=[pltpu.VMEM((t, LANES), f32), pltpu.VMEM((t, LANES), f32),
                        pltpu.VMEM((3, t, LANES), f32), pltpu.VMEM((3, t, LANES), f32)],
        compiler_params=pltpu.CompilerParams(
            dimension_semantics=("arbitrary", "arbitrary"), vmem_limit_bytes=VMEM_LIMIT),
        name="attn_prompt",
    )(*([z3] * 12), cos, sin, qnw, knw)
    return outs


def _bcast8(x):
    return jnp.broadcast_to(x, (8, x.shape[-1]))


def _sample_body(*refs, past_len, nbs):
    gens = [_sample_one(bi, *refs, past_len=past_len) for bi in range(nbs)]
    while gens:
        gens = [g for g in gens if next(g, StopIteration) is not StopIteration]


def _sample_one(bi, z_ref, sh_ref, wkv_ref, pool_ref, ck0, cv0, ck1, cv1, ck2, cv2,
                mu_ref, rp_ref, wup_ref, aup_ref, qnw_ref, knw_ref, cos_ref, sa_ref, sb_ref,
                pw_ref, psc_ref,
                mix_ref, swkv_ref, spool_ref, sk0, sv0, sk1, sv1, sk2, sv2,
                qk_scr, *, past_len):
    z = z_ref[bi]
    za = z[:, 0:RWKV_COLS]
    zb = z[:, RWKV_COLS:RWKV_COLS + ATT_COLS]
    zc = z[:, RWKV_COLS + ATT_COLS:RWKV_COLS + ATT_COLS + 2 * POOL_W]
    er = lax.broadcasted_iota(jnp.int32, (HEAD, HEAD), 0)
    ec = lax.broadcasted_iota(jnp.int32, (HEAD, HEAD), 1)
    eye = er == ec

    zs = za + (sh_ref[bi] - za) * mu_ref[...]
    w_ = RWKV_W
    r, k, v, g = (zs[:, i * w_:(i + 1) * w_] for i in range(4))
    w_dn = zs[:, 4 * w_:4 * w_ + LORA]
    a_dn = zs[:, 4 * w_ + LORA:]
    rp = rp_ref[...]
    w0, a0, k_k, k_a, r_k, ln_w, ln_b = (rp[i:i + 1] for i in range(7))
    lw = _dot(_bcast8(jnp.tanh(w_dn)).astype(bf16), wup_ref[...].astype(bf16))[0:1]
    la = _dot(_bcast8(a_dn).astype(bf16), aup_ref[...].astype(bf16))[0:1]
    yield
    decay = jnp.exp(-_DECAY_SCALE * jax.nn.sigmoid(w0 + lw))
    a = jax.nn.sigmoid(a0 + la)
    kk = k * k_k
    k2 = k * (1.0 + (a - 1.0) * k_a)
    heads = range(RWKV_HEADS)
    hsl = [slice(h * HEAD, (h + 1) * HEAD) for h in heads]

    def lane_sum(x):
        return jnp.sum(x, axis=-1, keepdims=True)

    kk_ss = [lane_sum(kk[:, hs] * kk[:, hs]) for hs in hsl]
    bonus = [lane_sum(r[:, hs] * k2[:, hs] * r_k[:, hs]) for hs in hsl]
    v_col = [lane_sum(jnp.where(eye, jnp.broadcast_to(v[:, hs], (HEAD, HEAD)), 0.0)) for hs in hsl]
    yield
    kkn = [kk[:, hs] / jnp.maximum(jnp.sqrt(ss), 1e-12) for hs, ss in zip(hsl, kk_ss)]
    sa_col = [lane_sum(wkv_ref[bi, h] * (-kkn[h])) for h in heads]
    yield
    sn = [wkv_ref[bi, h] * decay[:, hsl[h]] + sa_col[h] * (kkn[h] * a[:, hsl[h]]) + v_col[h] * k2[:, hsl[h]]
          for h in heads]
    for h in heads:
        swkv_ref[bi, h] = sn[h]
    o_col = [lane_sum(sn[h] * r[:, hsl[h]]) for h in heads]
    yield
    o = [jnp.sum(jnp.where(eye, jnp.broadcast_to(oc, (HEAD, HEAD)), 0.0), axis=0, keepdims=True) for oc in o_col]
    mean = [jnp.mean(x, axis=-1, keepdims=True) for x in o]
    yield
    dl = [x - mu_ for x, mu_ in zip(o, mean)]
    var = [jnp.mean(x * x, axis=-1, keepdims=True) for x in dl]
    yield
    for h in heads:
        hs = hsl[h]
        on = dl[h] * lax.rsqrt(var[h] + GN_EPS) * ln_w[:, hs] + ln_b[:, hs] + bonus[h] * v[:, hs]
        mix_ref[bi, :, hs] = on * _silu(g[:, hs])

    aw = ATT_W
    q, kx, vx, gx = (zb[:, i * aw:(i + 1) * aw] for i in range(4))
    m0 = _pair_consts(8)
    for hp in range(ATT_W // LANES):
        ls = slice(hp * LANES, (hp + 1) * LANES)
        for idx, (src, nw) in enumerate(((q, qnw_ref), (kx, knw_ref))):
            x = _bcast8(src[:, ls])
            sq = x * x
            s0 = jnp.sum(jnp.where(m0, sq, 0.0), axis=-1, keepdims=True)
            s1 = jnp.sum(jnp.where(m0, 0.0, sq), axis=-1, keepdims=True)
            ms = jnp.where(m0, s0, s1) * (1.0 / HEAD)
            xn = x * lax.rsqrt(ms + NORM_EPS) * nw[...]
            qk_scr[bi, idx, :, ls] = _rope_pair(xn, cos_ref[...], sa_ref[...], sb_ref[...])
    yield
    qn = qk_scr[bi, 0, 0:1, :]
    kn = qk_scr[bi, 1, 0:1, :]
    scale = HEAD ** -0.5

    def heads4(row, gi):
        return jnp.concatenate([row[:, (gi * 4 + j) * HEAD:(gi * 4 + j + 1) * HEAD] for j in range(4)], axis=0)

    def to_col(row):
        return jnp.sum(jnp.where(eye, jnp.broadcast_to(row, (HEAD, HEAD)), 0.0), axis=-1, keepdims=True)

    def to_row(col):
        return jnp.sum(jnp.where(eye, jnp.broadcast_to(col, (HEAD, HEAD)), 0.0), axis=0, keepdims=True)

    caches = ((ck0, cv0, sk0, sv0), (ck1, cv1, sk1, sv1), (ck2, cv2, sk2, sv2))
    windows = []
    for gi, (ck, cv, sk, sv) in enumerate(caches):
        sk[bi, 0] = heads4(kn, gi)
        sv[bi, 0] = heads4(vx, gi)
        row_id = lax.broadcasted_iota(jnp.int32, (1, ck.shape[-1]), 1)
        windows.append((row_id % ATT_GROUPS[gi][1]) == 0)
    combos = [(gi, j) for gi in range(3) for j in range(4)]
    asl = [slice((gi * 4 + j) * HEAD, (gi * 4 + j + 1) * HEAD) for gi, j in combos]
    qh = [qn[:, hs] * scale for hs in asl]
    q_col = [to_col(x) for x in qh]
    s_new = [jnp.sum(kn[:, hs] * x, axis=-1, keepdims=True) for hs, x in zip(asl, qh)]
    yield
    s_all = [jnp.where(windows[gi], jnp.sum(caches[gi][0][bi, j] * qc, axis=0, keepdims=True), -jnp.inf)
             for (gi, j), qc in zip(combos, q_col)]
    m_all = [jnp.maximum(jnp.max(s, axis=-1, keepdims=True), sn_) for s, sn_ in zip(s_all, s_new)]
    yield
    p_all = [jnp.exp(s - m) for s, m in zip(s_all, m_all)]
    p_new = [jnp.exp(sn_ - m) for sn_, m in zip(s_new, m_all)]
    l_all = [jnp.sum(p, axis=-1, keepdims=True) + pn for p, pn in zip(p_all, p_new)]
    o_colv = [jnp.sum(caches[gi][1][bi, j] * p, axis=-1, keepdims=True) for (gi, j), p in zip(combos, p_all)]
    yield
    o_rows = [(to_row(oc) + pn * vx[:, hs]) / l for oc, pn, hs, l in zip(o_colv, p_new, asl, l_all)]
    lse = [m + jnp.log(l) for m, l in zip(m_all, l_all)]
    for j in range(4):
        mx = jnp.maximum(jnp.maximum(lse[j], lse[4 + j]), lse[8 + j])
        es = [jnp.exp(lse[gi * 4 + j] - mx) for gi in range(3)]
        inv = 1.0 / (es[0] + es[1] + es[2])
        for gi in range(3):
            ci = gi * 4 + j
            mix_ref[bi, :, RWKV_W + ci * HEAD:RWKV_W + (ci + 1) * HEAD] = (
                o_rows[ci] * (es[gi] * inv) * _silu(gx[:, asl[ci]]))

    u = zc[:, 0:POOL_W]
    gate = zc[:, POOL_W:]
    prev = pool_ref[bi]
    for gi, w in enumerate(POOL_WINDOWS):
        cs = slice(gi * LANES, (gi + 1) * LANES)
        ug = u[:, cs]
        sw = jnp.sum(prev[POOL_BUF - (w - 1):POOL_BUF, cs], axis=0, keepdims=True) + ug
        cnt = min(float(w), float(past_len) + 1.0)
        d = sw / cnt - ug
        dd = _dot(_bcast8(d).astype(bf16), pw_ref[gi].astype(bf16))[0:1] * psc_ref[:, cs]
        mix_ref[bi, :, RWKV_W + ATT_W + gi * LANES:RWKV_W + ATT_W + (gi + 1) * LANES] = dd * _silu(gate[:, cs])
    spool_ref[bi, 0:POOL_BUF - 1, :] = prev[1:POOL_BUF]
    spool_ref[bi, POOL_BUF - 1:POOL_BUF, :] = u


def _sample_step(zs, sh_all, wkv_all, pool_all, caches_all, layer, mu, rp, wup, aup, qnw, knw, cos, sa, sb,
                 pw, psc, *, past_len, nbs=2):
    nb = zs.shape[0]
    d_in = zs.shape[-1]
    d_mix = RWKV_W + ATT_W + POOL_W

    def full(shape):
        nd = len(shape)
        return pl.BlockSpec(shape, lambda b, nd=nd: (0,) * nd)

    assert nb % nbs == 0
    in_specs = [
        pl.BlockSpec((nbs, 1, d_in), lambda b: (b, 0, 0)),
        pl.BlockSpec((None, nbs, 1, RWKV_COLS), lambda b: (layer, b, 0, 0)),
        pl.BlockSpec((None, nbs, RWKV_HEADS, HEAD, HEAD), lambda b: (layer, b, 0, 0, 0)),
        pl.BlockSpec((None, nbs, POOL_BUF, POOL_W), lambda b: (layer, b, 0, 0)),
    ]
    in_specs += [pl.BlockSpec((None, nbs, 4, HEAD, c.shape[-1]), lambda b: (layer, b, 0, 0, 0))
                 for c in caches_all]

    def of_layer(x):
        nd = x.ndim - 1
        return pl.BlockSpec((None,) + x.shape[1:], lambda b, nd=nd: (layer,) + (0,) * nd)

    in_specs += [of_layer(mu), of_layer(rp), of_layer(wup), of_layer(aup), of_layer(qnw), of_layer(knw),
                 full(cos.shape), full(sa.shape), full(sb.shape), of_layer(pw), of_layer(psc)]
    out_specs = [
        pl.BlockSpec((nbs, 1, d_mix), lambda b: (b, 0, 0)),
        pl.BlockSpec((nbs, RWKV_HEADS, HEAD, HEAD), lambda b: (b, 0, 0, 0)),
        pl.BlockSpec((nbs, POOL_BUF, POOL_W), lambda b: (b, 0, 0)),
    ] + [pl.BlockSpec((nbs, 1, 4, HEAD), lambda b: (b, 0, 0, 0))] * 6
    out_shape = [
        jax.ShapeDtypeStruct((nb, 1, d_mix), f32),
        jax.ShapeDtypeStruct((nb, RWKV_HEADS, HEAD, HEAD), f32),
        jax.ShapeDtypeStruct((nb, POOL_BUF, POOL_W), f32),
    ] + [jax.ShapeDtypeStruct((nb, 1, 4, HEAD), f32)] * 6
    return pl.pallas_call(
        functools.partial(_sample_body, past_len=past_len, nbs=nbs),
        grid=(nb // nbs,),
        in_specs=in_specs,
        out_specs=out_specs,
        out_shape=out_shape,
        scratch_shapes=[pltpu.VMEM((nbs, 2, 8, ATT_W), f32)],
        compiler_params=pltpu.CompilerParams(
            dimension_semantics=("arbitrary",), vmem_limit_bytes=VMEM_LIMIT),
        name="sample_step",
    )(zs, sh_all, wkv_all, pool_all, *caches_all, mu, rp, wup, aup, qnw, knw, cos, sa, sb, pw, psc)


def _rope_tables(pos):
    half = ROPE_DIMS // 2
    inv = jnp.power(jnp.float32(ROPE_THETA), -jnp.arange(half, dtype=f32) * 2.0 / ROPE_DIMS)
    ang = pos[:, None] * inv[None, :]
    cos, sin = jnp.cos(ang), jnp.sin(ang)
    n = pos.shape[0]
    pad = jnp.zeros((n, HEAD - ROPE_DIMS), f32)
    zero = jnp.zeros((n, half), f32)
    c_head = jnp.concatenate([cos, cos, pad + 1.0], axis=1)
    a_head = jnp.concatenate([-sin, zero, pad], axis=1)
    b_head = jnp.concatenate([zero, sin, pad], axis=1)
    return tuple(jnp.concatenate([x, x], axis=1) for x in (c_head, a_head, b_head))


def kernel(x_prompt, x_sample, state_wkv, state_shift, state_pool,
           cache_k_w128, cache_v_w128, cache_k_w512, cache_v_w512, cache_k_w2048, cache_v_w2048,
           norm_w, w_in, w_out, rwkv_mu, rwkv_w0, rwkv_w_up, rwkv_a0, rwkv_a_up,
           rwkv_k_k, rwkv_k_a, rwkv_r_k, rwkv_ln_w, rwkv_ln_b, q_norm_w, k_norm_w, pool_w, pool_scale):
    b, t, d = x_prompt.shape
    nb, ts, _ = x_sample.shape
    depth = w_in.shape[0]
    d_in = w_in.shape[2]
    assert ts == 1 and t % 512 == 0
    past_len = PAST_LEN
    caches_in = ((cache_k_w128, cache_v_w128), (cache_k_w512, cache_v_w512), (cache_k_w2048, cache_v_w2048))
    for (window, _), (ck, _) in zip(ATT_GROUPS, caches_in):
        assert ck.shape[2] == window, "window buffers are expected to be full"

    cos_p, sa_p, sb_p = _rope_tables(jnp.arange(t, dtype=f32))
    cos_s, sa_s, sb_s = _rope_tables(past_len + jnp.arange(1, dtype=f32))

    hp = x_prompt.reshape(b * t, d)
    hs = x_sample.reshape(nb, d)
    npair = RWKV_HEADS // 2
    p_out = [[] for _ in range(9)]
    s_out = [[] for _ in range(9)]
    sh_all = state_shift.reshape(depth, nb, 1, RWKV_COLS)
    caches_all = [jnp.transpose(c, (0, 1, 3, 4, 2)) for pair in caches_in for c in pair]

    nw_all = norm_w.reshape(depth, 1, d)
    mu_all = rwkv_mu.reshape(depth, 1, RWKV_COLS)
    mu4 = rwkv_mu[:, :4 * RWKV_W].reshape(depth, 4, npair, LANES).transpose(0, 2, 1, 3)
    mul = jnp.broadcast_to(rwkv_mu[:, 4 * RWKV_W:].reshape(depth, 1, 1, LANES), (depth, npair, 1, LANES))
    vecs_all = jnp.stack([rwkv_w0, rwkv_a0, rwkv_k_k, rwkv_k_a, rwkv_r_k.reshape(depth, -1),
                          rwkv_ln_w, rwkv_ln_b], axis=1)
    pp_all = jnp.concatenate(
        [mu4, mul, vecs_all.reshape(depth, 7, npair, LANES).transpose(0, 2, 1, 3)], axis=2)
    wup = rwkv_w_up.reshape(depth, LORA, npair, LANES).transpose(0, 2, 1, 3)
    aup = rwkv_a_up.reshape(depth, LORA, npair, LANES).transpose(0, 2, 1, 3)
    zl = jnp.zeros_like(wup)
    wl_all = jnp.concatenate(
        [jnp.concatenate([wup, zl], axis=3), jnp.concatenate([zl, aup], axis=3)], axis=2)
    qnw_all = jnp.tile(q_norm_w, (1, 2)).reshape(depth, 1, LANES)
    knw_all = jnp.tile(k_norm_w, (1, 2)).reshape(depth, 1, LANES)
    psc_all = pool_scale.reshape(depth, 1, POOL_W)
    sin_p = sa_p + sb_p

    for l in range(depth):
        zp, zs = _inproj(hp, hs, nw_all, w_in, l)
        z3 = zp.reshape(d_in // LANES, b, t, LANES)

        ya, p_wkv, yc = _rwkv_pool_prompt(z3, pp_all, wl_all, pool_w, psc_all, l)
        att = _attn_prompt(z3, cos_p, sin_p, qnw_all, knw_all, l)

        sample = _sample_step(
            zs.reshape(nb, 1, -1), sh_all, state_wkv, state_pool, caches_all, l,
            mu_all, vecs_all, rwkv_w_up, rwkv_a_up, qnw_all, knw_all, cos_s, sa_s, sb_s,
            pool_w, psc_all, past_len=past_len)
        mix_s = sample[0].reshape(nb, -1)

        ys = [ya.reshape(b * t, RWKV_W)] + [y.reshape(b * t, 4 * HEAD) for y in att[:3]] + [yc.reshape(b * t, POOL_W)]
        hp, hs = _outproj(hp, ys, hs, mix_s, w_out, l)

        p_out[0].append(p_wkv)
        p_out[1].append(z3[:ATT_BLK0, :, -1, :].transpose(1, 0, 2).reshape(b, RWKV_COLS))
        p_out[2].append(z3[POOL_BLK0:POOL_BLK0 + POOL_W // LANES, :, t - POOL_BUF:, :]
                        .transpose(1, 2, 0, 3).reshape(b, POOL_BUF, POOL_W))
        for gi in range(3):
            keep = att[3 + 2 * gi].shape[1]
            p_out[3 + 2 * gi].append(att[3 + 2 * gi].reshape(b, keep, 4, HEAD))
            p_out[4 + 2 * gi].append(att[4 + 2 * gi].reshape(b, keep, 4, HEAD))
        s_out[0].append(sample[1])
        s_out[1].append(zs[:, :RWKV_COLS])
        s_out[2].append(sample[2])
        for i in range(6):
            s_out[3 + i].append(sample[3 + i])

    return (hp.reshape(b, t, d), hs.reshape(nb, 1, d),
            *[jnp.stack(x) for x in p_out], *[jnp.stack(x) for x in s_out])
```
